```python
import jax, jax.numpy as jnp
from jax import lax
import numpy as np

D_MODEL = 2048
BATCH = 2
SEQ = 8192
DEPTH = 4

N_MEM = 256
N_MIXERS = 3
N_LAYERS_A = (DEPTH + 2) // 3
N_LAYERS_B = (DEPTH + 1) // 3
N_LAYERS_C = DEPTH // 3

CONV_A_WIDTH = 31
CONV_C_WIDTH = 3
N_FOURIER_GROUPS = 8
FOURIER_GROUP = D_MODEL // N_FOURIER_GROUPS
N_XATTN_HEADS = 4
XATTN_HEAD_DIM = D_MODEL // N_XATTN_HEADS
N_EXPERT_GROUPS = 8
EXPERTS_PER_GROUP = 4
N_EXPERTS = N_EXPERT_GROUPS * EXPERTS_PER_GROUP
TOP_K_IN_GROUP = 2
D_EXPERT = 512
EPS = 1e-6

kernel_name = "hybrid_conformer_fnet_shortconv_hmoe_encoder"


def rmsnorm(x, g):
    xf = x.astype(jnp.float32)
    y = xf * lax.rsqrt(jnp.mean(xf * xf, axis=-1, keepdims=True) + EPS)
    return (y * g.astype(jnp.float32)).astype(x.dtype)


def layernorm(x, g, b):
    xf = x.astype(jnp.float32)
    mu = jnp.mean(xf, axis=-1, keepdims=True)
    var = jnp.mean(jnp.square(xf - mu), axis=-1, keepdims=True)
    y = (xf - mu) * lax.rsqrt(var + EPS)
    return (y * g.astype(jnp.float32) + b.astype(jnp.float32)).astype(x.dtype)


def depthwise_conv_centred(u, w):
    k = w.shape[0]
    c = u.shape[-1]
    return lax.conv_general_dilated(
        u, w.astype(u.dtype)[:, None, :], window_strides=(1,),
        padding=[(k // 2, k // 2)], dimension_numbers=("NWC", "WIO", "NWC"),
        feature_group_count=c)


def conformer_conv(xn, w_in, w_conv, b_conv, g_ln, b_ln, w_out):
    a, gate = jnp.split(xn @ w_in, 2, axis=-1)
    u = a * jax.nn.sigmoid(gate)
    u = depthwise_conv_centred(u, w_conv) + b_conv.astype(u.dtype)
    u = jax.nn.silu(layernorm(u, g_ln, b_ln))
    return u @ w_out


def fourier_mix(xn, w_out):
    bsz, s, d = xn.shape
    xg = xn.astype(jnp.float32).reshape(bsz, s, N_FOURIER_GROUPS, FOURIER_GROUP)
    f = jnp.real(jnp.fft.fftn(xg, axes=(1, 3), norm="ortho"))
    return f.reshape(bsz, s, d).astype(xn.dtype) @ w_out


def short_gated_conv(xn, w_in, w_conv, w_out):
    b_gate, c_gate, v = jnp.split(xn @ w_in, 3, axis=-1)
    u = depthwise_conv_centred(c_gate * v, w_conv)
    return (b_gate * u) @ w_out


def memory_cross_attention(xn, memn, w_q, w_k, w_v, w_o):
    bsz, s, d = xn.shape
    q = (xn @ w_q).reshape(bsz, s, N_XATTN_HEADS, XATTN_HEAD_DIM)
    k = (memn @ w_k).reshape(bsz, N_MEM, N_XATTN_HEADS, XATTN_HEAD_DIM)
    v = (memn @ w_v).reshape(bsz, N_MEM, N_XATTN_HEADS, XATTN_HEAD_DIM)
    scores = jnp.einsum("bshd,bmhd->bhsm", q, k).astype(jnp.float32) * (XATTN_HEAD_DIM ** -0.5)
    p = jax.nn.softmax(scores, axis=-1).astype(xn.dtype)
    o = jnp.einsum("bhsm,bmhd->bshd", p, v).reshape(bsz, s, d)
    return o @ w_o


def hierarchical_moe(xn, w_rg, b_rg, w_re, b_re, w_gate_up, w_down):
    bsz, s, d = xn.shape
    t = bsz * s
    xt = xn.reshape(t, d)
    g_logit = (xt @ w_rg).astype(jnp.float32) + b_rg.astype(jnp.float32)
    grp = jnp.argmax(g_logit, axis=-1)
    g_w = jnp.take_along_axis(jax.nn.softmax(g_logit, axis=-1), grp[:, None], axis=1)[:, 0]
    e_logit = ((xt @ w_re).astype(jnp.float32) + b_re.astype(jnp.float32)).reshape(
        t, N_EXPERT_GROUPS, EXPERTS_PER_GROUP)
    e_sel = jnp.take_along_axis(e_logit, grp[:, None, None], axis=1)[:, 0]
    top_l, top_i = lax.top_k(e_sel, TOP_K_IN_GROUP)
    weight = g_w[:, None] * jax.nn.softmax(top_l, axis=-1)
    expert_id = (grp[:, None] * EXPERTS_PER_GROUP + top_i).reshape(-1)
    order = jnp.argsort(expert_id)
    tok = order // TOP_K_IN_GROUP
    sizes = jnp.bincount(expert_id, length=N_EXPERTS).astype(jnp.int32)
    xs = xt[tok]
    gate, up = jnp.split(lax.ragged_dot(xs, w_gate_up, sizes), 2, axis=-1)
    out = lax.ragged_dot(jax.nn.silu(gate) * up, w_down, sizes)
    out = (out.astype(jnp.float32) * weight.reshape(-1)[order][:, None]).astype(xn.dtype)
    y = jax.ops.segment_sum(out, tok, num_segments=t)
    return y.reshape(bsz, s, d)


def setup_inputs(seed: int = 0) -> dict:
    key = jax.random.key(seed)
    ks = iter(jax.random.split(key, 32))
    f32 = jnp.float32
    D = D_MODEL

    def nrm(shape, scale):
        return jax.random.normal(next(ks), shape, f32) * scale

    def gain(shape):
        return 1.0 + 0.02 * jax.random.normal(next(ks), shape, f32)

    return {
        "x": nrm((BATCH, SEQ, D), 1.0),
        "mem": nrm((BATCH, N_MEM, D), 1.0),
        "g_mix": gain((DEPTH, D)),
        "g_xattn": gain((DEPTH, D)),
        "g_mem": gain((DEPTH, D)),
        "g_moe": gain((DEPTH, D)),
        "g_final": gain((D,)),
        "w_a_in": nrm((N_LAYERS_A, D, 2 * D), D ** -0.5),
        "w_a_conv": nrm((N_LAYERS_A, CONV_A_WIDTH, D), CONV_A_WIDTH ** -0.5),
        "b_a_conv": nrm((N_LAYERS_A, D), 0.02),
        "g_a_ln": gain((N_LAYERS_A, D)),
        "b_a_ln": nrm((N_LAYERS_A, D), 0.02),
        "w_a_out": nrm((N_LAYERS_A, D, D), D ** -0.5),
        "w_b_out": nrm((N_LAYERS_B, D, D), D ** -0.5),
        "w_c_in": nrm((N_LAYERS_C, D, 3 * D), D ** -0.5),
        "w_c_conv": nrm((N_LAYERS_C, CONV_C_WIDTH, D), CONV_C_WIDTH ** -0.5),
        "w_c_out": nrm((N_LAYERS_C, D, D), D ** -0.5),
        "w_xq": nrm((DEPTH, D, D), D ** -0.5),
        "w_xk": nrm((DEPTH, D, D), D ** -0.5),
        "w_xv": nrm((DEPTH, D, D), D ** -0.5),
        "w_xo": nrm((DEPTH, D, D), D ** -0.5),
        "w_route_group": nrm((DEPTH, D, N_EXPERT_GROUPS), D ** -0.5),
        "b_route_group": nrm((DEPTH, N_EXPERT_GROUPS), 0.01),
        "w_route_expert": nrm((DEPTH, D, N_EXPERTS), D ** -0.5),
        "b_route_expert": nrm((DEPTH, N_EXPERTS), 0.01),
        "w_gate_up": nrm((DEPTH, N_EXPERTS, D, 2 * D_EXPERT), D ** -0.5),
        "w_down": nrm((DEPTH, N_EXPERTS, D_EXPERT, D), D_EXPERT ** -0.5),
    }


def reference(x, mem, g_mix, g_xattn, g_mem, g_moe, g_final,
              w_a_in, w_a_conv, b_a_conv, g_a_ln, b_a_ln, w_a_out,
              w_b_out,
              w_c_in, w_c_conv, w_c_out,
              w_xq, w_xk, w_xv, w_xo,
              w_route_group, b_route_group, w_route_expert, b_route_expert,
              w_gate_up, w_down):
    h = x
    for i in range(DEPTH):
        kind = i % N_MIXERS
        j = i // N_MIXERS
        hn = rmsnorm(h, g_mix[i])
        if kind == 0:
            mix = conformer_conv(hn, w_a_in[j], w_a_conv[j], b_a_conv[j],
                                 g_a_ln[j], b_a_ln[j], w_a_out[j])
        elif kind == 1:
            mix = fourier_mix(hn, w_b_out[j])
        else:
            mix = short_gated_conv(hn, w_c_in[j], w_c_conv[j], w_c_out[j])
        h = h + mix
        h = h + memory_cross_attention(rmsnorm(h, g_xattn[i]), rmsnorm(mem, g_mem[i]),
                                       w_xq[i], w_xk[i], w_xv[i], w_xo[i])
        h = h + hierarchical_moe(rmsnorm(h, g_moe[i]), w_route_group[i], b_route_group[i],
                                 w_route_expert[i], b_route_expert[i],
                                 w_gate_up[i], w_down[i])
    return rmsnorm(h, g_final)
```

```python
import functools
import math

import numpy as np
import jax
import jax.numpy as jnp
from jax import lax
from jax.experimental import pallas as pl
from jax.experimental.pallas import tpu as pltpu

F32 = jnp.float32
BF16 = jnp.bfloat16
EPS = 1e-6

N_MIXERS = 3
N_FOURIER_GROUPS = 8
N_HEADS = 4
N_GROUPS = 8
EXPERTS_PER_GROUP = 4
N_PAIRS = 6
N_BUCKETS = N_GROUPS * N_PAIRS
PAIR_LO = (0, 0, 0, 1, 1, 2)
PAIR_HI = (1, 2, 3, 2, 3, 3)

LANES = 128
SEQ_MINOR = 128
HALO = 16
VMEM_LIMIT_BYTES = 56 * 1024 * 1024

TM_DENSE = 512
TM_MOE = 256
ROW_CHUNK = 64
N_CHUNK = 512


def _params(n_axes):
    return pltpu.CompilerParams(
        dimension_semantics=("arbitrary",) * n_axes,
        vmem_limit_bytes=VMEM_LIMIT_BYTES)


def _resident(shape):
    nd = len(shape)
    return pl.BlockSpec(shape, lambda *_: (0,) * nd, pipeline_mode=pl.Buffered(1))


def _rms(x, g):
    return x * lax.rsqrt(jnp.mean(x * x, axis=-1, keepdims=True) + EPS) * g


def _sigmoid(x):
    return 1.0 / (1.0 + jnp.exp(-x))


def _dot(a, b):
    return jnp.dot(a, b, preferred_element_type=F32)


def _a_in_kernel(x_ref, g_ref, w_ref, u_ref, xn_ref):
    d = x_ref.shape[1]
    xn_ref[...] = _rms(x_ref[...], g_ref[...]).astype(BF16)
    for c in range(0, d, N_CHUNK):
        a = _dot(xn_ref[...], w_ref[:, c:c + N_CHUNK])
        gate = _dot(xn_ref[...], w_ref[:, d + c:d + c + N_CHUNK])
        u_ref[:, c:c + N_CHUNK] = (a * _sigmoid(gate)).astype(BF16)


def _a_in(h, g, w_bf16):
    t, d = h.shape
    return pl.pallas_call(
        _a_in_kernel,
        grid=(t // TM_DENSE,),
        in_specs=[pl.BlockSpec((TM_DENSE, d), lambda i: (i, 0)),
                  _resident((1, d)),
                  _resident((d, 2 * d))],
        out_specs=pl.BlockSpec((TM_DENSE, d), lambda i: (i, 0)),
        out_shape=jax.ShapeDtypeStruct((t, d), BF16),
        scratch_shapes=[pltpu.VMEM((TM_DENSE, d), BF16)],
        compiler_params=_params(1),
        name="conformer_in",
    )(h, g.reshape(1, d), w_bf16)


def _c_in_kernel(x_ref, g_ref, w_ref, b_ref, cv_ref, xn_ref):
    d = x_ref.shape[1]
    xn_ref[...] = _rms(x_ref[...], g_ref[...]).astype(BF16)
    for c in range(0, d, N_CHUNK):
        b_ref[:, c:c + N_CHUNK] = _dot(xn_ref[...], w_ref[:, c:c + N_CHUNK]).astype(BF16)
        cg = _dot(xn_ref[...], w_ref[:, d + c:d + c + N_CHUNK])
        v = _dot(xn_ref[...], w_ref[:, 2 * d + c:2 * d + c + N_CHUNK])
        cv_ref[:, c:c + N_CHUNK] = (cg * v).astype(BF16)


def _c_in(h, g, w_bf16):
    t, d = h.shape
    tile = pl.BlockSpec((TM_DENSE, d), lambda i: (i, 0))
    return pl.pallas_call(
        _c_in_kernel,
        grid=(t // TM_DENSE,),
        in_specs=[tile, _resident((1, d)), _resident((d, 3 * d))],
        out_specs=[tile, tile],
        out_shape=[jax.ShapeDtypeStruct((t, d), BF16)] * 2,
        scratch_shapes=[pltpu.VMEM((TM_DENSE, d), BF16)],
        compiler_params=_params(1),
        name="shortconv_in",
    )(h, g.reshape(1, d), w_bf16)


def _fill_ext(ext_ref, cur_ref, prev_ref, next_ref, tm):
    i = pl.program_id(1)
    n = pl.num_programs(1)
    prev = prev_ref[0].astype(F32)
    nxt = next_ref[0].astype(F32)
    ext_ref[0:HALO, :] = jnp.where(i > 0, prev, jnp.zeros_like(prev))
    ext_ref[HALO + tm:HALO + tm + HALO, :] = jnp.where(i < n - 1, nxt, jnp.zeros_like(nxt))

    def body(j, carry):
        r0 = pl.multiple_of(j * ROW_CHUNK, ROW_CHUNK)
        ext_ref[pl.ds(HALO + r0, ROW_CHUNK), :] = cur_ref[0, pl.ds(r0, ROW_CHUNK), :].astype(F32)
        return carry

    lax.fori_loop(0, tm // ROW_CHUNK, body, 0)


def _depthwise(ext_ref, w_ref, cv_ref, width, tm):
    d = cv_ref.shape[1]
    rows = 32
    off = HALO - width // 2

    def body(j, carry):
        r0 = pl.multiple_of(j * rows, rows)
        for c in range(0, d, N_CHUNK):
            blk = ext_ref[pl.ds(r0, rows + 2 * HALO), c:c + N_CHUNK]
            acc = jnp.zeros((rows, N_CHUNK), F32)
            for k in range(width):
                acc = acc + blk[off + k:off + k + rows, :] * w_ref[k:k + 1, c:c + N_CHUNK]
            cv_ref[pl.ds(r0, rows), c:c + N_CHUNK] = acc
        return carry

    lax.fori_loop(0, tm // rows, body, 0)


def _project_residual(out_ref, h_ref, v_ref, w_ref):
    d = v_ref.shape[1]
    for c in range(0, d, N_CHUNK):
        out_ref[0, :, c:c + N_CHUNK] = h_ref[0, :, c:c + N_CHUNK] + _dot(v_ref[...], w_ref[:, c:c + N_CHUNK])


def _a_conv_kernel(cur_ref, prev_ref, next_ref, h_ref, wc_ref, bc_ref, gl_ref, bl_ref, wo_ref,
                   out_ref, ext_ref, cv_ref, v_ref, *, width):
    tm = cur_ref.shape[1]
    _fill_ext(ext_ref, cur_ref, prev_ref, next_ref, tm)
    _depthwise(ext_ref, wc_ref, cv_ref, width, tm)

    def body(j, carry):
        r0 = pl.multiple_of(j * ROW_CHUNK, ROW_CHUNK)
        u = cv_ref[pl.ds(r0, ROW_CHUNK), :] + bc_ref[...]
        mu = jnp.mean(u, axis=-1, keepdims=True)
        uc = u - mu
        var = jnp.mean(uc * uc, axis=-1, keepdims=True)
        y = uc * lax.rsqrt(var + EPS) * gl_ref[...] + bl_ref[...]
        v_ref[pl.ds(r0, ROW_CHUNK), :] = (y * _sigmoid(y)).astype(BF16)
        return carry

    lax.fori_loop(0, tm // ROW_CHUNK, body, 0)
    _project_residual(out_ref, h_ref, v_ref, wo_ref)


def _c_conv_kernel(cur_ref, prev_ref, next_ref, gate_ref, h_ref, wc_ref, wo_ref,
                   out_ref, ext_ref, cv_ref, v_ref, *, width):
    tm = cur_ref.shape[1]
    _fill_ext(ext_ref, cur_ref, prev_ref, next_ref, tm)
    _depthwise(ext_ref, wc_ref, cv_ref, width, tm)

    def body(j, carry):
        r0 = pl.multiple_of(j * ROW_CHUNK, ROW_CHUNK)
        gate = gate_ref[0, pl.ds(r0, ROW_CHUNK), :].astype(F32)
        v_ref[pl.ds(r0, ROW_CHUNK), :] = (gate * cv_ref[pl.ds(r0, ROW_CHUNK), :]).astype(BF16)
        return carry

    lax.fori_loop(0, tm // ROW_CHUNK, body, 0)
    _project_residual(out_ref, h_ref, v_ref, wo_ref)


def _conv_specs(s, d, tm):
    per = tm // HALO
    last = s // HALO - 1
    cur = pl.BlockSpec((1, tm, d), lambda b, i: (b, i, 0))
    prev = pl.BlockSpec((1, HALO, d), lambda b, i: (b, jnp.maximum(i * per - 1, 0), 0))
    nxt = pl.BlockSpec((1, HALO, d), lambda b, i: (b, jnp.minimum((i + 1) * per, last), 0))
    return cur, prev, nxt


def _conv_scratch(tm, d):
    return [pltpu.VMEM((tm + 2 * HALO, d), F32), pltpu.VMEM((tm, d), F32), pltpu.VMEM((tm, d), BF16)]


def _a_conv(u, h, w_conv, b_conv, g_ln, b_ln, w_out_bf16):
    bsz, s, d = h.shape
    tm = TM_DENSE
    width = w_conv.shape[0]
    cur, prev, nxt = _conv_specs(s, d, tm)
    row = _resident((1, d))
    return pl.pallas_call(
        functools.partial(_a_conv_kernel, width=width),
        grid=(bsz, s // tm),
        in_specs=[cur, prev, nxt, cur, _resident((width, d)), row, row, row, _resident((d, d))],
        out_specs=cur,
        out_shape=jax.ShapeDtypeStruct((bsz, s, d), F32),
        scratch_shapes=_conv_scratch(tm, d),
        compiler_params=_params(2),
        name="conformer_conv_out",
    )(u, u, u, h, w_conv, b_conv.reshape(1, d), g_ln.reshape(1, d), b_ln.reshape(1, d), w_out_bf16)


def _c_conv(cv, gate, h, w_conv, w_out_bf16):
    bsz, s, d = h.shape
    tm = TM_DENSE
    width = w_conv.shape[0]
    cur, prev, nxt = _conv_specs(s, d, tm)
    return pl.pallas_call(
        functools.partial(_c_conv_kernel, width=width),
        grid=(bsz, s // tm),
        in_specs=[cur, prev, nxt, cur, cur, _resident((width, d)), _resident((d, d))],
        out_specs=cur,
        out_shape=jax.ShapeDtypeStruct((bsz, s, d), F32),
        scratch_shapes=_conv_scratch(tm, d),
        compiler_params=_params(2),
        name="shortconv_conv_out",
    )(cv, cv, cv, gate, h, w_conv, w_out_bf16)


def _dft_tables(s, group):
    s1 = s // SEQ_MINOR

    def cs(n_rows, n_cols, period):
        m = (np.outer(np.arange(n_rows), np.arange(n_cols)) % period).astype(np.float64)
        ang = 2.0 * np.pi * m / period
        return np.cos(ang), -np.sin(ang)

    c1, i1 = cs(s1, s1, s1)
    w1 = np.concatenate([c1, i1], axis=0)
    twr, twi = cs(s1, SEQ_MINOR, s)
    c2, i2 = cs(SEQ_MINOR, SEQ_MINOR, SEQ_MINOR)
    w2 = np.block([[c2, -i2], [i2, c2]])
    cc, ic = cs(group, group, group)
    scale = 1.0 / math.sqrt(float(s) * float(group))
    return (jnp.asarray(w1, BF16), jnp.asarray(twr, F32), jnp.asarray(twi, F32),
            jnp.asarray(w2, BF16), jnp.asarray(cc * scale, BF16), jnp.asarray(-ic * scale, BF16))


def _b_stage1_kernel(x_ref, g_ref, w1_ref, twr_ref, twi_ref, br_ref, bi_ref, *, nb):
    s1 = x_ref.shape[1]
    d = g_ref.shape[1]
    for n in range(nb):
        xn = _rms(x_ref[0, :, n * d:(n + 1) * d], g_ref[...]).astype(BF16)
        y = _dot(w1_ref[...], xn)
        yr, yi = y[:s1], y[s1:]
        tr = twr_ref[0, :, n:n + 1]
        ti = twi_ref[0, :, n:n + 1]
        br_ref[0, :, n * d:(n + 1) * d] = (yr * tr - yi * ti).astype(BF16)
        bi_ref[0, :, n * d:(n + 1) * d] = (yr * ti + yi * tr).astype(BF16)


def _b_stage2_kernel(br_ref, bi_ref, h_ref, w2_ref, cc_ref, sc_ref, wo_ref, out_ref,
                     ar_ref, ai_ref, f_ref, *, ka, group):
    d = br_ref.shape[2]
    m = SEQ_MINOR
    for n in range(ka):
        cat = jnp.concatenate([br_ref[0, n * m:(n + 1) * m, :], bi_ref[0, n * m:(n + 1) * m, :]], axis=0)
        a = _dot(w2_ref[...], cat)
        ar_ref[n * m:(n + 1) * m, :] = a[:m].astype(BF16)
        ai_ref[n * m:(n + 1) * m, :] = a[m:].astype(BF16)
    for c in range(0, d, group):
        f = _dot(ar_ref[:, c:c + group], cc_ref[...]) + _dot(ai_ref[:, c:c + group], sc_ref[...])
        f_ref[:, c:c + group] = f.astype(BF16)
    for c in range(0, d, N_CHUNK):
        res = _dot(f_ref[...], wo_ref[:, c:c + N_CHUNK])
        for n in range(ka):
            lo = n * d + c
            out_ref[0, :, lo:lo + N_CHUNK] = h_ref[0, :, lo:lo + N_CHUNK] + res[n * m:(n + 1) * m, :]


def _fourier_mixer(h, g, w_out_bf16):
    bsz, s, d = h.shape
    s1 = s // SEQ_MINOR
    group = d // N_FOURIER_GROUPS
    nb = 4
    ka = 4
    w1, twr, twi, w2, cc, sc = _dft_tables(s, group)
    steps1 = SEQ_MINOR // nb
    twr = twr.reshape(s1, steps1, nb).transpose(1, 0, 2)
    twi = twi.reshape(s1, steps1, nb).transpose(1, 0, 2)

    x1 = h.reshape(bsz, s1, SEQ_MINOR * d)
    blk1 = pl.BlockSpec((1, s1, nb * d), lambda b, j: (b, 0, j))
    tw_spec = pl.BlockSpec((1, s1, nb), lambda b, j: (j, 0, 0))
    br, bi = pl.pallas_call(
        functools.partial(_b_stage1_kernel, nb=nb),
        grid=(bsz, steps1),
        in_specs=[blk1, _resident((1, d)), _resident((2 * s1, s1)), tw_spec, tw_spec],
        out_specs=[blk1, blk1],
        out_shape=[jax.ShapeDtypeStruct((bsz, s1, SEQ_MINOR * d), BF16)] * 2,
        compiler_params=_params(2),
        name="fourier_stage1",
    )(x1, g.reshape(1, d), w1, twr, twi)

    br = br.reshape(bsz, s, d)
    bi = bi.reshape(bsz, s, d)
    h2 = h.reshape(bsz, SEQ_MINOR, s1 * d)
    rows = pl.BlockSpec((1, ka * SEQ_MINOR, d), lambda b, j: (b, j, 0))
    strided = pl.BlockSpec((1, SEQ_MINOR, ka * d), lambda b, j: (b, 0, j))
    out = pl.pallas_call(
        functools.partial(_b_stage2_kernel, ka=ka, group=group),
        grid=(bsz, s1 // ka),
        in_specs=[rows, rows, strided, _resident((2 * SEQ_MINOR, 2 * SEQ_MINOR)),
                  _resident((group, group)), _resident((group, group)), _resident((d, d))],
        out_specs=strided,
        out_shape=jax.ShapeDtypeStruct((bsz, SEQ_MINOR, s1 * d), F32),
        scratch_shapes=[pltpu.VMEM((ka * SEQ_MINOR, d), BF16)] * 3,
        compiler_params=_params(2),
        name="fourier_stage2_out",
    )(br, bi, h2, w2, cc, sc, w_out_bf16)
    return out.reshape(bsz, s, d)


def _kv_kernel(mem_ref, g_ref, wk_ref, wv_ref, k_ref, v_ref):
    memn = _rms(mem_ref[...], g_ref[...]).astype(BF16)
    k_ref[...] = _dot(memn, wk_ref[...]).astype(BF16)
    v_ref[...] = _dot(memn, wv_ref[...]).astype(BF16)


def _memory_kv(mem2d, g, wk_bf16, wv_bf16):
    r, d = mem2d.shape
    col = pl.BlockSpec((d, N_CHUNK), lambda j: (0, j))
    out = pl.BlockSpec((r, N_CHUNK), lambda j: (0, j))
    return pl.pallas_call(
        _kv_kernel,
        grid=(d // N_CHUNK,),
        in_specs=[_resident((r, d)), _resident((1, d)), col, col],
        out_specs=[out, out],
        out_shape=[jax.ShapeDtypeStruct((r, d), BF16)] * 2,
        compiler_params=_params(1),
        name="memory_kv",
    )(mem2d, g.reshape(1, d), wk_bf16, wv_bf16)


def _xattn_kernel(h_ref, g_ref, wq_ref, k_ref, v_ref, wo_ref, out_ref, xn_ref, o_ref):
    d = h_ref.shape[2]
    hd = d // N_HEADS
    scale = float(hd) ** -0.5
    xn_ref[...] = _rms(h_ref[0], g_ref[...]).astype(BF16)
    for c in range(0, d, hd):
        q = _dot(xn_ref[...], wq_ref[:, c:c + hd]).astype(BF16)
        s = lax.dot_general(q, k_ref[0, :, c:c + hd], (((1,), (1,)), ((), ())),
                            preferred_element_type=F32) * scale
        e = jnp.exp(s - jnp.max(s, axis=-1, keepdims=True))
        p = (e / jnp.sum(e, axis=-1, keepdims=True)).astype(BF16)
        o_ref[:, c:c + hd] = _dot(p, v_ref[0, :, c:c + hd]).astype(BF16)
    _project_residual(out_ref, h_ref, o_ref, wo_ref)


def _cross_attention(h, g, wq_bf16, k, v, wo_bf16):
    bsz, s, d = h.shape
    n_mem = k.shape[1]
    tm = TM_DENSE
    tile = pl.BlockSpec((1, tm, d), lambda b, i: (b, i, 0))
    kv = pl.BlockSpec((1, n_mem, d), lambda b, i: (b, 0, 0))
    return pl.pallas_call(
        _xattn_kernel,
        grid=(bsz, s // tm),
        in_specs=[tile, _resident((1, d)), _resident((d, d)), kv, kv, _resident((d, d))],
        out_specs=tile,
        out_shape=jax.ShapeDtypeStruct((bsz, s, d), F32),
        scratch_shapes=[pltpu.VMEM((tm, d), BF16)] * 2,
        compiler_params=_params(2),
        name="cross_attention",
    )(h, g.reshape(1, d), wq_bf16, k, v, wo_bf16)


def _split_bf16(a):
    hi = a.astype(BF16)
    lo = (a - hi.astype(F32)).astype(BF16)
    return hi, lo


def _router_kernel(h_ref, g_ref, wh_ref, wl_ref, b_ref, out_ref):
    xn = _rms(h_ref[...], g_ref[...])
    xh, xl = _split_bf16(xn)
    logits = _dot(xh, wh_ref[...]) + _dot(xl, wh_ref[...]) + _dot(xh, wl_ref[...]) + b_ref[...]
    lane = lax.broadcasted_iota(jnp.int32, logits.shape, 1)
    neg = jnp.float32(-jnp.inf)
    big = jnp.int32(LANES)

    gl = jnp.where(lane < N_GROUPS, logits, neg)
    gmax = jnp.max(gl, axis=-1, keepdims=True)
    grp = jnp.min(jnp.where(gl == gmax, lane, big), axis=-1, keepdims=True)
    g_w = 1.0 / jnp.sum(jnp.exp(gl - gmax), axis=-1, keepdims=True)

    first = N_GROUPS + grp * EXPERTS_PER_GROUP
    el = jnp.where((lane >= first) & (lane < first + EXPERTS_PER_GROUP), logits, neg)
    m1 = jnp.max(el, axis=-1, keepdims=True)
    i1 = jnp.min(jnp.where(el == m1, lane, big), axis=-1, keepdims=True)
    el2 = jnp.where(lane == i1, neg, el)
    m2 = jnp.max(el2, axis=-1, keepdims=True)
    i2 = jnp.min(jnp.where(el2 == m2, lane, big), axis=-1, keepdims=True)
    e21 = jnp.exp(m2 - m1)
    w1 = g_w / (1.0 + e21)
    w2 = g_w * e21 / (1.0 + e21)

    a1 = i1 - first
    a2 = i2 - first
    lo = jnp.minimum(a1, a2)
    hi = jnp.maximum(a1, a2)
    pair = lo * 3 - jnp.right_shift(lo * (lo - 1), 1) + hi - lo - 1
    bucket = grp * N_PAIRS + pair
    w_lo = jnp.where(a1 < a2, w1, w2)
    w_hi = jnp.where(a1 < a2, w2, w1)
    out_ref[...] = jnp.where(lane == 0, bucket.astype(F32),
                             jnp.where(lane == 1, w_lo, jnp.where(lane == 2, w_hi, 0.0)))


def _router(h, g, w_rg, b_rg, w_re, b_re):
    t, d = h.shape
    n = w_rg.shape[1] + w_re.shape[1]
    w = jnp.zeros((d, LANES), F32).at[:, :n].set(jnp.concatenate([w_rg, w_re], axis=1))
    b = jnp.zeros((1, LANES), F32).at[0, :n].set(jnp.concatenate([b_rg, b_re]))
    wh, wl = _split_bf16(w)
    return pl.pallas_call(
        _router_kernel,
        grid=(t // TM_DENSE,),
        in_specs=[pl.BlockSpec((TM_DENSE, d), lambda i: (i, 0)), _resident((1, d)),
                  _resident((d, LANES)), _resident((d, LANES)), _resident((1, LANES))],
        out_specs=pl.BlockSpec((TM_DENSE, LANES), lambda i: (i, 0)),
        out_shape=jax.ShapeDtypeStruct((t, LANES), F32),
        compiler_params=_params(1),
        name="moe_router",
    )(h, g.reshape(1, d), wh, wl, b)


def _expert_kernel(ea_ref, eb_ref, used_ref, x_ref, wt_ref, g_ref, gua_ref, gub_ref, da_ref, db_ref, out_ref):
    i = pl.program_id(0)
    f = da_ref.shape[1]

    @pl.when(i < used_ref[0])
    def _():
        x = x_ref[...]
        xn = _rms(x, g_ref[...]).astype(BF16)
        acc = x
        for slot, (gu_ref, dn_ref) in enumerate(((gua_ref, da_ref), (gub_ref, db_ref))):
            gu = _dot(xn, gu_ref[0])
            gate, up = gu[:, :f], gu[:, f:]
            act = (gate * _sigmoid(gate) * up).astype(BF16)
            acc = acc + _dot(act, dn_ref[0]) * wt_ref[:, slot:slot + 1]
        out_ref[...] = acc

    @pl.when(i >= used_ref[0])
    def _():
        out_ref[...] = x_ref[...]


def _experts(xs, wts, g, tile_a, tile_b, n_used, w_gate_up_bf16, w_down_bf16):
    p, d = xs.shape
    f2 = w_gate_up_bf16.shape[2]
    f = w_down_bf16.shape[1]
    tm = TM_MOE
    grid_spec = pltpu.PrefetchScalarGridSpec(
        num_scalar_prefetch=3,
        grid=(p // tm,),
        in_specs=[pl.BlockSpec((tm, d), lambda i, ea, eb, nu: (i, 0)),
                  pl.BlockSpec((tm, 8), lambda i, ea, eb, nu: (i, 0)),
                  pl.BlockSpec((1, d), lambda i, ea, eb, nu: (0, 0)),
                  pl.BlockSpec((1, d, f2), lambda i, ea, eb, nu: (ea[i], 0, 0)),
                  pl.BlockSpec((1, d, f2), lambda i, ea, eb, nu: (eb[i], 0, 0)),
                  pl.BlockSpec((1, f, d), lambda i, ea, eb, nu: (ea[i], 0, 0)),
                  pl.BlockSpec((1, f, d), lambda i, ea, eb, nu: (eb[i], 0, 0))],
        out_specs=pl.BlockSpec((tm, d), lambda i, ea, eb, nu: (i, 0)),
    )
    return pl.pallas_call(
        _expert_kernel,
        grid_spec=grid_spec,
        out_shape=jax.ShapeDtypeStruct((p, d), F32),
        compiler_params=_params(1),
        name="moe_experts",
    )(tile_a, tile_b, n_used, xs, wts, g.reshape(1, d),
      w_gate_up_bf16, w_gate_up_bf16, w_down_bf16, w_down_bf16)


def _moe(h, g, w_rg, b_rg, w_re, b_re, w_gate_up_bf16, w_down_bf16):
    t, d = h.shape
    tm = TM_MOE
    n_tiles = t // tm + N_BUCKETS
    routed = _router(h, g, w_rg, b_rg, w_re, b_re)
    bucket = routed[:, 0].astype(jnp.int32)

    order = jnp.argsort(bucket, stable=True).astype(jnp.int32)
    sizes = jnp.zeros((N_BUCKETS,), jnp.int32).at[bucket].add(1)
    tiles_per = (sizes + tm - 1) // tm
    tile_end = jnp.cumsum(tiles_per)
    tile_start = tile_end - tiles_per
    n_used = tile_end[-1]
    start_sorted = jnp.cumsum(sizes) - sizes
    sorted_bucket = bucket[order]
    dest = tile_start[sorted_bucket] * tm + jnp.arange(t, dtype=jnp.int32) - start_sorted[sorted_bucket]
    src_tok = jnp.zeros((n_tiles * tm,), jnp.int32).at[dest].set(order)
    valid = jnp.zeros((n_tiles * tm,), F32).at[dest].set(1.0)
    pos = jnp.zeros((t,), jnp.int32).at[order].set(dest)

    tile_ids = jnp.minimum(jnp.arange(n_tiles, dtype=jnp.int32), n_used - 1)
    tile_bucket = jnp.searchsorted(tile_end, tile_ids, side="right").astype(jnp.int32)
    grp = tile_bucket // N_PAIRS
    pair = tile_bucket % N_PAIRS
    tile_a = grp * EXPERTS_PER_GROUP + jnp.asarray(PAIR_LO, jnp.int32)[pair]
    tile_b = grp * EXPERTS_PER_GROUP + jnp.asarray(PAIR_HI, jnp.int32)[pair]

    xs = h[src_tok]
    wts = jnp.zeros((n_tiles * tm, 8), F32).at[:, 0:2].set(routed[src_tok, 1:3] * valid[:, None])
    ys = _experts(xs, wts, g, tile_a, tile_b, n_used.reshape(1), w_gate_up_bf16, w_down_bf16)
    return ys[pos]


def _final_norm_kernel(x_ref, g_ref, out_ref):
    out_ref[...] = _rms(x_ref[...], g_ref[...])


def _final_norm(h, g):
    t, d = h.shape
    tile = pl.BlockSpec((TM_DENSE, d), lambda i: (i, 0))
    return pl.pallas_call(
        _final_norm_kernel,
        grid=(t // TM_DENSE,),
        in_specs=[tile, _resident((1, d))],
        out_specs=tile,
        out_shape=jax.ShapeDtypeStruct((t, d), F32),
        compiler_params=_params(1),
        name="final_norm",
    )(h, g.reshape(1, d))


def kernel(x, mem, g_mix, g_xattn, g_mem, g_moe, g_final, w_a_in, w_a_conv, b_a_conv, g_a_ln, b_a_ln, w_a_out, w_b_out, w_c_in, w_c_conv, w_c_out, w_xq, w_xk, w_xv, w_xo, w_route_group, b_route_group, w_route_expert, b_route_expert, w_gate_up, w_down):
    bsz, s, d = x.shape
    t = bsz * s
    depth = g_mix.shape[0]
    mem2d = mem.reshape(bsz * mem.shape[1], d)
    bf = lambda w: w.astype(BF16)

    h = x
    for i in range(depth):
        kind, j = i % N_MIXERS, i // N_MIXERS
        if kind == 0:
            u = _a_in(h.reshape(t, d), g_mix[i], bf(w_a_in[j]))
            h = _a_conv(u.reshape(bsz, s, d), h, w_a_conv[j], b_a_conv[j], g_a_ln[j], b_a_ln[j], bf(w_a_out[j]))
        elif kind == 1:
            h = _fourier_mixer(h, g_mix[i], bf(w_b_out[j]))
        else:
            gate, cv = _c_in(h.reshape(t, d), g_mix[i], bf(w_c_in[j]))
            h = _c_conv(cv.reshape(bsz, s, d), gate.reshape(bsz, s, d), h, w_c_conv[j], bf(w_c_out[j]))
        k, v = _memory_kv(mem2d, g_mem[i], bf(w_xk[i]), bf(w_xv[i]))
        h = _cross_attention(h, g_xattn[i], bf(w_xq[i]), k.reshape(bsz, -1, d), v.reshape(bsz, -1, d), bf(w_xo[i]))
        h = _moe(h.reshape(t, d), g_moe[i], w_route_group[i], b_route_group[i],
                 w_route_expert[i], b_route_expert[i], bf(w_gate_up[i]), bf(w_down[i])).reshape(bsz, s, d)
    return _final_norm(h.reshape(t, d), g_final).reshape(bsz, s, d)
```

```python
import functools
import math

import numpy as np
import jax
import jax.numpy as jnp
from jax import lax
from jax.experimental import pallas as pl
from jax.experimental.pallas import tpu as pltpu

F32 = jnp.float32
BF16 = jnp.bfloat16
EPS = 1e-6

N_MIXERS = 3
N_FOURIER_GROUPS = 8
N_HEADS = 4
N_GROUPS = 8
EXPERTS_PER_GROUP = 4
N_PAIRS = 6
N_BUCKETS = N_GROUPS * N_PAIRS
PAIR_LO = (0, 0, 0, 1, 1, 2)
PAIR_HI = (1, 2, 3, 2, 3, 3)

LANES = 128
SUBLANES = 8
SEQ_MINOR = 128
HALO = 16
VMEM_LIMIT_BYTES = 56 * 1024 * 1024

TM_DENSE = 512
TM_MOE = 256
ROW_CHUNK = 64
N_CHUNK = 512
CONV_COLS = 256


def _params(n_axes):
    return pltpu.CompilerParams(
        dimension_semantics=("arbitrary",) * n_axes,
        vmem_limit_bytes=VMEM_LIMIT_BYTES)


def _layer_block(shape, layer):
    zeros = (0,) * (len(shape) - 1)
    return pl.BlockSpec((1,) + tuple(shape[1:]), lambda *_: (layer,) + zeros, pipeline_mode=pl.Buffered(1))


def _rows3(a):
    return a.reshape(a.shape[0], 1, a.shape[1])


def _rms(x, g):
    return x * lax.rsqrt(jnp.mean(x * x, axis=-1, keepdims=True) + EPS) * g


def _sigmoid(x):
    return 1.0 / (1.0 + jnp.exp(-x))


def _dot(a, b):
    return jnp.dot(a, b, preferred_element_type=F32)


def _a_in_kernel(x_ref, g_ref, w_ref, u_ref, xn_ref):
    d = x_ref.shape[1]
    xn_ref[...] = _rms(x_ref[...], g_ref[0]).astype(BF16)
    for c in range(0, d, N_CHUNK):
        a = _dot(xn_ref[...], w_ref[0, :, c:c + N_CHUNK])
        gate = _dot(xn_ref[...], w_ref[0, :, d + c:d + c + N_CHUNK])
        u_ref[:, c:c + N_CHUNK] = (a * _sigmoid(gate)).astype(BF16)


def _a_in(h, g_all, layer, w_all, j):
    t, d = h.shape
    return pl.pallas_call(
        _a_in_kernel,
        grid=(t // TM_DENSE,),
        in_specs=[pl.BlockSpec((TM_DENSE, d), lambda i: (i, 0)),
                  _layer_block(g_all.shape, layer),
                  _layer_block(w_all.shape, j)],
        out_specs=pl.BlockSpec((TM_DENSE, d), lambda i: (i, 0)),
        out_shape=jax.ShapeDtypeStruct((t, d), BF16),
        scratch_shapes=[pltpu.VMEM((TM_DENSE, d), BF16)],
        compiler_params=_params(1),
        name="conformer_in",
    )(h, g_all, w_all)


def _c_in_kernel(x_ref, g_ref, w_ref, b_ref, cv_ref, xn_ref):
    d = x_ref.shape[1]
    xn_ref[...] = _rms(x_ref[...], g_ref[0]).astype(BF16)
    for c in range(0, d, N_CHUNK):
        b_ref[:, c:c + N_CHUNK] = _dot(xn_ref[...], w_ref[0, :, c:c + N_CHUNK]).astype(BF16)
        cg = _dot(xn_ref[...], w_ref[0, :, d + c:d + c + N_CHUNK])
        v = _dot(xn_ref[...], w_ref[0, :, 2 * d + c:2 * d + c + N_CHUNK])
        cv_ref[:, c:c + N_CHUNK] = (cg * v).astype(BF16)


def _c_in(h, g_all, layer, w_all, j):
    t, d = h.shape
    tile = pl.BlockSpec((TM_DENSE, d), lambda i: (i, 0))
    return pl.pallas_call(
        _c_in_kernel,
        grid=(t // TM_DENSE,),
        in_specs=[tile, _layer_block(g_all.shape, layer), _layer_block(w_all.shape, j)],
        out_specs=[tile, tile],
        out_shape=[jax.ShapeDtypeStruct((t, d), BF16)] * 2,
        scratch_shapes=[pltpu.VMEM((TM_DENSE, d), BF16)],
        compiler_params=_params(1),
        name="shortconv_in",
    )(h, g_all, w_all)


def _fill_ext(ext_ref, cur_ref, prev_ref, next_ref, tm):
    i = pl.program_id(1)
    n = pl.num_programs(1)
    prev = prev_ref[0].astype(F32)
    nxt = next_ref[0].astype(F32)
    ext_ref[0:HALO, :] = jnp.where(i > 0, prev, jnp.zeros_like(prev))
    ext_ref[HALO + tm:HALO + tm + HALO, :] = jnp.where(i < n - 1, nxt, jnp.zeros_like(nxt))

    def body(j, carry):
        r0 = pl.multiple_of(j * ROW_CHUNK, ROW_CHUNK)
        ext_ref[pl.ds(HALO + r0, ROW_CHUNK), :] = cur_ref[0, pl.ds(r0, ROW_CHUNK), :].astype(F32)
        return carry

    lax.fori_loop(0, tm // ROW_CHUNK, body, 0)


def _depthwise(ext_ref, w_ref, cv_ref, width, tm):
    d = cv_ref.shape[1]
    rows = ROW_CHUNK
    span = rows + 2 * HALO
    off = HALO - width // 2

    def body(j, carry):
        r0 = pl.multiple_of(j * rows, rows)
        for c in range(0, d, CONV_COLS):
            blk = ext_ref[pl.ds(r0, span), c:c + CONV_COLS]
            acc = jnp.zeros((rows, CONV_COLS), F32)
            for r in range(SUBLANES):
                taps = [k for k in range(width) if (off + k) % SUBLANES == r]
                if not taps:
                    continue
                shifted = blk if r == 0 else pltpu.roll(blk, span - r, axis=0)
                for k in taps:
                    q = (off + k) // SUBLANES * SUBLANES
                    acc = acc + shifted[q:q + rows, :] * w_ref[0, k:k + 1, c:c + CONV_COLS]
            cv_ref[pl.ds(r0, rows), c:c + CONV_COLS] = acc
        return carry

    lax.fori_loop(0, tm // rows, body, 0)


def _project_residual(out_ref, h_ref, v_ref, w_ref):
    d = v_ref.shape[1]
    for c in range(0, d, N_CHUNK):
        out_ref[0, :, c:c + N_CHUNK] = h_ref[0, :, c:c + N_CHUNK] + _dot(v_ref[...], w_ref[0, :, c:c + N_CHUNK])


def _a_conv_kernel(cur_ref, prev_ref, next_ref, h_ref, wc_ref, bc_ref, gl_ref, bl_ref, wo_ref,
                   out_ref, ext_ref, cv_ref, v_ref, *, width):
    tm = cur_ref.shape[1]
    _fill_ext(ext_ref, cur_ref, prev_ref, next_ref, tm)
    _depthwise(ext_ref, wc_ref, cv_ref, width, tm)

    def body(j, carry):
        r0 = pl.multiple_of(j * ROW_CHUNK, ROW_CHUNK)
        u = cv_ref[pl.ds(r0, ROW_CHUNK), :] + bc_ref[0]
        mu = jnp.mean(u, axis=-1, keepdims=True)
        uc = u - mu
        var = jnp.mean(uc * uc, axis=-1, keepdims=True)
        y = uc * lax.rsqrt(var + EPS) * gl_ref[0] + bl_ref[0]
        v_ref[pl.ds(r0, ROW_CHUNK), :] = (y * _sigmoid(y)).astype(BF16)
        return carry

    lax.fori_loop(0, tm // ROW_CHUNK, body, 0)
    _project_residual(out_ref, h_ref, v_ref, wo_ref)


def _c_conv_kernel(cur_ref, prev_ref, next_ref, gate_ref, h_ref, wc_ref, wo_ref,
                   out_ref, ext_ref, cv_ref, v_ref, *, width):
    tm = cur_ref.shape[1]
    _fill_ext(ext_ref, cur_ref, prev_ref, next_ref, tm)
    _depthwise(ext_ref, wc_ref, cv_ref, width, tm)

    def body(j, carry):
        r0 = pl.multiple_of(j * ROW_CHUNK, ROW_CHUNK)
        gate = gate_ref[0, pl.ds(r0, ROW_CHUNK), :].astype(F32)
        v_ref[pl.ds(r0, ROW_CHUNK), :] = (gate * cv_ref[pl.ds(r0, ROW_CHUNK), :]).astype(BF16)
        return carry

    lax.fori_loop(0, tm // ROW_CHUNK, body, 0)
    _project_residual(out_ref, h_ref, v_ref, wo_ref)


def _conv_specs(s, d, tm):
    per = tm // HALO
    last = s // HALO - 1
    cur = pl.BlockSpec((1, tm, d), lambda b, i: (b, i, 0))
    prev = pl.BlockSpec((1, HALO, d), lambda b, i: (b, jnp.maximum(i * per - 1, 0), 0))
    nxt = pl.BlockSpec((1, HALO, d), lambda b, i: (b, jnp.minimum((i + 1) * per, last), 0))
    return cur, prev, nxt


def _conv_scratch(tm, d):
    return [pltpu.VMEM((tm + 2 * HALO, d), F32), pltpu.VMEM((tm, d), F32), pltpu.VMEM((tm, d), BF16)]


def _a_conv(u, h, w_conv, b_conv, g_ln, b_ln, w_out, j):
    bsz, s, d = h.shape
    tm = TM_DENSE
    cur, prev, nxt = _conv_specs(s, d, tm)
    return pl.pallas_call(
        functools.partial(_a_conv_kernel, width=w_conv.shape[1]),
        grid=(bsz, s // tm),
        in_specs=[cur, prev, nxt, cur, _layer_block(w_conv.shape, j), _layer_block(b_conv.shape, j),
                  _layer_block(g_ln.shape, j), _layer_block(b_ln.shape, j), _layer_block(w_out.shape, j)],
        out_specs=cur,
        out_shape=jax.ShapeDtypeStruct((bsz, s, d), F32),
        scratch_shapes=_conv_scratch(tm, d),
        compiler_params=_params(2),
        name="conformer_conv_out",
    )(u, u, u, h, w_conv, b_conv, g_ln, b_ln, w_out)


def _c_conv(cv, gate, h, w_conv, w_out, j):
    bsz, s, d = h.shape
    tm = TM_DENSE
    cur, prev, nxt = _conv_specs(s, d, tm)
    return pl.pallas_call(
        functools.partial(_c_conv_kernel, width=w_conv.shape[1]),
        grid=(bsz, s // tm),
        in_specs=[cur, prev, nxt, cur, cur, _layer_block(w_conv.shape, j), _layer_block(w_out.shape, j)],
        out_specs=cur,
        out_shape=jax.ShapeDtypeStruct((bsz, s, d), F32),
        scratch_shapes=_conv_scratch(tm, d),
        compiler_params=_params(2),
        name="shortconv_conv_out",
    )(cv, cv, cv, gate, h, w_conv, w_out)


def _dft_tables(s, group):
    s1 = s // SEQ_MINOR

    def cs(n_rows, n_cols, period):
        m = (np.outer(np.arange(n_rows), np.arange(n_cols)) % period).astype(np.float64)
        ang = 2.0 * np.pi * m / period
        return np.cos(ang), -np.sin(ang)

    c1, i1 = cs(s1, s1, s1)
    w1 = np.concatenate([c1, i1], axis=0)
    twr, twi = cs(s1, SEQ_MINOR, s)
    c2, i2 = cs(SEQ_MINOR, SEQ_MINOR, SEQ_MINOR)
    w2 = np.block([[c2, -i2], [i2, c2]])
    cc, ic = cs(group, group, group)
    scale = 1.0 / math.sqrt(float(s) * float(group))
    return (jnp.asarray(w1, BF16), jnp.asarray(twr, F32), jnp.asarray(twi, F32),
            jnp.asarray(w2, BF16), jnp.asarray(cc * scale, BF16), jnp.asarray(-ic * scale, BF16))


def _b_stage1_kernel(x_ref, g_ref, w1_ref, twr_ref, twi_ref, br_ref, bi_ref, *, nb):
    s1 = x_ref.shape[1]
    d = g_ref.shape[2]
    for n in range(nb):
        xn = _rms(x_ref[0, :, n * d:(n + 1) * d], g_ref[0]).astype(BF16)
        y = _dot(w1_ref[...], xn)
        yr, yi = y[:s1], y[s1:]
        tr = twr_ref[0, :, n:n + 1]
        ti = twi_ref[0, :, n:n + 1]
        br_ref[0, :, n * d:(n + 1) * d] = (yr * tr - yi * ti).astype(BF16)
        bi_ref[0, :, n * d:(n + 1) * d] = (yr * ti + yi * tr).astype(BF16)


def _b_stage2_kernel(br_ref, bi_ref, h_ref, w2_ref, cc_ref, sc_ref, wo_ref, out_ref,
                     ar_ref, ai_ref, f_ref, *, ka, group):
    d = br_ref.shape[2]
    m = SEQ_MINOR
    for n in range(ka):
        cat = jnp.concatenate([br_ref[0, n * m:(n + 1) * m, :], bi_ref[0, n * m:(n + 1) * m, :]], axis=0)
        a = _dot(w2_ref[...], cat)
        ar_ref[n * m:(n + 1) * m, :] = a[:m].astype(BF16)
        ai_ref[n * m:(n + 1) * m, :] = a[m:].astype(BF16)
    for c in range(0, d, group):
        f = _dot(ar_ref[:, c:c + group], cc_ref[...]) + _dot(ai_ref[:, c:c + group], sc_ref[...])
        f_ref[:, c:c + group] = f.astype(BF16)
    for c in range(0, d, N_CHUNK):
        res = _dot(f_ref[...], wo_ref[0, :, c:c + N_CHUNK])
        for n in range(ka):
            lo = n * d + c
            out_ref[0, :, lo:lo + N_CHUNK] = h_ref[0, :, lo:lo + N_CHUNK] + res[n * m:(n + 1) * m, :]


def _const_block(shape):
    nd = len(shape)
    return pl.BlockSpec(shape, lambda *_: (0,) * nd, pipeline_mode=pl.Buffered(1))


def _fourier_mixer(h, g_all, layer, w_out, j):
    bsz, s, d = h.shape
    s1 = s // SEQ_MINOR
    group = d // N_FOURIER_GROUPS
    nb = 4
    ka = 4
    w1, twr, twi, w2, cc, sc = _dft_tables(s, group)
    steps1 = SEQ_MINOR // nb
    twr = twr.reshape(s1, steps1, nb).transpose(1, 0, 2)
    twi = twi.reshape(s1, steps1, nb).transpose(1, 0, 2)

    x1 = h.reshape(bsz, s1, SEQ_MINOR * d)
    blk1 = pl.BlockSpec((1, s1, nb * d), lambda b, i: (b, 0, i))
    tw_spec = pl.BlockSpec((1, s1, nb), lambda b, i: (i, 0, 0))
    br, bi = pl.pallas_call(
        functools.partial(_b_stage1_kernel, nb=nb),
        grid=(bsz, steps1),
        in_specs=[blk1, _layer_block(g_all.shape, layer), _const_block((2 * s1, s1)), tw_spec, tw_spec],
        out_specs=[blk1, blk1],
        out_shape=[jax.ShapeDtypeStruct((bsz, s1, SEQ_MINOR * d), BF16)] * 2,
        compiler_params=_params(2),
        name="fourier_stage1",
    )(x1, g_all, w1, twr, twi)

    br = br.reshape(bsz, s, d)
    bi = bi.reshape(bsz, s, d)
    h2 = h.reshape(bsz, SEQ_MINOR, s1 * d)
    rows = pl.BlockSpec((1, ka * SEQ_MINOR, d), lambda b, i: (b, i, 0))
    strided = pl.BlockSpec((1, SEQ_MINOR, ka * d), lambda b, i: (b, 0, i))
    out = pl.pallas_call(
        functools.partial(_b_stage2_kernel, ka=ka, group=group),
        grid=(bsz, s1 // ka),
        in_specs=[rows, rows, strided, _const_block((2 * SEQ_MINOR, 2 * SEQ_MINOR)),
                  _const_block((group, group)), _const_block((group, group)), _layer_block(w_out.shape, j)],
        out_specs=strided,
        out_shape=jax.ShapeDtypeStruct((bsz, SEQ_MINOR, s1 * d), F32),
        scratch_shapes=[pltpu.VMEM((ka * SEQ_MINOR, d), BF16)] * 3,
        compiler_params=_params(2),
        name="fourier_stage2_out",
    )(br, bi, h2, w2, cc, sc, w_out)
    return out.reshape(bsz, s, d)


def _kv_kernel(mem_ref, g_ref, wk_ref, wv_ref, k_ref, v_ref):
    memn = _rms(mem_ref[...], g_ref[0]).astype(BF16)
    k_ref[...] = _dot(memn, wk_ref[0]).astype(BF16)
    v_ref[...] = _dot(memn, wv_ref[0]).astype(BF16)


def _memory_kv(mem2d, g_all, wk, wv, layer):
    r, d = mem2d.shape
    col = pl.BlockSpec((1, d, N_CHUNK), lambda c: (layer, 0, c))
    out = pl.BlockSpec((r, N_CHUNK), lambda c: (0, c))
    return pl.pallas_call(
        _kv_kernel,
        grid=(d // N_CHUNK,),
        in_specs=[_const_block((r, d)), _layer_block(g_all.shape, layer), col, col],
        out_specs=[out, out],
        out_shape=[jax.ShapeDtypeStruct((r, d), BF16)] * 2,
        compiler_params=_params(1),
        name="memory_kv",
    )(mem2d, g_all, wk, wv)


def _split_bf16(a):
    hi = a.astype(BF16)
    lo = (a - hi.astype(F32)).astype(BF16)
    return hi, lo


def _route(xn, wh, wl, bias):
    xh, xl = _split_bf16(xn)
    logits = _dot(xh, wh) + _dot(xl, wh) + _dot(xh, wl) + bias
    lane = lax.broadcasted_iota(jnp.int32, logits.shape, 1)
    neg = jnp.float32(-jnp.inf)
    big = jnp.int32(LANES)

    gl = jnp.where(lane < N_GROUPS, logits, neg)
    gmax = jnp.max(gl, axis=-1, keepdims=True)
    grp = jnp.min(jnp.where(gl == gmax, lane, big), axis=-1, keepdims=True)
    g_w = 1.0 / jnp.sum(jnp.exp(gl - gmax), axis=-1, keepdims=True)

    first = N_GROUPS + grp * EXPERTS_PER_GROUP
    el = jnp.where((lane >= first) & (lane < first + EXPERTS_PER_GROUP), logits, neg)
    m1 = jnp.max(el, axis=-1, keepdims=True)
    i1 = jnp.min(jnp.where(el == m1, lane, big), axis=-1, keepdims=True)
    el2 = jnp.where(lane == i1, neg, el)
    m2 = jnp.max(el2, axis=-1, keepdims=True)
    i2 = jnp.min(jnp.where(el2 == m2, lane, big), axis=-1, keepdims=True)
    e21 = jnp.exp(m2 - m1)
    w1 = g_w / (1.0 + e21)
    w2 = g_w * e21 / (1.0 + e21)

    a1 = i1 - first
    a2 = i2 - first
    lo = jnp.minimum(a1, a2)
    hi = jnp.maximum(a1, a2)
    pair = lo * 3 - jnp.right_shift(lo * (lo - 1), 1) + hi - lo - 1
    bucket = grp * N_PAIRS + pair
    w_lo = jnp.where(a1 < a2, w1, w2)
    w_hi = jnp.where(a1 < a2, w2, w1)
    routed = jnp.where(lane == 0, bucket.astype(F32),
                       jnp.where(lane == 1, w_lo, jnp.where(lane == 2, w_hi, 0.0)))
    return routed, (lane == bucket).astype(F32)


def _xattn_kernel(h_ref, g_ref, wq_ref, k_ref, v_ref, wo_ref, gm_ref, rwh_ref, rwl_ref, rb_ref,
                  out_ref, cnt_ref, xn_ref, o_ref):
    d = h_ref.shape[2]
    hd = d // N_HEADS
    scale = float(hd) ** -0.5
    xn_ref[...] = _rms(h_ref[0], g_ref[0]).astype(BF16)
    for c in range(0, d, hd):
        q = _dot(xn_ref[...], wq_ref[0, :, c:c + hd]).astype(BF16)
        s = lax.dot_general(q, k_ref[0, :, c:c + hd], (((1,), (1,)), ((), ())),
                            preferred_element_type=F32) * scale
        e = jnp.exp(s - jnp.max(s, axis=-1, keepdims=True))
        p = (e / jnp.sum(e, axis=-1, keepdims=True)).astype(BF16)
        o_ref[:, c:c + hd] = _dot(p, v_ref[0, :, c:c + hd]).astype(BF16)
    for c in range(0, d, N_CHUNK):
        out_ref[:, c:c + N_CHUNK] = h_ref[0, :, c:c + N_CHUNK] + _dot(o_ref[...], wo_ref[0, :, c:c + N_CHUNK])

    routed, onehot = _route(_rms(out_ref[:, 0:d], gm_ref[0]), rwh_ref[...], rwl_ref[...], rb_ref[...])
    out_ref[:, d:d + LANES] = routed

    @pl.when((pl.program_id(0) == 0) & (pl.program_id(1) == 0))
    def _():
        cnt_ref[...] = jnp.zeros_like(cnt_ref)

    cnt_ref[...] += jnp.sum(onehot, axis=0, keepdims=True)


def _cross_attention_route(h, g_all, wq, k, v, wo, g_moe_all, w_rg, b_rg, w_re, b_re, layer):
    bsz, s, d = h.shape
    n_mem = k.shape[1]
    tm = TM_DENSE
    per = s // tm
    n = w_rg.shape[1] + w_re.shape[1]
    w = jnp.zeros((d, LANES), F32).at[:, :n].set(jnp.concatenate([w_rg, w_re], axis=1))
    b = jnp.zeros((1, LANES), F32).at[0, :n].set(jnp.concatenate([b_rg, b_re]))
    wh, wl = _split_bf16(w)
    tile = pl.BlockSpec((1, tm, d), lambda bb, i: (bb, i, 0))
    kv = pl.BlockSpec((1, n_mem, d), lambda bb, i: (bb, 0, 0))
    return pl.pallas_call(
        _xattn_kernel,
        grid=(bsz, per),
        in_specs=[tile, _layer_block(g_all.shape, layer), _layer_block(wq.shape, layer), kv, kv,
                  _layer_block(wo.shape, layer), _layer_block(g_moe_all.shape, layer),
                  _const_block((d, LANES)), _const_block((d, LANES)), _const_block((1, LANES))],
        out_specs=[pl.BlockSpec((tm, d + LANES), lambda bb, i: (bb * per + i, 0)),
                   pl.BlockSpec((1, LANES), lambda bb, i: (0, 0))],
        out_shape=[jax.ShapeDtypeStruct((bsz * s, d + LANES), F32),
                   jax.ShapeDtypeStruct((1, LANES), F32)],
        scratch_shapes=[pltpu.VMEM((tm, d), BF16)] * 2,
        compiler_params=_params(2),
        name="cross_attention_route",
    )(h, g_all, wq, k, v, wo, g_moe_all, wh, wl, b)


def _wait_rows(count, block_copy, row_copy):
    n8 = pl.multiple_of(jnp.right_shift(count, 3) * SUBLANES, SUBLANES)

    @pl.when(n8 > 0)
    def _():
        block_copy(n8).wait()

    def one(r, carry):
        row_copy().wait()
        return carry

    lax.fori_loop(0, count - n8, one, 0)


def _expert_kernel(order_ref, c0_ref, nv_ref, ea_ref, eb_ref,
                   haug_ref, g_ref, gua_ref, gub_ref, da_ref, db_ref,
                   out_ref, xbuf, obuf, gsem, ssem):
    i = pl.program_id(0)
    n = pl.num_programs(0)
    slot = lax.rem(i, 2)
    other = 1 - slot
    d = obuf.shape[2]
    f = da_ref.shape[2]

    def gather_row(tok, s, r):
        return pltpu.make_async_copy(haug_ref.at[pl.ds(tok, 1), :], xbuf.at[s, pl.ds(r, 1), :], gsem.at[s])

    def scatter_row(tok, s, r):
        return pltpu.make_async_copy(obuf.at[s, pl.ds(r, 1), :], out_ref.at[pl.ds(tok, 1), :], ssem.at[s])

    def start_gather(tile, s):
        base = c0_ref[tile]

        def body(r, carry):
            gather_row(order_ref[base + r], s, r).start()
            return carry

        lax.fori_loop(0, nv_ref[tile], body, 0)

    def wait_gather(tile, s):
        _wait_rows(nv_ref[tile],
                   lambda m: pltpu.make_async_copy(haug_ref.at[pl.ds(0, m), :], xbuf.at[s, pl.ds(0, m), :], gsem.at[s]),
                   lambda: gather_row(0, s, 0))

    def start_scatter(tile, s):
        base = c0_ref[tile]

        def body(r, carry):
            scatter_row(order_ref[base + r], s, r).start()
            return carry

        lax.fori_loop(0, nv_ref[tile], body, 0)

    def wait_scatter(tile, s):
        _wait_rows(nv_ref[tile],
                   lambda m: pltpu.make_async_copy(obuf.at[s, pl.ds(0, m), :], out_ref.at[pl.ds(0, m), :], ssem.at[s]),
                   lambda: scatter_row(0, s, 0))

    @pl.when(i == 0)
    def _():
        xbuf[...] = jnp.zeros_like(xbuf)
        start_gather(0, 0)

    @pl.when(i + 1 < n)
    def _():
        start_gather(i + 1, other)

    wait_gather(i, slot)

    @pl.when(i >= 2)
    def _():
        wait_scatter(i - 2, slot)

    @pl.when(nv_ref[i] > 0)
    def _():
        x = xbuf[slot, :, 0:d]
        xn = _rms(x, g_ref[0]).astype(BF16)
        acc = x
        for lane, (gu_ref, dn_ref) in enumerate(((gua_ref, da_ref), (gub_ref, db_ref))):
            gu = _dot(xn, gu_ref[0, 0])
            gate, up = gu[:, :f], gu[:, f:]
            act = (gate * _sigmoid(gate) * up).astype(BF16)
            acc = acc + _dot(act, dn_ref[0, 0]) * xbuf[slot, :, d + 1 + lane:d + 2 + lane]
        obuf[slot] = acc
        start_scatter(i, slot)

    @pl.when(i == n - 1)
    def _():
        @pl.when(n >= 2)
        def _():
            wait_scatter(i - 1, other)
        wait_scatter(i, slot)


def _moe_experts(haug, counts, g_all, w_gate_up, w_down, layer):
    t, da = haug.shape
    d = da - LANES
    f2 = w_gate_up.shape[3]
    f = w_down.shape[2]
    tm = TM_MOE
    n_tiles = t // tm + N_BUCKETS

    bucket = haug[:, d].astype(jnp.int32)
    order = jnp.argsort(bucket, stable=True).astype(jnp.int32)
    sizes = counts[0, :N_BUCKETS].astype(jnp.int32)
    tiles_per = (sizes + tm - 1) // tm
    tile_end = jnp.cumsum(tiles_per)
    tile_start = tile_end - tiles_per
    n_used = tile_end[-1]
    start_sorted = jnp.cumsum(sizes) - sizes
    ids = jnp.arange(n_tiles, dtype=jnp.int32)
    tb = jnp.sum((tile_end[None, :] <= jnp.minimum(ids, n_used - 1)[:, None]).astype(jnp.int32), axis=1)
    k = ids - tile_start[tb]
    nv = jnp.where(ids < n_used, jnp.clip(sizes[tb] - k * tm, 0, tm), 0).astype(jnp.int32)
    c0 = jnp.where(ids < n_used, start_sorted[tb] + k * tm, 0).astype(jnp.int32)
    grp = tb // N_PAIRS
    pair = tb % N_PAIRS
    tile_a = (grp * EXPERTS_PER_GROUP + jnp.asarray(PAIR_LO, jnp.int32)[pair]).astype(jnp.int32)
    tile_b = (grp * EXPERTS_PER_GROUP + jnp.asarray(PAIR_HI, jnp.int32)[pair]).astype(jnp.int32)

    any_space = pl.BlockSpec(memory_space=pl.ANY)
    grid_spec = pltpu.PrefetchScalarGridSpec(
        num_scalar_prefetch=5,
        grid=(n_tiles,),
        in_specs=[any_space,
                  pl.BlockSpec((1, 1, d), lambda i, o, c, v, ea, eb: (layer, 0, 0)),
                  pl.BlockSpec((1, 1, d, f2), lambda i, o, c, v, ea, eb: (layer, ea[i], 0, 0)),
                  pl.BlockSpec((1, 1, d, f2), lambda i, o, c, v, ea, eb: (layer, eb[i], 0, 0)),
                  pl.BlockSpec((1, 1, f, d), lambda i, o, c, v, ea, eb: (layer, ea[i], 0, 0)),
                  pl.BlockSpec((1, 1, f, d), lambda i, o, c, v, ea, eb: (layer, eb[i], 0, 0))],
        out_specs=any_space,
        scratch_shapes=[pltpu.VMEM((2, tm, da), F32), pltpu.VMEM((2, tm, d), F32),
                        pltpu.SemaphoreType.DMA((2,)), pltpu.SemaphoreType.DMA((2,))],
    )
    return pl.pallas_call(
        _expert_kernel,
        grid_spec=grid_spec,
        out_shape=jax.ShapeDtypeStruct((t, d), F32),
        compiler_params=_params(1),
        name="moe_experts",
    )(order, c0, nv, tile_a, tile_b, haug, g_all, w_gate_up, w_gate_up, w_down, w_down)


def _final_norm_kernel(x_ref, g_ref, out_ref):
    out_ref[...] = _rms(x_ref[...], g_ref[...])


def _final_norm(h, g):
    t, d = h.shape
    tile = pl.BlockSpec((TM_DENSE, d), lambda i: (i, 0))
    return pl.pallas_call(
        _final_norm_kernel,
        grid=(t // TM_DENSE,),
        in_specs=[tile, _const_block((1, d))],
        out_specs=tile,
        out_shape=jax.ShapeDtypeStruct((t, d), F32),
        compiler_params=_params(1),
        name="final_norm",
    )(h, g.reshape(1, d))


def kernel(x, mem, g_mix, g_xattn, g_mem, g_moe, g_final, w_a_in, w_a_conv, b_a_conv, g_a_ln, b_a_ln, w_a_out, w_b_out, w_c_in, w_c_conv, w_c_out, w_xq, w_xk, w_xv, w_xo, w_route_group, b_route_group, w_route_expert, b_route_expert, w_gate_up, w_down):
    bsz, s, d = x.shape
    t = bsz * s
    depth = g_mix.shape[0]
    mem2d = mem.reshape(bsz * mem.shape[1], d)
    bf = lambda w: w.astype(BF16)
    g_mix, g_xattn, g_mem, g_moe = _rows3(g_mix), _rows3(g_xattn), _rows3(g_mem), _rows3(g_moe)
    b_a_conv, g_a_ln, b_a_ln = _rows3(b_a_conv), _rows3(g_a_ln), _rows3(b_a_ln)
    w_a_in, w_a_out, w_b_out, w_c_in, w_c_out = bf(w_a_in), bf(w_a_out), bf(w_b_out), bf(w_c_in), bf(w_c_out)
    w_xq, w_xk, w_xv, w_xo = bf(w_xq), bf(w_xk), bf(w_xv), bf(w_xo)
    w_gate_up, w_down = bf(w_gate_up), bf(w_down)

    h = x
    for i in range(depth):
        kind, j = i % N_MIXERS, i // N_MIXERS
        if kind == 0:
            u = _a_in(h.reshape(t, d), g_mix, i, w_a_in, j)
            h = _a_conv(u.reshape(bsz, s, d), h, w_a_conv, b_a_conv, g_a_ln, b_a_ln, w_a_out, j)
        elif kind == 1:
            h = _fourier_mixer(h, g_mix, i, w_b_out, j)
        else:
            gate, cv = _c_in(h.reshape(t, d), g_mix, i, w_c_in, j)
            h = _c_conv(cv.reshape(bsz, s, d), gate.reshape(bsz, s, d), h, w_c_conv, w_c_out, j)
        k, v = _memory_kv(mem2d, g_mem, w_xk, w_xv, i)
        haug, counts = _cross_attention_route(
            h, g_xattn, w_xq, k.reshape(bsz, -1, d), v.reshape(bsz, -1, d), w_xo, g_moe,
            w_route_group[i], b_route_group[i], w_route_expert[i], b_route_expert[i], i)
        h = _moe_experts(haug, counts, g_moe, w_gate_up, w_down, i).reshape(bsz, s, d)
    return _final_norm(h.reshape(t, d), g_final).reshape(bsz, s, d)
```

```python
import functools
import math

import numpy as np
import jax
import jax.numpy as jnp
from jax import lax
from jax.experimental import pallas as pl
from jax.experimental.pallas import tpu as pltpu

F32 = jnp.float32
BF16 = jnp.bfloat16
EPS = 1e-6

N_MIXERS = 3
N_FOURIER_GROUPS = 8
N_HEADS = 4
N_GROUPS = 8
EXPERTS_PER_GROUP = 4
N_PAIRS = 6
N_BUCKETS = N_GROUPS * N_PAIRS
PAIR_LO = (0, 0, 0, 1, 1, 2)
PAIR_HI = (1, 2, 3, 2, 3, 3)

LANES = 128
SUBLANES = 8
SEQ_MINOR = 128
HALO = 16
VMEM_LIMIT_BYTES = 56 * 1024 * 1024

TM_DENSE = 512
TM_MOE = 256
ROW_CHUNK = 64
N_CHUNK = 512
CONV_COLS = 256


def _params(n_axes):
    return pltpu.CompilerParams(
        dimension_semantics=("arbitrary",) * n_axes,
        vmem_limit_bytes=VMEM_LIMIT_BYTES)


def _layer_block(shape, layer):
    zeros = (0,) * (len(shape) - 1)
    return pl.BlockSpec((1,) + tuple(shape[1:]), lambda *_: (layer,) + zeros, pipeline_mode=pl.Buffered(1))


def _rows3(a):
    return a.reshape(a.shape[0], 1, a.shape[1])


def _rms(x, g):
    return x * lax.rsqrt(jnp.mean(x * x, axis=-1, keepdims=True) + EPS) * g


def _sigmoid(x):
    return 1.0 / (1.0 + jnp.exp(-x))


def _dot(a, b):
    return jnp.dot(a, b, preferred_element_type=F32)


def _a_in_kernel(x_ref, g_ref, w_ref, u_ref, xn_ref):
    d = x_ref.shape[1]
    xn_ref[...] = _rms(x_ref[...], g_ref[0]).astype(BF16)
    for c in range(0, d, N_CHUNK):
        a = _dot(xn_ref[...], w_ref[0, :, c:c + N_CHUNK])
        gate = _dot(xn_ref[...], w_ref[0, :, d + c:d + c + N_CHUNK])
        u_ref[:, c:c + N_CHUNK] = (a * _sigmoid(gate)).astype(BF16)


def _a_in(h, g_all, layer, w_all, j):
    t, d = h.shape
    return pl.pallas_call(
        _a_in_kernel,
        grid=(t // TM_DENSE,),
        in_specs=[pl.BlockSpec((TM_DENSE, d), lambda i: (i, 0)),
                  _layer_block(g_all.shape, layer),
                  _layer_block(w_all.shape, j)],
        out_specs=pl.BlockSpec((TM_DENSE, d), lambda i: (i, 0)),
        out_shape=jax.ShapeDtypeStruct((t, d), BF16),
        scratch_shapes=[pltpu.VMEM((TM_DENSE, d), BF16)],
        compiler_params=_params(1),
        name="conformer_in",
    )(h, g_all, w_all)


def _c_in_kernel(x_ref, g_ref, w_ref, b_ref, cv_ref, xn_ref):
    d = x_ref.shape[1]
    xn_ref[...] = _rms(x_ref[...], g_ref[0]).astype(BF16)
    for c in range(0, d, N_CHUNK):
        b_ref[:, c:c + N_CHUNK] = _dot(xn_ref[...], w_ref[0, :, c:c + N_CHUNK]).astype(BF16)
        cg = _dot(xn_ref[...], w_ref[0, :, d + c:d + c + N_CHUNK])
        v = _dot(xn_ref[...], w_ref[0, :, 2 * d + c:2 * d + c + N_CHUNK])
        cv_ref[:, c:c + N_CHUNK] = (cg * v).astype(BF16)


def _c_in(h, g_all, layer, w_all, j):
    t, d = h.shape
    tile = pl.BlockSpec((TM_DENSE, d), lambda i: (i, 0))
    return pl.pallas_call(
        _c_in_kernel,
        grid=(t // TM_DENSE,),
        in_specs=[tile, _layer_block(g_all.shape, layer), _layer_block(w_all.shape, j)],
        out_specs=[tile, tile],
        out_shape=[jax.ShapeDtypeStruct((t, d), BF16)] * 2,
        scratch_shapes=[pltpu.VMEM((TM_DENSE, d), BF16)],
        compiler_params=_params(1),
        name="shortconv_in",
    )(h, g_all, w_all)


def _fill_ext(ext_ref, cur_ref, prev_ref, next_ref, tm):
    i = pl.program_id(1)
    n = pl.num_programs(1)
    prev = prev_ref[0].astype(F32)
    nxt = next_ref[0].astype(F32)
    ext_ref[0:HALO, :] = jnp.where(i > 0, prev, jnp.zeros_like(prev))
    ext_ref[HALO + tm:HALO + tm + HALO, :] = jnp.where(i < n - 1, nxt, jnp.zeros_like(nxt))

    def body(j, carry):
        r0 = pl.multiple_of(j * ROW_CHUNK, ROW_CHUNK)
        ext_ref[pl.ds(HALO + r0, ROW_CHUNK), :] = cur_ref[0, pl.ds(r0, ROW_CHUNK), :].astype(F32)
        return carry

    lax.fori_loop(0, tm // ROW_CHUNK, body, 0)


def _depthwise(ext_ref, w_ref, cv_ref, width, tm):
    d = cv_ref.shape[1]
    rows = ROW_CHUNK
    span = rows + 2 * HALO
    off = HALO - width // 2

    def body(j, carry):
        r0 = pl.multiple_of(j * rows, rows)
        for c in range(0, d, CONV_COLS):
            blk = ext_ref[pl.ds(r0, span), c:c + CONV_COLS]
            acc = jnp.zeros((rows, CONV_COLS), F32)
            for r in range(SUBLANES):
                taps = [k for k in range(width) if (off + k) % SUBLANES == r]
                if not taps:
                    continue
                shifted = blk if r == 0 else pltpu.roll(blk, span - r, axis=0)
                for k in taps:
                    q = (off + k) // SUBLANES * SUBLANES
                    acc = acc + shifted[q:q + rows, :] * w_ref[0, k:k + 1, c:c + CONV_COLS]
            cv_ref[pl.ds(r0, rows), c:c + CONV_COLS] = acc
        return carry

    lax.fori_loop(0, tm // rows, body, 0)


def _project_residual(out_ref, h_ref, v_ref, w_ref):
    d = v_ref.shape[1]
    for c in range(0, d, N_CHUNK):
        out_ref[0, :, c:c + N_CHUNK] = h_ref[0, :, c:c + N_CHUNK] + _dot(v_ref[...], w_ref[0, :, c:c + N_CHUNK])


def _a_conv_kernel(cur_ref, prev_ref, next_ref, h_ref, wc_ref, bc_ref, gl_ref, bl_ref, wo_ref,
                   out_ref, ext_ref, cv_ref, v_ref, *, width):
    tm = cur_ref.shape[1]
    _fill_ext(ext_ref, cur_ref, prev_ref, next_ref, tm)
    _depthwise(ext_ref, wc_ref, cv_ref, width, tm)

    def body(j, carry):
        r0 = pl.multiple_of(j * ROW_CHUNK, ROW_CHUNK)
        u = cv_ref[pl.ds(r0, ROW_CHUNK), :] + bc_ref[0]
        mu = jnp.mean(u, axis=-1, keepdims=True)
        uc = u - mu
        var = jnp.mean(uc * uc, axis=-1, keepdims=True)
        y = uc * lax.rsqrt(var + EPS) * gl_ref[0] + bl_ref[0]
        v_ref[pl.ds(r0, ROW_CHUNK), :] = (y * _sigmoid(y)).astype(BF16)
        return carry

    lax.fori_loop(0, tm // ROW_CHUNK, body, 0)
    _project_residual(out_ref, h_ref, v_ref, wo_ref)


def _c_conv_kernel(cur_ref, prev_ref, next_ref, gate_ref, h_ref, wc_ref, wo_ref,
                   out_ref, ext_ref, cv_ref, v_ref, *, width):
    tm = cur_ref.shape[1]
    _fill_ext(ext_ref, cur_ref, prev_ref, next_ref, tm)
    _depthwise(ext_ref, wc_ref, cv_ref, width, tm)

    def body(j, carry):
        r0 = pl.multiple_of(j * ROW_CHUNK, ROW_CHUNK)
        gate = gate_ref[0, pl.ds(r0, ROW_CHUNK), :].astype(F32)
        v_ref[pl.ds(r0, ROW_CHUNK), :] = (gate * cv_ref[pl.ds(r0, ROW_CHUNK), :]).astype(BF16)
        return carry

    lax.fori_loop(0, tm // ROW_CHUNK, body, 0)
    _project_residual(out_ref, h_ref, v_ref, wo_ref)


def _conv_specs(s, d, tm):
    per = tm // HALO
    last = s // HALO - 1
    cur = pl.BlockSpec((1, tm, d), lambda b, i: (b, i, 0))
    prev = pl.BlockSpec((1, HALO, d), lambda b, i: (b, jnp.maximum(i * per - 1, 0), 0))
    nxt = pl.BlockSpec((1, HALO, d), lambda b, i: (b, jnp.minimum((i + 1) * per, last), 0))
    return cur, prev, nxt


def _conv_scratch(tm, d):
    return [pltpu.VMEM((tm + 2 * HALO, d), F32), pltpu.VMEM((tm, d), F32), pltpu.VMEM((tm, d), BF16)]


def _a_conv(u, h, w_conv, b_conv, g_ln, b_ln, w_out, j):
    bsz, s, d = h.shape
    tm = TM_DENSE
    cur, prev, nxt = _conv_specs(s, d, tm)
    return pl.pallas_call(
        functools.partial(_a_conv_kernel, width=w_conv.shape[1]),
        grid=(bsz, s // tm),
        in_specs=[cur, prev, nxt, cur, _layer_block(w_conv.shape, j), _layer_block(b_conv.shape, j),
                  _layer_block(g_ln.shape, j), _layer_block(b_ln.shape, j), _layer_block(w_out.shape, j)],
        out_specs=cur,
        out_shape=jax.ShapeDtypeStruct((bsz, s, d), F32),
        scratch_shapes=_conv_scratch(tm, d),
        compiler_params=_params(2),
        name="conformer_conv_out",
    )(u, u, u, h, w_conv, b_conv, g_ln, b_ln, w_out)


def _c_conv(cv, gate, h, w_conv, w_out, j):
    bsz, s, d = h.shape
    tm = TM_DENSE
    cur, prev, nxt = _conv_specs(s, d, tm)
    return pl.pallas_call(
        functools.partial(_c_conv_kernel, width=w_conv.shape[1]),
        grid=(bsz, s // tm),
        in_specs=[cur, prev, nxt, cur, cur, _layer_block(w_conv.shape, j), _layer_block(w_out.shape, j)],
        out_specs=cur,
        out_shape=jax.ShapeDtypeStruct((bsz, s, d), F32),
        scratch_shapes=_conv_scratch(tm, d),
        compiler_params=_params(2),
        name="shortconv_conv_out",
    )(cv, cv, cv, gate, h, w_conv, w_out)


def _dft_tables(s, group):
    s1 = s // SEQ_MINOR

    def cs(n_rows, n_cols, period):
        m = (np.outer(np.arange(n_rows), np.arange(n_cols)) % period).astype(np.float64)
        ang = 2.0 * np.pi * m / period
        return np.cos(ang), -np.sin(ang)

    c1, i1 = cs(s1, s1, s1)
    w1 = np.concatenate([c1, i1], axis=0)
    twr, twi = cs(s1, SEQ_MINOR, s)
    c2, i2 = cs(SEQ_MINOR, SEQ_MINOR, SEQ_MINOR)
    w2 = np.block([[c2, -i2], [i2, c2]])
    cc, ic = cs(group, group, group)
    scale = 1.0 / math.sqrt(float(s) * float(group))
    return (jnp.asarray(w1, BF16), jnp.asarray(twr, F32), jnp.asarray(twi, F32),
            jnp.asarray(w2, BF16), jnp.asarray(cc * scale, BF16), jnp.asarray(-ic * scale, BF16))


def _b_stage1_kernel(x_ref, g_ref, w1_ref, twr_ref, twi_ref, br_ref, bi_ref, *, nb):
    s1 = x_ref.shape[1]
    d = g_ref.shape[2]
    for n in range(nb):
        xn = _rms(x_ref[0, :, n * d:(n + 1) * d], g_ref[0]).astype(BF16)
        y = _dot(w1_ref[...], xn)
        yr, yi = y[:s1], y[s1:]
        tr = twr_ref[0, :, n:n + 1]
        ti = twi_ref[0, :, n:n + 1]
        br_ref[0, :, n * d:(n + 1) * d] = (yr * tr - yi * ti).astype(BF16)
        bi_ref[0, :, n * d:(n + 1) * d] = (yr * ti + yi * tr).astype(BF16)


def _b_stage2_kernel(br_ref, bi_ref, h_ref, w2_ref, cc_ref, sc_ref, wo_ref, out_ref,
                     ar_ref, ai_ref, f_ref, *, ka, group):
    d = br_ref.shape[2]
    m = SEQ_MINOR
    for n in range(ka):
        cat = jnp.concatenate([br_ref[0, n * m:(n + 1) * m, :], bi_ref[0, n * m:(n + 1) * m, :]], axis=0)
        a = _dot(w2_ref[...], cat)
        ar_ref[n * m:(n + 1) * m, :] = a[:m].astype(BF16)
        ai_ref[n * m:(n + 1) * m, :] = a[m:].astype(BF16)
    for c in range(0, d, group):
        f = _dot(ar_ref[:, c:c + group], cc_ref[...]) + _dot(ai_ref[:, c:c + group], sc_ref[...])
        f_ref[:, c:c + group] = f.astype(BF16)
    for c in range(0, d, N_CHUNK):
        res = _dot(f_ref[...], wo_ref[0, :, c:c + N_CHUNK])
        for n in range(ka):
            lo = n * d + c
            out_ref[0, :, lo:lo + N_CHUNK] = h_ref[0, :, lo:lo + N_CHUNK] + res[n * m:(n + 1) * m, :]


def _const_block(shape):
    nd = len(shape)
    return pl.BlockSpec(shape, lambda *_: (0,) * nd, pipeline_mode=pl.Buffered(1))


def _fourier_mixer(h, g_all, layer, w_out, j):
    bsz, s, d = h.shape
    s1 = s // SEQ_MINOR
    group = d // N_FOURIER_GROUPS
    nb = 4
    ka = 4
    w1, twr, twi, w2, cc, sc = _dft_tables(s, group)
    steps1 = SEQ_MINOR // nb
    twr = twr.reshape(s1, steps1, nb).transpose(1, 0, 2)
    twi = twi.reshape(s1, steps1, nb).transpose(1, 0, 2)

    x1 = h.reshape(bsz, s1, SEQ_MINOR * d)
    blk1 = pl.BlockSpec((1, s1, nb * d), lambda b, i: (b, 0, i))
    tw_spec = pl.BlockSpec((1, s1, nb), lambda b, i: (i, 0, 0))
    br, bi = pl.pallas_call(
        functools.partial(_b_stage1_kernel, nb=nb),
        grid=(bsz, steps1),
        in_specs=[blk1, _layer_block(g_all.shape, layer), _const_block((2 * s1, s1)), tw_spec, tw_spec],
        out_specs=[blk1, blk1],
        out_shape=[jax.ShapeDtypeStruct((bsz, s1, SEQ_MINOR * d), BF16)] * 2,
        compiler_params=_params(2),
        name="fourier_stage1",
    )(x1, g_all, w1, twr, twi)

    br = br.reshape(bsz, s, d)
    bi = bi.reshape(bsz, s, d)
    h2 = h.reshape(bsz, SEQ_MINOR, s1 * d)
    rows = pl.BlockSpec((1, ka * SEQ_MINOR, d), lambda b, i: (b, i, 0))
    strided = pl.BlockSpec((1, SEQ_MINOR, ka * d), lambda b, i: (b, 0, i))
    out = pl.pallas_call(
        functools.partial(_b_stage2_kernel, ka=ka, group=group),
        grid=(bsz, s1 // ka),
        in_specs=[rows, rows, strided, _const_block((2 * SEQ_MINOR, 2 * SEQ_MINOR)),
                  _const_block((group, group)), _const_block((group, group)), _layer_block(w_out.shape, j)],
        out_specs=strided,
        out_shape=jax.ShapeDtypeStruct((bsz, SEQ_MINOR, s1 * d), F32),
        scratch_shapes=[pltpu.VMEM((ka * SEQ_MINOR, d), BF16)] * 3,
        compiler_params=_params(2),
        name="fourier_stage2_out",
    )(br, bi, h2, w2, cc, sc, w_out)
    return out.reshape(bsz, s, d)


def _kv_kernel(mem_ref, g_ref, wk_ref, wv_ref, k_ref, v_ref):
    memn = _rms(mem_ref[...], g_ref[0]).astype(BF16)
    k_ref[...] = _dot(memn, wk_ref[0]).astype(BF16)
    v_ref[...] = _dot(memn, wv_ref[0]).astype(BF16)


def _memory_kv(mem2d, g_all, wk, wv, layer):
    r, d = mem2d.shape
    col = pl.BlockSpec((1, d, N_CHUNK), lambda c: (layer, 0, c))
    out = pl.BlockSpec((r, N_CHUNK), lambda c: (0, c))
    return pl.pallas_call(
        _kv_kernel,
        grid=(d // N_CHUNK,),
        in_specs=[_const_block((r, d)), _layer_block(g_all.shape, layer), col, col],
        out_specs=[out, out],
        out_shape=[jax.ShapeDtypeStruct((r, d), BF16)] * 2,
        compiler_params=_params(1),
        name="memory_kv",
    )(mem2d, g_all, wk, wv)


def _split_bf16(a):
    hi = a.astype(BF16)
    lo = (a - hi.astype(F32)).astype(BF16)
    return hi, lo


def _route(xn, whl, bias):
    xh, xl = _split_bf16(xn)
    both = _dot(xh, whl)
    logits = both[:, :LANES] + both[:, LANES:] + _dot(xl, whl[:, :LANES]) + bias
    lane = lax.broadcasted_iota(jnp.int32, logits.shape, 1)
    neg = jnp.float32(-jnp.inf)
    big = jnp.int32(LANES)

    gl = jnp.where(lane < N_GROUPS, logits, neg)
    gmax = jnp.max(gl, axis=-1, keepdims=True)
    grp = jnp.min(jnp.where(gl == gmax, lane, big), axis=-1, keepdims=True)
    g_w = 1.0 / jnp.sum(jnp.exp(gl - gmax), axis=-1, keepdims=True)

    first = N_GROUPS + grp * EXPERTS_PER_GROUP
    el = jnp.where((lane >= first) & (lane < first + EXPERTS_PER_GROUP), logits, neg)
    m1 = jnp.max(el, axis=-1, keepdims=True)
    i1 = jnp.min(jnp.where(el == m1, lane, big), axis=-1, keepdims=True)
    el2 = jnp.where(lane == i1, neg, el)
    m2 = jnp.max(el2, axis=-1, keepdims=True)
    i2 = jnp.min(jnp.where(el2 == m2, lane, big), axis=-1, keepdims=True)
    e21 = jnp.exp(m2 - m1)
    w1 = g_w / (1.0 + e21)
    w2 = g_w * e21 / (1.0 + e21)

    a1 = i1 - first
    a2 = i2 - first
    lo = jnp.minimum(a1, a2)
    hi = jnp.maximum(a1, a2)
    pair = lo * 3 - jnp.right_shift(lo * (lo - 1), 1) + hi - lo - 1
    bucket = grp * N_PAIRS + pair
    w_lo = jnp.where(a1 < a2, w1, w2)
    w_hi = jnp.where(a1 < a2, w2, w1)
    routed = jnp.where(lane == 0, bucket.astype(F32),
                       jnp.where(lane == 1, w_lo, jnp.where(lane == 2, w_hi, 0.0)))
    return routed, (lane == bucket).astype(F32)


def _xattn_kernel(h_ref, g_ref, wq_ref, k_ref, v_ref, wo_ref, gm_ref, rw_ref, rb_ref,
                  out_ref, cnt_ref, xn_ref, o_ref):
    d = h_ref.shape[2]
    hd = d // N_HEADS
    scale = float(hd) ** -0.5
    xn_ref[...] = _rms(h_ref[0], g_ref[0]).astype(BF16)
    for c in range(0, d, hd):
        q = _dot(xn_ref[...], wq_ref[0, :, c:c + hd]).astype(BF16)
        s = lax.dot_general(q, k_ref[0, :, c:c + hd], (((1,), (1,)), ((), ())),
                            preferred_element_type=F32) * scale
        e = jnp.exp(s - jnp.max(s, axis=-1, keepdims=True))
        p = (e / jnp.sum(e, axis=-1, keepdims=True)).astype(BF16)
        o_ref[:, c:c + hd] = _dot(p, v_ref[0, :, c:c + hd]).astype(BF16)
    for c in range(0, d, N_CHUNK):
        out_ref[:, c:c + N_CHUNK] = h_ref[0, :, c:c + N_CHUNK] + _dot(o_ref[...], wo_ref[0, :, c:c + N_CHUNK])

    routed, onehot = _route(_rms(out_ref[:, 0:d], gm_ref[0]), rw_ref[...], rb_ref[...])
    out_ref[:, d:d + LANES] = routed

    @pl.when((pl.program_id(0) == 0) & (pl.program_id(1) == 0))
    def _():
        cnt_ref[...] = jnp.zeros_like(cnt_ref)

    cnt_ref[...] += jnp.sum(onehot, axis=0, keepdims=True)


def _cross_attention_route(h, g_all, wq, k, v, wo, g_moe_all, w_rg, b_rg, w_re, b_re, layer):
    bsz, s, d = h.shape
    n_mem = k.shape[1]
    tm = TM_DENSE
    per = s // tm
    n = w_rg.shape[1] + w_re.shape[1]
    w = jnp.zeros((d, LANES), F32).at[:, :n].set(jnp.concatenate([w_rg, w_re], axis=1))
    b = jnp.zeros((1, LANES), F32).at[0, :n].set(jnp.concatenate([b_rg, b_re]))
    whl = jnp.concatenate(_split_bf16(w), axis=1)
    tile = pl.BlockSpec((1, tm, d), lambda bb, i: (bb, i, 0))
    kv = pl.BlockSpec((1, n_mem, d), lambda bb, i: (bb, 0, 0))
    return pl.pallas_call(
        _xattn_kernel,
        grid=(bsz, per),
        in_specs=[tile, _layer_block(g_all.shape, layer), _layer_block(wq.shape, layer), kv, kv,
                  _layer_block(wo.shape, layer), _layer_block(g_moe_all.shape, layer),
                  _const_block((d, 2 * LANES)), _const_block((1, LANES))],
        out_specs=[pl.BlockSpec((tm, d + LANES), lambda bb, i: (bb * per + i, 0)),
                   pl.BlockSpec((1, LANES), lambda bb, i: (0, 0))],
        out_shape=[jax.ShapeDtypeStruct((bsz * s, d + LANES), F32),
                   jax.ShapeDtypeStruct((1, LANES), F32)],
        scratch_shapes=[pltpu.VMEM((tm, d), BF16)] * 2,
        compiler_params=_params(2),
        name="cross_attention_route",
    )(h, g_all, wq, k, v, wo, g_moe_all, whl, b)


def _wait_rows(count, block_copy, row_copy):
    n8 = pl.multiple_of(jnp.right_shift(count, 3) * SUBLANES, SUBLANES)

    @pl.when(n8 > 0)
    def _():
        block_copy(n8).wait()

    def one(r, carry):
        row_copy().wait()
        return carry

    lax.fori_loop(0, count - n8, one, 0)


def _expert_kernel(order_ref, c0_ref, nv_ref, ea_ref, eb_ref,
                   haug_ref, g_ref, gua_ref, gub_ref, da_ref, db_ref, gf_ref,
                   out_ref, xbuf, obuf, gsem, ssem, *, final_norm):
    i = pl.program_id(0)
    n = pl.num_programs(0)
    slot = lax.rem(i, 2)
    other = 1 - slot
    tm = xbuf.shape[1]
    d = obuf.shape[2]
    f = da_ref.shape[2]
    last_tok = order_ref.shape[0] - 1

    nv = nv_ref[i]
    nxt = jnp.minimum(i + 1, n - 1)
    prev = jnp.maximum(i - 1, 0)
    nv_next = jnp.where(i + 1 < n, nv_ref[nxt], 0)
    nv_prev = jnp.where(i >= 1, nv_ref[prev], 0)
    nv_prev2 = jnp.where(i >= 2, nv_ref[jnp.maximum(i - 2, 0)], 0)

    def gather_row(tok, s, r):
        return pltpu.make_async_copy(haug_ref.at[pl.ds(tok, 1), :], xbuf.at[s, pl.ds(r, 1), :], gsem.at[s])

    def scatter_row(tok, s, r):
        return pltpu.make_async_copy(obuf.at[s, pl.ds(r, 1), :], out_ref.at[pl.ds(tok, 1), :], ssem.at[s])

    def start_rows_loop(count, base, s, row_copy):
        def body(r, carry):
            row_copy(order_ref[base + r], s, r).start()
            return carry

        lax.fori_loop(0, count, body, 0)

    def start_rows_unrolled(count, base, s, row_copy):
        for r in range(tm):
            @pl.when(r < count)
            def _():
                row_copy(order_ref[jnp.minimum(base + r, last_tok)], s, r).start()

    def wait_gather(count, s):
        _wait_rows(count,
                   lambda m: pltpu.make_async_copy(haug_ref.at[pl.ds(0, m), :], xbuf.at[s, pl.ds(0, m), :], gsem.at[s]),
                   lambda: gather_row(0, s, 0))

    def wait_scatter(count, s):
        _wait_rows(count,
                   lambda m: pltpu.make_async_copy(obuf.at[s, pl.ds(0, m), :], out_ref.at[pl.ds(0, m), :], ssem.at[s]),
                   lambda: scatter_row(0, s, 0))

    @pl.when(i == 0)
    def _():
        xbuf[...] = jnp.zeros_like(xbuf)
        start_rows_loop(nv, c0_ref[0], 0, gather_row)

    wait_gather(nv, slot)

    def compute_tile(rows):
        x = xbuf[slot, 0:rows, 0:d]
        xn = _rms(x, g_ref[0]).astype(BF16)
        acc = x
        for lane, (gu_ref, dn_ref) in enumerate(((gua_ref, da_ref), (gub_ref, db_ref))):
            gu = _dot(xn, gu_ref[0, 0])
            gate, up = gu[:, :f], gu[:, f:]
            act = (gate * _sigmoid(gate) * up).astype(BF16)
            acc = acc + _dot(act, dn_ref[0, 0]) * xbuf[slot, 0:rows, d + 1 + lane:d + 2 + lane]
        if final_norm:
            acc = _rms(acc, gf_ref[...])
        start_rows_unrolled(nv_next, c0_ref[nxt], other, gather_row)
        start_rows_unrolled(nv_prev, c0_ref[prev], other, scatter_row)
        wait_scatter(nv_prev2, slot)
        obuf[slot, 0:rows] = acc

    half = tm // 2

    @pl.when(nv > half)
    def _():
        compute_tile(tm)

    @pl.when((nv > 0) & (nv <= half))
    def _():
        compute_tile(half)

    @pl.when(nv == 0)
    def _():
        start_rows_loop(nv_prev, c0_ref[prev], other, scatter_row)
        wait_scatter(nv_prev2, slot)

    @pl.when(i == n - 1)
    def _():
        wait_scatter(nv_prev, other)


def _moe_experts(haug, counts, g_all, w_gate_up, w_down, layer, g_final, final_norm):
    t, da = haug.shape
    d = da - LANES
    f2 = w_gate_up.shape[3]
    f = w_down.shape[2]
    tm = TM_MOE
    n_tiles = t // tm + N_BUCKETS

    bucket = haug[:, d].astype(jnp.int32)
    order = jnp.argsort(bucket, stable=True).astype(jnp.int32)
    sizes = counts[0, :N_BUCKETS].astype(jnp.int32)
    tiles_per = (sizes + tm - 1) // tm
    tile_end = jnp.cumsum(tiles_per)
    tile_start = tile_end - tiles_per
    n_used = tile_end[-1]
    start_sorted = jnp.cumsum(sizes) - sizes
    ids = jnp.arange(n_tiles, dtype=jnp.int32)
    tb = jnp.sum((tile_end[None, :] <= jnp.minimum(ids, n_used - 1)[:, None]).astype(jnp.int32), axis=1)
    k = ids - tile_start[tb]
    nv = jnp.where(ids < n_used, jnp.clip(sizes[tb] - k * tm, 0, tm), 0).astype(jnp.int32)
    c0 = jnp.where(ids < n_used, start_sorted[tb] + k * tm, 0).astype(jnp.int32)
    grp = tb // N_PAIRS
    pair = tb % N_PAIRS
    tile_a = (grp * EXPERTS_PER_GROUP + jnp.asarray(PAIR_LO, jnp.int32)[pair]).astype(jnp.int32)
    tile_b = (grp * EXPERTS_PER_GROUP + jnp.asarray(PAIR_HI, jnp.int32)[pair]).astype(jnp.int32)

    any_space = pl.BlockSpec(memory_space=pl.ANY)
    grid_spec = pltpu.PrefetchScalarGridSpec(
        num_scalar_prefetch=5,
        grid=(n_tiles,),
        in_specs=[any_space,
                  pl.BlockSpec((1, 1, d), lambda i, o, c, v, ea, eb: (layer, 0, 0)),
                  pl.BlockSpec((1, 1, d, f2), lambda i, o, c, v, ea, eb: (layer, ea[i], 0, 0)),
                  pl.BlockSpec((1, 1, d, f2), lambda i, o, c, v, ea, eb: (layer, eb[i], 0, 0)),
                  pl.BlockSpec((1, 1, f, d), lambda i, o, c, v, ea, eb: (layer, ea[i], 0, 0)),
                  pl.BlockSpec((1, 1, f, d), lambda i, o, c, v, ea, eb: (layer, eb[i], 0, 0)),
                  pl.BlockSpec((1, d), lambda i, o, c, v, ea, eb: (0, 0))],
        out_specs=any_space,
        scratch_shapes=[pltpu.VMEM((2, tm, da), F32), pltpu.VMEM((2, tm, d), F32),
                        pltpu.SemaphoreType.DMA((2,)), pltpu.SemaphoreType.DMA((2,))],
    )
    return pl.pallas_call(
        functools.partial(_expert_kernel, final_norm=final_norm),
        grid_spec=grid_spec,
        out_shape=jax.ShapeDtypeStruct((t, d), F32),
        compiler_params=_params(1),
        name="moe_experts",
    )(order, c0, nv, tile_a, tile_b, haug, g_all, w_gate_up, w_gate_up, w_down, w_down, g_final.reshape(1, d))


def kernel(x, mem, g_mix, g_xattn, g_mem, g_moe, g_final, w_a_in, w_a_conv, b_a_conv, g_a_ln, b_a_ln, w_a_out, w_b_out, w_c_in, w_c_conv, w_c_out, w_xq, w_xk, w_xv, w_xo, w_route_group, b_route_group, w_route_expert, b_route_expert, w_gate_up, w_down):
    bsz, s, d = x.shape
    t = bsz * s
    depth = g_mix.shape[0]
    mem2d = mem.reshape(bsz * mem.shape[1], d)
    bf = lambda w: w.astype(BF16)
    g_mix, g_xattn, g_mem, g_moe = _rows3(g_mix), _rows3(g_xattn), _rows3(g_mem), _rows3(g_moe)
    b_a_conv, g_a_ln, b_a_ln = _rows3(b_a_conv), _rows3(g_a_ln), _rows3(b_a_ln)
    w_a_in, w_a_out, w_b_out, w_c_in, w_c_out = bf(w_a_in), bf(w_a_out), bf(w_b_out), bf(w_c_in), bf(w_c_out)
    w_xq, w_xk, w_xv, w_xo = bf(w_xq), bf(w_xk), bf(w_xv), bf(w_xo)
    w_gate_up, w_down = bf(w_gate_up), bf(w_down)

    h = x
    for i in range(depth):
        kind, j = i % N_MIXERS, i // N_MIXERS
        if kind == 0:
            u = _a_in(h.reshape(t, d), g_mix, i, w_a_in, j)
            h = _a_conv(u.reshape(bsz, s, d), h, w_a_conv, b_a_conv, g_a_ln, b_a_ln, w_a_out, j)
        elif kind == 1:
            h = _fourier_mixer(h, g_mix, i, w_b_out, j)
        else:
            gate, cv = _c_in(h.reshape(t, d), g_mix, i, w_c_in, j)
            h = _c_conv(cv.reshape(bsz, s, d), gate.reshape(bsz, s, d), h, w_c_conv, w_c_out, j)
        k, v = _memory_kv(mem2d, g_mem, w_xk, w_xv, i)
        haug, counts = _cross_attention_route(
            h, g_xattn, w_xq, k.reshape(bsz, -1, d), v.reshape(bsz, -1, d), w_xo, g_moe,
            w_route_group[i], b_route_group[i], w_route_expert[i], b_route_expert[i], i)
        h = _moe_experts(haug, counts, g_moe, w_gate_up, w_down, i, g_final, i == depth - 1).reshape(bsz, s, d)
    return h
```

```python
import functools
import math

import numpy as np
import jax
import jax.numpy as jnp
from jax import lax
from jax.experimental import pallas as pl
from jax.experimental.pallas import tpu as pltpu

F32 = jnp.float32
BF16 = jnp.bfloat16
EPS = 1e-6

N_MIXERS = 3
N_FOURIER_GROUPS = 8
N_HEADS = 4
N_GROUPS = 8
EXPERTS_PER_GROUP = 4
N_PAIRS = 6
N_BUCKETS = N_GROUPS * N_PAIRS
PAIR_LO = (0, 0, 0, 1, 1, 2)
PAIR_HI = (1, 2, 3, 2, 3, 3)

LANES = 128
SUBLANES = 8
SEQ_MINOR = 128
HALO = 16
VMEM_LIMIT_BYTES = 56 * 1024 * 1024

TM_DENSE = 512
TM_MOE = 256
ROW_CHUNK = 64
N_CHUNK = 512
CONV_COLS = 256


def _params(n_axes):
    return pltpu.CompilerParams(
        dimension_semantics=("arbitrary",) * n_axes,
        vmem_limit_bytes=VMEM_LIMIT_BYTES)


def _layer_block(shape, layer):
    zeros = (0,) * (len(shape) - 1)
    return pl.BlockSpec((1,) + tuple(shape[1:]), lambda *_: (layer,) + zeros, pipeline_mode=pl.Buffered(1))


def _rows3(a):
    return a.reshape(a.shape[0], 1, a.shape[1])


def _rms(x, g):
    return x * lax.rsqrt(jnp.mean(x * x, axis=-1, keepdims=True) + EPS) * g


def _sigmoid(x):
    return 1.0 / (1.0 + jnp.exp(-x))


def _dot(a, b):
    return jnp.dot(a, b, preferred_element_type=F32)


def _a_in_kernel(x_ref, g_ref, w_ref, u_ref, xn_ref):
    d = x_ref.shape[1]
    xn_ref[...] = _rms(x_ref[...], g_ref[0]).astype(BF16)
    for c in range(0, d, N_CHUNK):
        a = _dot(xn_ref[...], w_ref[0, :, c:c + N_CHUNK])
        gate = _dot(xn_ref[...], w_ref[0, :, d + c:d + c + N_CHUNK])
        u_ref[:, c:c + N_CHUNK] = (a * _sigmoid(gate)).astype(BF16)


def _a_in(h, g_all, layer, w_all, j):
    t, d = h.shape
    return pl.pallas_call(
        _a_in_kernel,
        grid=(t // TM_DENSE,),
        in_specs=[pl.BlockSpec((TM_DENSE, d), lambda i: (i, 0)),
                  _layer_block(g_all.shape, layer),
                  _layer_block(w_all.shape, j)],
        out_specs=pl.BlockSpec((TM_DENSE, d), lambda i: (i, 0)),
        out_shape=jax.ShapeDtypeStruct((t, d), BF16),
        scratch_shapes=[pltpu.VMEM((TM_DENSE, d), BF16)],
        compiler_params=_params(1),
        name="conformer_in",
    )(h, g_all, w_all)


def _c_in_kernel(x_ref, g_ref, w_ref, b_ref, cv_ref, xn_ref):
    d = x_ref.shape[1]
    xn_ref[...] = _rms(x_ref[...], g_ref[0]).astype(BF16)
    for c in range(0, d, N_CHUNK):
        b_ref[:, c:c + N_CHUNK] = _dot(xn_ref[...], w_ref[0, :, c:c + N_CHUNK]).astype(BF16)
        cg = _dot(xn_ref[...], w_ref[0, :, d + c:d + c + N_CHUNK])
        v = _dot(xn_ref[...], w_ref[0, :, 2 * d + c:2 * d + c + N_CHUNK])
        cv_ref[:, c:c + N_CHUNK] = (cg * v).astype(BF16)


def _c_in(h, g_all, layer, w_all, j):
    t, d = h.shape
    tile = pl.BlockSpec((TM_DENSE, d), lambda i: (i, 0))
    return pl.pallas_call(
        _c_in_kernel,
        grid=(t // TM_DENSE,),
        in_specs=[tile, _layer_block(g_all.shape, layer), _layer_block(w_all.shape, j)],
        out_specs=[tile, tile],
        out_shape=[jax.ShapeDtypeStruct((t, d), BF16)] * 2,
        scratch_shapes=[pltpu.VMEM((TM_DENSE, d), BF16)],
        compiler_params=_params(1),
        name="shortconv_in",
    )(h, g_all, w_all)


def _fill_ext(ext_ref, cur_ref, prev_ref, next_ref, tm):
    i = pl.program_id(1)
    n = pl.num_programs(1)
    prev = prev_ref[0].astype(F32)
    nxt = next_ref[0].astype(F32)
    ext_ref[0:HALO, :] = jnp.where(i > 0, prev, jnp.zeros_like(prev))
    ext_ref[HALO + tm:HALO + tm + HALO, :] = jnp.where(i < n - 1, nxt, jnp.zeros_like(nxt))

    def body(j, carry):
        r0 = pl.multiple_of(j * ROW_CHUNK, ROW_CHUNK)
        ext_ref[pl.ds(HALO + r0, ROW_CHUNK), :] = cur_ref[0, pl.ds(r0, ROW_CHUNK), :].astype(F32)
        return carry

    lax.fori_loop(0, tm // ROW_CHUNK, body, 0)


def _depthwise(ext_ref, w_ref, cv_ref, width, tm):
    d = cv_ref.shape[1]
    rows = ROW_CHUNK
    span = rows + 2 * HALO
    off = HALO - width // 2

    def body(j, carry):
        r0 = pl.multiple_of(j * rows, rows)
        for c in range(0, d, CONV_COLS):
            blk = ext_ref[pl.ds(r0, span), c:c + CONV_COLS]
            acc = jnp.zeros((rows, CONV_COLS), F32)
            for r in range(SUBLANES):
                taps = [k for k in range(width) if (off + k) % SUBLANES == r]
                if not taps:
                    continue
                shifted = blk if r == 0 else pltpu.roll(blk, span - r, axis=0)
                for k in taps:
                    q = (off + k) // SUBLANES * SUBLANES
                    acc = acc + shifted[q:q + rows, :] * w_ref[0, k:k + 1, c:c + CONV_COLS]
            cv_ref[pl.ds(r0, rows), c:c + CONV_COLS] = acc
        return carry

    lax.fori_loop(0, tm // rows, body, 0)


def _project_residual(out_ref, h_ref, v_ref, w_ref):
    d = v_ref.shape[1]
    for c in range(0, d, N_CHUNK):
        out_ref[0, :, c:c + N_CHUNK] = h_ref[0, :, c:c + N_CHUNK] + _dot(v_ref[...], w_ref[0, :, c:c + N_CHUNK])


def _a_conv_kernel(cur_ref, prev_ref, next_ref, h_ref, wc_ref, bc_ref, gl_ref, bl_ref, wo_ref,
                   out_ref, ext_ref, cv_ref, v_ref, *, width):
    tm = cur_ref.shape[1]
    _fill_ext(ext_ref, cur_ref, prev_ref, next_ref, tm)
    _depthwise(ext_ref, wc_ref, cv_ref, width, tm)

    def body(j, carry):
        r0 = pl.multiple_of(j * ROW_CHUNK, ROW_CHUNK)
        u = cv_ref[pl.ds(r0, ROW_CHUNK), :] + bc_ref[0]
        mu = jnp.mean(u, axis=-1, keepdims=True)
        uc = u - mu
        var = jnp.mean(uc * uc, axis=-1, keepdims=True)
        y = uc * lax.rsqrt(var + EPS) * gl_ref[0] + bl_ref[0]
        v_ref[pl.ds(r0, ROW_CHUNK), :] = (y * _sigmoid(y)).astype(BF16)
        return carry

    lax.fori_loop(0, tm // ROW_CHUNK, body, 0)
    _project_residual(out_ref, h_ref, v_ref, wo_ref)


def _c_conv_kernel(cur_ref, prev_ref, next_ref, gate_ref, h_ref, wc_ref, wo_ref,
                   out_ref, ext_ref, cv_ref, v_ref, *, width):
    tm = cur_ref.shape[1]
    _fill_ext(ext_ref, cur_ref, prev_ref, next_ref, tm)
    _depthwise(ext_ref, wc_ref, cv_ref, width, tm)

    def body(j, carry):
        r0 = pl.multiple_of(j * ROW_CHUNK, ROW_CHUNK)
        gate = gate_ref[0, pl.ds(r0, ROW_CHUNK), :].astype(F32)
        v_ref[pl.ds(r0, ROW_CHUNK), :] = (gate * cv_ref[pl.ds(r0, ROW_CHUNK), :]).astype(BF16)
        return carry

    lax.fori_loop(0, tm // ROW_CHUNK, body, 0)
    _project_residual(out_ref, h_ref, v_ref, wo_ref)


def _conv_specs(s, d, tm):
    per = tm // HALO
    last = s // HALO - 1
    cur = pl.BlockSpec((1, tm, d), lambda b, i: (b, i, 0))
    prev = pl.BlockSpec((1, HALO, d), lambda b, i: (b, jnp.maximum(i * per - 1, 0), 0))
    nxt = pl.BlockSpec((1, HALO, d), lambda b, i: (b, jnp.minimum((i + 1) * per, last), 0))
    return cur, prev, nxt


def _conv_scratch(tm, d):
    return [pltpu.VMEM((tm + 2 * HALO, d), F32), pltpu.VMEM((tm, d), F32), pltpu.VMEM((tm, d), BF16)]


def _a_conv(u, h, w_conv, b_conv, g_ln, b_ln, w_out, j):
    bsz, s, d = h.shape
    tm = TM_DENSE
    cur, prev, nxt = _conv_specs(s, d, tm)
    return pl.pallas_call(
        functools.partial(_a_conv_kernel, width=w_conv.shape[1]),
        grid=(bsz, s // tm),
        in_specs=[cur, prev, nxt, cur, _layer_block(w_conv.shape, j), _layer_block(b_conv.shape, j),
                  _layer_block(g_ln.shape, j), _layer_block(b_ln.shape, j), _layer_block(w_out.shape, j)],
        out_specs=cur,
        out_shape=jax.ShapeDtypeStruct((bsz, s, d), F32),
        scratch_shapes=_conv_scratch(tm, d),
        compiler_params=_params(2),
        name="conformer_conv_out",
    )(u, u, u, h, w_conv, b_conv, g_ln, b_ln, w_out)


def _c_conv(cv, gate, h, w_conv, w_out, j):
    bsz, s, d = h.shape
    tm = TM_DENSE
    cur, prev, nxt = _conv_specs(s, d, tm)
    return pl.pallas_call(
        functools.partial(_c_conv_kernel, width=w_conv.shape[1]),
        grid=(bsz, s // tm),
        in_specs=[cur, prev, nxt, cur, cur, _layer_block(w_conv.shape, j), _layer_block(w_out.shape, j)],
        out_specs=cur,
        out_shape=jax.ShapeDtypeStruct((bsz, s, d), F32),
        scratch_shapes=_conv_scratch(tm, d),
        compiler_params=_params(2),
        name="shortconv_conv_out",
    )(cv, cv, cv, gate, h, w_conv, w_out)


def _dft_tables(s, group):
    s1 = s // SEQ_MINOR

    def cs(n_rows, n_cols, period):
        m = (np.outer(np.arange(n_rows), np.arange(n_cols)) % period).astype(np.float64)
        ang = 2.0 * np.pi * m / period
        return np.cos(ang), -np.sin(ang)

    c1, i1 = cs(s1, s1, s1)
    w1 = np.concatenate([c1, i1], axis=0)
    twr, twi = cs(s1, SEQ_MINOR, s)
    c2, i2 = cs(SEQ_MINOR, SEQ_MINOR, SEQ_MINOR)
    w2 = np.block([[c2, -i2], [i2, c2]])
    cc, ic = cs(group, group, group)
    scale = 1.0 / math.sqrt(float(s) * float(group))
    return (jnp.asarray(w1, BF16), jnp.asarray(twr, F32), jnp.asarray(twi, F32),
            jnp.asarray(w2, BF16), jnp.asarray(cc * scale, BF16), jnp.asarray(-ic * scale, BF16))


def _b_stage1_kernel(x_ref, g_ref, w1_ref, twr_ref, twi_ref, br_ref, bi_ref, *, nb):
    s1 = x_ref.shape[1]
    d = g_ref.shape[2]
    for n in range(nb):
        xn = _rms(x_ref[0, :, n * d:(n + 1) * d], g_ref[0]).astype(BF16)
        y = _dot(w1_ref[...], xn)
        yr, yi = y[:s1], y[s1:]
        tr = twr_ref[0, :, n:n + 1]
        ti = twi_ref[0, :, n:n + 1]
        br_ref[0, :, n * d:(n + 1) * d] = (yr * tr - yi * ti).astype(BF16)
        bi_ref[0, :, n * d:(n + 1) * d] = (yr * ti + yi * tr).astype(BF16)


def _b_stage2_kernel(br_ref, bi_ref, h_ref, w2_ref, cc_ref, sc_ref, wo_ref, out_ref,
                     ar_ref, ai_ref, f_ref, *, ka, group):
    d = br_ref.shape[2]
    m = SEQ_MINOR
    for n in range(ka):
        cat = jnp.concatenate([br_ref[0, n * m:(n + 1) * m, :], bi_ref[0, n * m:(n + 1) * m, :]], axis=0)
        a = _dot(w2_ref[...], cat)
        ar_ref[n * m:(n + 1) * m, :] = a[:m].astype(BF16)
        ai_ref[n * m:(n + 1) * m, :] = a[m:].astype(BF16)
    for c in range(0, d, group):
        f = _dot(ar_ref[:, c:c + group], cc_ref[...]) + _dot(ai_ref[:, c:c + group], sc_ref[...])
        f_ref[:, c:c + group] = f.astype(BF16)
    for c in range(0, d, N_CHUNK):
        res = _dot(f_ref[...], wo_ref[0, :, c:c + N_CHUNK])
        for n in range(ka):
            lo = n * d + c
            out_ref[0, :, lo:lo + N_CHUNK] = h_ref[0, :, lo:lo + N_CHUNK] + res[n * m:(n + 1) * m, :]


def _const_block(shape):
    nd = len(shape)
    return pl.BlockSpec(shape, lambda *_: (0,) * nd, pipeline_mode=pl.Buffered(1))


def _fourier_mixer(h, g_all, layer, w_out, j):
    bsz, s, d = h.shape
    s1 = s // SEQ_MINOR
    group = d // N_FOURIER_GROUPS
    nb = 4
    ka = 4
    w1, twr, twi, w2, cc, sc = _dft_tables(s, group)
    steps1 = SEQ_MINOR // nb
    twr = twr.reshape(s1, steps1, nb).transpose(1, 0, 2)
    twi = twi.reshape(s1, steps1, nb).transpose(1, 0, 2)

    x1 = h.reshape(bsz, s1, SEQ_MINOR * d)
    blk1 = pl.BlockSpec((1, s1, nb * d), lambda b, i: (b, 0, i))
    tw_spec = pl.BlockSpec((1, s1, nb), lambda b, i: (i, 0, 0))
    br, bi = pl.pallas_call(
        functools.partial(_b_stage1_kernel, nb=nb),
        grid=(bsz, steps1),
        in_specs=[blk1, _layer_block(g_all.shape, layer), _const_block((2 * s1, s1)), tw_spec, tw_spec],
        out_specs=[blk1, blk1],
        out_shape=[jax.ShapeDtypeStruct((bsz, s1, SEQ_MINOR * d), BF16)] * 2,
        compiler_params=_params(2),
        name="fourier_stage1",
    )(x1, g_all, w1, twr, twi)

    br = br.reshape(bsz, s, d)
    bi = bi.reshape(bsz, s, d)
    h2 = h.reshape(bsz, SEQ_MINOR, s1 * d)
    rows = pl.BlockSpec((1, ka * SEQ_MINOR, d), lambda b, i: (b, i, 0))
    strided = pl.BlockSpec((1, SEQ_MINOR, ka * d), lambda b, i: (b, 0, i))
    out = pl.pallas_call(
        functools.partial(_b_stage2_kernel, ka=ka, group=group),
        grid=(bsz, s1 // ka),
        in_specs=[rows, rows, strided, _const_block((2 * SEQ_MINOR, 2 * SEQ_MINOR)),
                  _const_block((group, group)), _const_block((group, group)), _layer_block(w_out.shape, j)],
        out_specs=strided,
        out_shape=jax.ShapeDtypeStruct((bsz, SEQ_MINOR, s1 * d), F32),
        scratch_shapes=[pltpu.VMEM((ka * SEQ_MINOR, d), BF16)] * 3,
        compiler_params=_params(2),
        name="fourier_stage2_out",
    )(br, bi, h2, w2, cc, sc, w_out)
    return out.reshape(bsz, s, d)


def _kv_kernel(mem_ref, g_ref, wk_ref, wv_ref, k_ref, v_ref):
    memn = _rms(mem_ref[...], g_ref[0]).astype(BF16)
    k_ref[...] = _dot(memn, wk_ref[0]).astype(BF16)
    v_ref[...] = _dot(memn, wv_ref[0]).astype(BF16)


def _memory_kv(mem2d, g_all, wk, wv, layer):
    r, d = mem2d.shape
    col = pl.BlockSpec((1, d, N_CHUNK), lambda c: (layer, 0, c))
    out = pl.BlockSpec((r, N_CHUNK), lambda c: (0, c))
    return pl.pallas_call(
        _kv_kernel,
        grid=(d // N_CHUNK,),
        in_specs=[_const_block((r, d)), _layer_block(g_all.shape, layer), col, col],
        out_specs=[out, out],
        out_shape=[jax.ShapeDtypeStruct((r, d), BF16)] * 2,
        compiler_params=_params(1),
        name="memory_kv",
    )(mem2d, g_all, wk, wv)


def _split_bf16(a):
    hi = a.astype(BF16)
    lo = (a - hi.astype(F32)).astype(BF16)
    return hi, lo


def _route(xn, whl, bias):
    xh, xl = _split_bf16(xn)
    both = _dot(xh, whl)
    logits = both[:, :LANES] + both[:, LANES:] + _dot(xl, whl[:, :LANES]) + bias
    lane = lax.broadcasted_iota(jnp.int32, logits.shape, 1)
    neg = jnp.float32(-jnp.inf)
    big = jnp.int32(LANES)

    gl = jnp.where(lane < N_GROUPS, logits, neg)
    gmax = jnp.max(gl, axis=-1, keepdims=True)
    grp = jnp.min(jnp.where(gl == gmax, lane, big), axis=-1, keepdims=True)
    g_w = 1.0 / jnp.sum(jnp.exp(gl - gmax), axis=-1, keepdims=True)

    first = N_GROUPS + grp * EXPERTS_PER_GROUP
    el = jnp.where((lane >= first) & (lane < first + EXPERTS_PER_GROUP), logits, neg)
    m1 = jnp.max(el, axis=-1, keepdims=True)
    i1 = jnp.min(jnp.where(el == m1, lane, big), axis=-1, keepdims=True)
    el2 = jnp.where(lane == i1, neg, el)
    m2 = jnp.max(el2, axis=-1, keepdims=True)
    i2 = jnp.min(jnp.where(el2 == m2, lane, big), axis=-1, keepdims=True)
    e21 = jnp.exp(m2 - m1)
    w1 = g_w / (1.0 + e21)
    w2 = g_w * e21 / (1.0 + e21)

    a1 = i1 - first
    a2 = i2 - first
    lo = jnp.minimum(a1, a2)
    hi = jnp.maximum(a1, a2)
    pair = lo * 3 - jnp.right_shift(lo * (lo - 1), 1) + hi - lo - 1
    bucket = grp * N_PAIRS + pair
    w_lo = jnp.where(a1 < a2, w1, w2)
    w_hi = jnp.where(a1 < a2, w2, w1)
    routed = jnp.where(lane == 0, bucket.astype(F32),
                       jnp.where(lane == 1, w_lo, jnp.where(lane == 2, w_hi, 0.0)))
    return routed, (lane == bucket).astype(F32)


def _xattn_kernel(h_ref, g_ref, wq_ref, k_ref, v_ref, wo_ref, gm_ref, rw_ref, rb_ref,
                  out_ref, cnt_ref, xn_ref, o_ref):
    d = h_ref.shape[2]
    hd = d // N_HEADS
    scale = float(hd) ** -0.5
    xn_ref[...] = _rms(h_ref[0], g_ref[0]).astype(BF16)
    for c in range(0, d, hd):
        q = _dot(xn_ref[...], wq_ref[0, :, c:c + hd]).astype(BF16)
        s = lax.dot_general(q, k_ref[0, :, c:c + hd], (((1,), (1,)), ((), ())),
                            preferred_element_type=F32) * scale
        e = jnp.exp(s - jnp.max(s, axis=-1, keepdims=True))
        p = (e / jnp.sum(e, axis=-1, keepdims=True)).astype(BF16)
        o_ref[:, c:c + hd] = _dot(p, v_ref[0, :, c:c + hd]).astype(BF16)
    for c in range(0, d, N_CHUNK):
        out_ref[:, c:c + N_CHUNK] = h_ref[0, :, c:c + N_CHUNK] + _dot(o_ref[...], wo_ref[0, :, c:c + N_CHUNK])

    routed, onehot = _route(_rms(out_ref[:, 0:d], gm_ref[0]), rw_ref[...], rb_ref[...])
    out_ref[:, d:d + LANES] = routed

    @pl.when((pl.program_id(0) == 0) & (pl.program_id(1) == 0))
    def _():
        cnt_ref[...] = jnp.zeros_like(cnt_ref)

    cnt_ref[...] += jnp.sum(onehot, axis=0, keepdims=True)


def _cross_attention_route(h, g_all, wq, k, v, wo, g_moe_all, w_rg, b_rg, w_re, b_re, layer):
    bsz, s, d = h.shape
    n_mem = k.shape[1]
    tm = TM_DENSE
    per = s // tm
    n = w_rg.shape[1] + w_re.shape[1]
    w = jnp.zeros((d, LANES), F32).at[:, :n].set(jnp.concatenate([w_rg, w_re], axis=1))
    b = jnp.zeros((1, LANES), F32).at[0, :n].set(jnp.concatenate([b_rg, b_re]))
    whl = jnp.concatenate(_split_bf16(w), axis=1)
    tile = pl.BlockSpec((1, tm, d), lambda bb, i: (bb, i, 0))
    kv = pl.BlockSpec((1, n_mem, d), lambda bb, i: (bb, 0, 0))
    return pl.pallas_call(
        _xattn_kernel,
        grid=(bsz, per),
        in_specs=[tile, _layer_block(g_all.shape, layer), _layer_block(wq.shape, layer), kv, kv,
                  _layer_block(wo.shape, layer), _layer_block(g_moe_all.shape, layer),
                  _const_block((d, 2 * LANES)), _const_block((1, LANES))],
        out_specs=[pl.BlockSpec((tm, d + LANES), lambda bb, i: (bb * per + i, 0)),
                   pl.BlockSpec((1, LANES), lambda bb, i: (0, 0))],
        out_shape=[jax.ShapeDtypeStruct((bsz * s, d + LANES), F32),
                   jax.ShapeDtypeStruct((1, LANES), F32)],
        scratch_shapes=[pltpu.VMEM((tm, d), BF16)] * 2,
        compiler_params=_params(2),
        name="cross_attention_route",
    )(h, g_all, wq, k, v, wo, g_moe_all, whl, b)


def _wait_rows(count, block_copy, row_copy):
    n8 = pl.multiple_of(jnp.right_shift(count, 3) * SUBLANES, SUBLANES)

    @pl.when(n8 > 0)
    def _():
        block_copy(n8).wait()

    def one(r, carry):
        row_copy().wait()
        return carry

    lax.fori_loop(0, count - n8, one, 0)


def _expert_kernel(order_ref, c0_ref, nv_ref, ea_ref, eb_ref,
                   haug_ref, g_ref, gua_ref, gub_ref, da_ref, db_ref, gf_ref,
                   out_ref, xbuf, obuf, gsem, ssem, *, final_norm):
    i = pl.program_id(0)
    n = pl.num_programs(0)
    parity = lax.rem(i, 2)
    tm = xbuf.shape[1]
    d = obuf.shape[2]
    f = da_ref.shape[2]

    nv = nv_ref[i]
    nxt = jnp.minimum(i + 1, n - 1)
    prev = jnp.maximum(i - 1, 0)
    nv_next = jnp.where(i + 1 < n, nv_ref[nxt], 0)
    nv_prev = jnp.where(i >= 1, nv_ref[prev], 0)
    nv_prev2 = jnp.where(i >= 2, nv_ref[jnp.maximum(i - 2, 0)], 0)

    def gather_row(tok, s, r):
        return pltpu.make_async_copy(haug_ref.at[pl.ds(tok, 1), :], xbuf.at[s, pl.ds(r, 1), :], gsem.at[s])

    def scatter_row(tok, s, r):
        return pltpu.make_async_copy(obuf.at[s, pl.ds(r, 1), :], out_ref.at[pl.ds(tok, 1), :], ssem.at[s])

    def start_gather(count, base, s):
        @pl.when(count > 0)
        def _():
            for r in range(tm):
                gather_row(order_ref[base + r], s, r).start()

    def start_scatter(count, base, s):
        @pl.when(count == tm)
        def _():
            for r in range(tm):
                scatter_row(order_ref[base + r], s, r).start()

        @pl.when((count > 0) & (count < tm))
        def _():
            for r in range(tm):
                @pl.when(r < count)
                def _():
                    scatter_row(order_ref[base + r], s, r).start()

    def wait_gather(count, s):
        @pl.when(count > 0)
        def _():
            pltpu.make_async_copy(haug_ref.at[pl.ds(0, tm), :], xbuf.at[s], gsem.at[s]).wait()

    def wait_scatter(count, s):
        _wait_rows(count,
                   lambda m: pltpu.make_async_copy(obuf.at[s, pl.ds(0, m), :], out_ref.at[pl.ds(0, m), :], ssem.at[s]),
                   lambda: scatter_row(0, s, 0))

    @pl.when(i == 0)
    def _():
        start_gather(nv, c0_ref[0], 0)

    for slot in range(2):
        @pl.when(parity == slot)
        def _():
            start_gather(nv_next, c0_ref[nxt], 1 - slot)
            start_scatter(nv_prev, c0_ref[prev], 1 - slot)
            wait_gather(nv, slot)

    def compute_tile(rows):
        x = xbuf[parity, 0:rows, 0:d]
        xn = _rms(x, g_ref[0]).astype(BF16)
        acc = x
        for lane, (gu_ref, dn_ref) in enumerate(((gua_ref, da_ref), (gub_ref, db_ref))):
            gu = _dot(xn, gu_ref[0, 0])
            gate, up = gu[:, :f], gu[:, f:]
            act = (gate * _sigmoid(gate) * up).astype(BF16)
            acc = acc + _dot(act, dn_ref[0, 0]) * xbuf[parity, 0:rows, d + 1 + lane:d + 2 + lane]
        if final_norm:
            acc = _rms(acc, gf_ref[...])
        wait_scatter(nv_prev2, parity)
        obuf[parity, 0:rows] = acc

    half = tm // 2

    @pl.when(nv > half)
    def _():
        compute_tile(tm)

    @pl.when((nv > 0) & (nv <= half))
    def _():
        compute_tile(half)

    @pl.when(nv == 0)
    def _():
        wait_scatter(nv_prev2, parity)

    @pl.when(i == n - 1)
    def _():
        wait_scatter(nv_prev, 1 - parity)


def _moe_experts(haug, counts, g_all, w_gate_up, w_down, layer, g_final, final_norm):
    t, da = haug.shape
    d = da - LANES
    f2 = w_gate_up.shape[3]
    f = w_down.shape[2]
    tm = TM_MOE
    n_tiles = t // tm + N_BUCKETS

    bucket = haug[:, d].astype(jnp.int32)
    order = jnp.argsort(bucket, stable=True).astype(jnp.int32)
    order = jnp.concatenate([order, jnp.zeros((tm,), jnp.int32)])
    sizes = counts[0, :N_BUCKETS].astype(jnp.int32)
    tiles_per = (sizes + tm - 1) // tm
    tile_end = jnp.cumsum(tiles_per)
    tile_start = tile_end - tiles_per
    n_used = tile_end[-1]
    start_sorted = jnp.cumsum(sizes) - sizes
    ids = jnp.arange(n_tiles, dtype=jnp.int32)
    tb = jnp.sum((tile_end[None, :] <= jnp.minimum(ids, n_used - 1)[:, None]).astype(jnp.int32), axis=1)
    k = ids - tile_start[tb]
    nv = jnp.where(ids < n_used, jnp.clip(sizes[tb] - k * tm, 0, tm), 0).astype(jnp.int32)
    c0 = jnp.where(ids < n_used, start_sorted[tb] + k * tm, 0).astype(jnp.int32)
    grp = tb // N_PAIRS
    pair = tb % N_PAIRS
    tile_a = (grp * EXPERTS_PER_GROUP + jnp.asarray(PAIR_LO, jnp.int32)[pair]).astype(jnp.int32)
    tile_b = (grp * EXPERTS_PER_GROUP + jnp.asarray(PAIR_HI, jnp.int32)[pair]).astype(jnp.int32)

    any_space = pl.BlockSpec(memory_space=pl.ANY)
    grid_spec = pltpu.PrefetchScalarGridSpec(
        num_scalar_prefetch=5,
        grid=(n_tiles,),
        in_specs=[any_space,
                  pl.BlockSpec((1, 1, d), lambda i, o, c, v, ea, eb: (layer, 0, 0)),
                  pl.BlockSpec((1, 1, d, f2), lambda i, o, c, v, ea, eb: (layer, ea[i], 0, 0)),
                  pl.BlockSpec((1, 1, d, f2), lambda i, o, c, v, ea, eb: (layer, eb[i], 0, 0)),
                  pl.BlockSpec((1, 1, f, d), lambda i, o, c, v, ea, eb: (layer, ea[i], 0, 0)),
                  pl.BlockSpec((1, 1, f, d), lambda i, o, c, v, ea, eb: (layer, eb[i], 0, 0)),
                  pl.BlockSpec((1, d), lambda i, o, c, v, ea, eb: (0, 0))],
        out_specs=any_space,
        scratch_shapes=[pltpu.VMEM((2, tm, da), F32), pltpu.VMEM((2, tm, d), F32),
                        pltpu.SemaphoreType.DMA((2,)), pltpu.SemaphoreType.DMA((2,))],
    )
    return pl.pallas_call(
        functools.partial(_expert_kernel, final_norm=final_norm),
        grid_spec=grid_spec,
        out_shape=jax.ShapeDtypeStruct((t, d), F32),
        compiler_params=_params(1),
        name="moe_experts",
    )(order, c0, nv, tile_a, tile_b, haug, g_all, w_gate_up, w_gate_up, w_down, w_down, g_final.reshape(1, d))


def kernel(x, mem, g_mix, g_xattn, g_mem, g_moe, g_final, w_a_in, w_a_conv, b_a_conv, g_a_ln, b_a_ln, w_a_out, w_b_out, w_c_in, w_c_conv, w_c_out, w_xq, w_xk, w_xv, w_xo, w_route_group, b_route_group, w_route_expert, b_route_expert, w_gate_up, w_down):
    bsz, s, d = x.shape
    t = bsz * s
    depth = g_mix.shape[0]
    mem2d = mem.reshape(bsz * mem.shape[1], d)
    bf = lambda w: w.astype(BF16)
    g_mix, g_xattn, g_mem, g_moe = _rows3(g_mix), _rows3(g_xattn), _rows3(g_mem), _rows3(g_moe)
    b_a_conv, g_a_ln, b_a_ln = _rows3(b_a_conv), _rows3(g_a_ln), _rows3(b_a_ln)
    w_a_in, w_a_out, w_b_out, w_c_in, w_c_out = bf(w_a_in), bf(w_a_out), bf(w_b_out), bf(w_c_in), bf(w_c_out)
    w_xq, w_xk, w_xv, w_xo = bf(w_xq), bf(w_xk), bf(w_xv), bf(w_xo)
    w_gate_up, w_down = bf(w_gate_up), bf(w_down)

    h = x
    for i in range(depth):
        kind, j = i % N_MIXERS, i // N_MIXERS
        if kind == 0:
            u = _a_in(h.reshape(t, d), g_mix, i, w_a_in, j)
            h = _a_conv(u.reshape(bsz, s, d), h, w_a_conv, b_a_conv, g_a_ln, b_a_ln, w_a_out, j)
        elif kind == 1:
            h = _fourier_mixer(h, g_mix, i, w_b_out, j)
        else:
            gate, cv = _c_in(h.reshape(t, d), g_mix, i, w_c_in, j)
            h = _c_conv(cv.reshape(bsz, s, d), gate.reshape(bsz, s, d), h, w_c_conv, w_c_out, j)
        k, v = _memory_kv(mem2d, g_mem, w_xk, w_xv, i)
        haug, counts = _cross_attention_route(
            h, g_xattn, w_xq, k.reshape(bsz, -1, d), v.reshape(bsz, -1, d), w_xo, g_moe,
            w_route_group[i], b_route_group[i], w_route_expert[i], b_route_expert[i], i)
        h = _moe_experts(haug, counts, g_moe, w_gate_up, w_down, i, g_final, i == depth - 1).reshape(bsz, s, d)
    return h
```

```python
import functools
import math

import numpy as np
import jax
import jax.numpy as jnp
from jax import lax
from jax.experimental import pallas as pl
from jax.experimental.pallas import tpu as pltpu

F32 = jnp.float32
BF16 = jnp.bfloat16
EPS = 1e-6

N_MIXERS = 3
N_FOURIER_GROUPS = 8
N_HEADS = 4
N_GROUPS = 8
EXPERTS_PER_GROUP = 4
N_PAIRS = 6
N_BUCKETS = N_GROUPS * N_PAIRS
PAIR_LO = (0, 0, 0, 1, 1, 2)
PAIR_HI = (1, 2, 3, 2, 3, 3)

LANES = 128
SUBLANES = 8
SEQ_MINOR = 128
HALO = 16
VMEM_LIMIT_BYTES = 56 * 1024 * 1024

TM_DENSE = 512
TM_MOE = 256
ROW_CHUNK = 64
N_CHUNK = 512
CONV_COLS = 256


def _params(n_axes):
    return pltpu.CompilerParams(
        dimension_semantics=("arbitrary",) * n_axes,
        vmem_limit_bytes=VMEM_LIMIT_BYTES)


def _layer_block(shape, layer):
    zeros = (0,) * (len(shape) - 1)
    return pl.BlockSpec((1,) + tuple(shape[1:]), lambda *_: (layer,) + zeros, pipeline_mode=pl.Buffered(1))


def _rows3(a):
    return a.reshape(a.shape[0], 1, a.shape[1])


def _rms(x, g):
    return x * lax.rsqrt(jnp.mean(x * x, axis=-1, keepdims=True) + EPS) * g


def _sigmoid(x):
    return 1.0 / (1.0 + jnp.exp(-x))


def _dot(a, b):
    return jnp.dot(a, b, preferred_element_type=F32)


def _a_in_kernel(x_ref, g_ref, w_ref, u_ref, xn_ref):
    d = x_ref.shape[1]
    xn_ref[...] = _rms(x_ref[...], g_ref[0]).astype(BF16)
    for c in range(0, d, N_CHUNK):
        a = _dot(xn_ref[...], w_ref[0, :, c:c + N_CHUNK])
        gate = _dot(xn_ref[...], w_ref[0, :, d + c:d + c + N_CHUNK])
        u_ref[:, c:c + N_CHUNK] = (a * _sigmoid(gate)).astype(BF16)


def _a_in(h, g_all, layer, w_all, j):
    t, d = h.shape
    return pl.pallas_call(
        _a_in_kernel,
        grid=(t // TM_DENSE,),
        in_specs=[pl.BlockSpec((TM_DENSE, d), lambda i: (i, 0)),
                  _layer_block(g_all.shape, layer),
                  _layer_block(w_all.shape, j)],
        out_specs=pl.BlockSpec((TM_DENSE, d), lambda i: (i, 0)),
        out_shape=jax.ShapeDtypeStruct((t, d), BF16),
        scratch_shapes=[pltpu.VMEM((TM_DENSE, d), BF16)],
        compiler_params=_params(1),
        name="conformer_in",
    )(h, g_all, w_all)


def _c_in_kernel(x_ref, g_ref, w_ref, b_ref, cv_ref, xn_ref):
    d = x_ref.shape[1]
    xn_ref[...] = _rms(x_ref[...], g_ref[0]).astype(BF16)
    for c in range(0, d, N_CHUNK):
        b_ref[:, c:c + N_CHUNK] = _dot(xn_ref[...], w_ref[0, :, c:c + N_CHUNK]).astype(BF16)
        cg = _dot(xn_ref[...], w_ref[0, :, d + c:d + c + N_CHUNK])
        v = _dot(xn_ref[...], w_ref[0, :, 2 * d + c:2 * d + c + N_CHUNK])
        cv_ref[:, c:c + N_CHUNK] = (cg * v).astype(BF16)


def _c_in(h, g_all, layer, w_all, j):
    t, d = h.shape
    tile = pl.BlockSpec((TM_DENSE, d), lambda i: (i, 0))
    return pl.pallas_call(
        _c_in_kernel,
        grid=(t // TM_DENSE,),
        in_specs=[tile, _layer_block(g_all.shape, layer), _layer_block(w_all.shape, j)],
        out_specs=[tile, tile],
        out_shape=[jax.ShapeDtypeStruct((t, d), BF16)] * 2,
        scratch_shapes=[pltpu.VMEM((TM_DENSE, d), BF16)],
        compiler_params=_params(1),
        name="shortconv_in",
    )(h, g_all, w_all)


def _fill_ext(ext_ref, cur_ref, prev_ref, next_ref, tm):
    i = pl.program_id(1)
    n = pl.num_programs(1)
    prev = prev_ref[0].astype(F32)
    nxt = next_ref[0].astype(F32)
    ext_ref[0:HALO, :] = jnp.where(i > 0, prev, jnp.zeros_like(prev))
    ext_ref[HALO + tm:HALO + tm + HALO, :] = jnp.where(i < n - 1, nxt, jnp.zeros_like(nxt))

    def body(j, carry):
        r0 = pl.multiple_of(j * ROW_CHUNK, ROW_CHUNK)
        ext_ref[pl.ds(HALO + r0, ROW_CHUNK), :] = cur_ref[0, pl.ds(r0, ROW_CHUNK), :].astype(F32)
        return carry

    lax.fori_loop(0, tm // ROW_CHUNK, body, 0)


def _depthwise(ext_ref, w_ref, cv_ref, width, tm):
    d = cv_ref.shape[1]
    rows = ROW_CHUNK
    span = rows + 2 * HALO
    off = HALO - width // 2

    def body(j, carry):
        r0 = pl.multiple_of(j * rows, rows)
        for c in range(0, d, CONV_COLS):
            blk = ext_ref[pl.ds(r0, span), c:c + CONV_COLS]
            acc = jnp.zeros((rows, CONV_COLS), F32)
            for r in range(SUBLANES):
                taps = [k for k in range(width) if (off + k) % SUBLANES == r]
                if not taps:
                    continue
                shifted = blk if r == 0 else pltpu.roll(blk, span - r, axis=0)
                for k in taps:
                    q = (off + k) // SUBLANES * SUBLANES
                    acc = acc + shifted[q:q + rows, :] * w_ref[0, k:k + 1, c:c + CONV_COLS]
            cv_ref[pl.ds(r0, rows), c:c + CONV_COLS] = acc
        return carry

    lax.fori_loop(0, tm // rows, body, 0)


def _project_residual(out_ref, h_ref, v_ref, w_ref):
    d = v_ref.shape[1]
    for c in range(0, d, N_CHUNK):
        out_ref[0, :, c:c + N_CHUNK] = h_ref[0, :, c:c + N_CHUNK] + _dot(v_ref[...], w_ref[0, :, c:c + N_CHUNK])


def _a_conv_kernel(cur_ref, prev_ref, next_ref, h_ref, wc_ref, bc_ref, gl_ref, bl_ref, wo_ref,
                   out_ref, ext_ref, cv_ref, v_ref, *, width):
    tm = cur_ref.shape[1]
    _fill_ext(ext_ref, cur_ref, prev_ref, next_ref, tm)
    _depthwise(ext_ref, wc_ref, cv_ref, width, tm)

    def body(j, carry):
        r0 = pl.multiple_of(j * ROW_CHUNK, ROW_CHUNK)
        u = cv_ref[pl.ds(r0, ROW_CHUNK), :] + bc_ref[0]
        mu = jnp.mean(u, axis=-1, keepdims=True)
        uc = u - mu
        var = jnp.mean(uc * uc, axis=-1, keepdims=True)
        y = uc * lax.rsqrt(var + EPS) * gl_ref[0] + bl_ref[0]
        v_ref[pl.ds(r0, ROW_CHUNK), :] = (y * _sigmoid(y)).astype(BF16)
        return carry

    lax.fori_loop(0, tm // ROW_CHUNK, body, 0)
    _project_residual(out_ref, h_ref, v_ref, wo_ref)


def _c_conv_kernel(cur_ref, prev_ref, next_ref, gate_ref, h_ref, wc_ref, wo_ref,
                   out_ref, ext_ref, cv_ref, v_ref, *, width):
    tm = cur_ref.shape[1]
    _fill_ext(ext_ref, cur_ref, prev_ref, next_ref, tm)
    _depthwise(ext_ref, wc_ref, cv_ref, width, tm)

    def body(j, carry):
        r0 = pl.multiple_of(j * ROW_CHUNK, ROW_CHUNK)
        gate = gate_ref[0, pl.ds(r0, ROW_CHUNK), :].astype(F32)
        v_ref[pl.ds(r0, ROW_CHUNK), :] = (gate * cv_ref[pl.ds(r0, ROW_CHUNK), :]).astype(BF16)
        return carry

    lax.fori_loop(0, tm // ROW_CHUNK, body, 0)
    _project_residual(out_ref, h_ref, v_ref, wo_ref)


def _conv_specs(s, d, tm):
    per = tm // HALO
    last = s // HALO - 1
    cur = pl.BlockSpec((1, tm, d), lambda b, i: (b, i, 0))
    prev = pl.BlockSpec((1, HALO, d), lambda b, i: (b, jnp.maximum(i * per - 1, 0), 0))
    nxt = pl.BlockSpec((1, HALO, d), lambda b, i: (b, jnp.minimum((i + 1) * per, last), 0))
    return cur, prev, nxt


def _conv_scratch(tm, d):
    return [pltpu.VMEM((tm + 2 * HALO, d), F32), pltpu.VMEM((tm, d), F32), pltpu.VMEM((tm, d), BF16)]


def _a_conv(u, h, w_conv, b_conv, g_ln, b_ln, w_out, j):
    bsz, s, d = h.shape
    tm = TM_DENSE
    cur, prev, nxt = _conv_specs(s, d, tm)
    return pl.pallas_call(
        functools.partial(_a_conv_kernel, width=w_conv.shape[1]),
        grid=(bsz, s // tm),
        in_specs=[cur, prev, nxt, cur, _layer_block(w_conv.shape, j), _layer_block(b_conv.shape, j),
                  _layer_block(g_ln.shape, j), _layer_block(b_ln.shape, j), _layer_block(w_out.shape, j)],
        out_specs=cur,
        out_shape=jax.ShapeDtypeStruct((bsz, s, d), F32),
        scratch_shapes=_conv_scratch(tm, d),
        compiler_params=_params(2),
        name="conformer_conv_out",
    )(u, u, u, h, w_conv, b_conv, g_ln, b_ln, w_out)


def _c_conv(cv, gate, h, w_conv, w_out, j):
    bsz, s, d = h.shape
    tm = TM_DENSE
    cur, prev, nxt = _conv_specs(s, d, tm)
    return pl.pallas_call(
        functools.partial(_c_conv_kernel, width=w_conv.shape[1]),
        grid=(bsz, s // tm),
        in_specs=[cur, prev, nxt, cur, cur, _layer_block(w_conv.shape, j), _layer_block(w_out.shape, j)],
        out_specs=cur,
        out_shape=jax.ShapeDtypeStruct((bsz, s, d), F32),
        scratch_shapes=_conv_scratch(tm, d),
        compiler_params=_params(2),
        name="shortconv_conv_out",
    )(cv, cv, cv, gate, h, w_conv, w_out)


def _dft_tables(s, group):
    s1 = s // SEQ_MINOR

    def cs(n_rows, n_cols, period):
        m = (np.outer(np.arange(n_rows), np.arange(n_cols)) % period).astype(np.float64)
        ang = 2.0 * np.pi * m / period
        return np.cos(ang), -np.sin(ang)

    c1, i1 = cs(s1, s1, s1)
    w1 = np.concatenate([c1, i1], axis=0)
    twr, twi = cs(s1, SEQ_MINOR, s)
    c2, i2 = cs(SEQ_MINOR, SEQ_MINOR, SEQ_MINOR)
    w2 = np.block([[c2, -i2], [i2, c2]])
    cc, ic = cs(group, group, group)
    scale = 1.0 / math.sqrt(float(s) * float(group))
    return (jnp.asarray(w1, BF16), jnp.asarray(twr, F32), jnp.asarray(twi, F32),
            jnp.asarray(w2, BF16), jnp.asarray(cc * scale, BF16), jnp.asarray(-ic * scale, BF16))


def _b_stage1_kernel(x_ref, g_ref, w1_ref, twr_ref, twi_ref, br_ref, bi_ref, *, nb):
    s1 = x_ref.shape[1]
    d = g_ref.shape[2]
    for n in range(nb):
        xn = _rms(x_ref[0, :, n * d:(n + 1) * d], g_ref[0]).astype(BF16)
        y = _dot(w1_ref[...], xn)
        yr, yi = y[:s1], y[s1:]
        tr = twr_ref[0, :, n:n + 1]
        ti = twi_ref[0, :, n:n + 1]
        br_ref[0, :, n * d:(n + 1) * d] = (yr * tr - yi * ti).astype(BF16)
        bi_ref[0, :, n * d:(n + 1) * d] = (yr * ti + yi * tr).astype(BF16)


def _b_stage2_kernel(br_ref, bi_ref, h_ref, w2_ref, cc_ref, sc_ref, wo_ref, out_ref,
                     ar_ref, ai_ref, f_ref, *, ka, group):
    d = br_ref.shape[2]
    m = SEQ_MINOR
    for n in range(ka):
        cat = jnp.concatenate([br_ref[0, n * m:(n + 1) * m, :], bi_ref[0, n * m:(n + 1) * m, :]], axis=0)
        a = _dot(w2_ref[...], cat)
        ar_ref[n * m:(n + 1) * m, :] = a[:m].astype(BF16)
        ai_ref[n * m:(n + 1) * m, :] = a[m:].astype(BF16)
    for c in range(0, d, group):
        f = _dot(ar_ref[:, c:c + group], cc_ref[...]) + _dot(ai_ref[:, c:c + group], sc_ref[...])
        f_ref[:, c:c + group] = f.astype(BF16)
    for c in range(0, d, N_CHUNK):
        out_ref[0, :, c:c + N_CHUNK] = h_ref[0, :, c:c + N_CHUNK] + _dot(f_ref[...], wo_ref[0, :, c:c + N_CHUNK])


def _const_block(shape):
    nd = len(shape)
    return pl.BlockSpec(shape, lambda *_: (0,) * nd, pipeline_mode=pl.Buffered(1))


def _fourier_row_orders(bsz, s):
    s1 = s // SEQ_MINOR
    k = np.arange(s)
    base = (np.arange(bsz) * s)[:, None]
    fm_of_nat = (base + ((k % s1) * SEQ_MINOR + k // s1)[None, :]).reshape(-1)
    nat_of_fm = np.argsort(fm_of_nat)
    row1 = ((np.arange(bsz) * s1)[:, None] + (k // SEQ_MINOR)[None, :]).reshape(-1)
    col1 = np.tile(k % SEQ_MINOR, bsz)
    as_i32 = lambda a: jnp.asarray(a, jnp.int32)
    return as_i32(fm_of_nat), as_i32(nat_of_fm), as_i32(row1), as_i32(col1)


def _fourier_layouts_from_natural(h):
    bsz, s, d = h.shape
    s1 = s // SEQ_MINOR
    x1 = h.reshape(bsz * s1, SEQ_MINOR * d)
    h_fm = h.reshape(bsz, SEQ_MINOR, s1, d).transpose(0, 2, 1, 3).reshape(bsz, s, d)
    return x1, h_fm


def _fourier_mixer(x1, h_fm, g_all, layer, w_out, j):
    bsz, s, d = h_fm.shape
    s1 = s // SEQ_MINOR
    group = d // N_FOURIER_GROUPS
    nb = 4
    ka = 4
    w1, twr, twi, w2, cc, sc = _dft_tables(s, group)
    steps1 = SEQ_MINOR // nb
    twr = twr.reshape(s1, steps1, nb).transpose(1, 0, 2)
    twi = twi.reshape(s1, steps1, nb).transpose(1, 0, 2)

    x1 = x1.reshape(bsz, s1, SEQ_MINOR * d)
    blk1 = pl.BlockSpec((1, s1, nb * d), lambda b, i: (b, 0, i))
    tw_spec = pl.BlockSpec((1, s1, nb), lambda b, i: (i, 0, 0))
    br, bi = pl.pallas_call(
        functools.partial(_b_stage1_kernel, nb=nb),
        grid=(bsz, steps1),
        in_specs=[blk1, _layer_block(g_all.shape, layer), _const_block((2 * s1, s1)), tw_spec, tw_spec],
        out_specs=[blk1, blk1],
        out_shape=[jax.ShapeDtypeStruct((bsz, s1, SEQ_MINOR * d), BF16)] * 2,
        compiler_params=_params(2),
        name="fourier_stage1",
    )(x1, g_all, w1, twr, twi)

    br = br.reshape(bsz, s, d)
    bi = bi.reshape(bsz, s, d)
    rows = pl.BlockSpec((1, ka * SEQ_MINOR, d), lambda b, i: (b, i, 0))
    return pl.pallas_call(
        functools.partial(_b_stage2_kernel, ka=ka, group=group),
        grid=(bsz, s1 // ka),
        in_specs=[rows, rows, rows, _const_block((2 * SEQ_MINOR, 2 * SEQ_MINOR)),
                  _const_block((group, group)), _const_block((group, group)), _layer_block(w_out.shape, j)],
        out_specs=rows,
        out_shape=jax.ShapeDtypeStruct((bsz, s, d), F32),
        scratch_shapes=[pltpu.VMEM((ka * SEQ_MINOR, d), BF16)] * 3,
        compiler_params=_params(2),
        name="fourier_stage2_out",
    )(br, bi, h_fm, w2, cc, sc, w_out)


def _kv_kernel(mem_ref, g_ref, wk_ref, wv_ref, k_ref, v_ref):
    memn = _rms(mem_ref[...], g_ref[0]).astype(BF16)
    k_ref[...] = _dot(memn, wk_ref[0]).astype(BF16)
    v_ref[...] = _dot(memn, wv_ref[0]).astype(BF16)


def _memory_kv(mem2d, g_all, wk, wv, layer):
    r, d = mem2d.shape
    col = pl.BlockSpec((1, d, N_CHUNK), lambda c: (layer, 0, c))
    out = pl.BlockSpec((r, N_CHUNK), lambda c: (0, c))
    return pl.pallas_call(
        _kv_kernel,
        grid=(d // N_CHUNK,),
        in_specs=[_const_block((r, d)), _layer_block(g_all.shape, layer), col, col],
        out_specs=[out, out],
        out_shape=[jax.ShapeDtypeStruct((r, d), BF16)] * 2,
        compiler_params=_params(1),
        name="memory_kv",
    )(mem2d, g_all, wk, wv)


def _split_bf16(a):
    hi = a.astype(BF16)
    lo = (a - hi.astype(F32)).astype(BF16)
    return hi, lo


def _route(xn, whl, bias):
    xh, xl = _split_bf16(xn)
    both = _dot(xh, whl)
    logits = both[:, :LANES] + both[:, LANES:] + _dot(xl, whl[:, :LANES]) + bias
    lane = lax.broadcasted_iota(jnp.int32, logits.shape, 1)
    neg = jnp.float32(-jnp.inf)
    big = jnp.int32(LANES)

    gl = jnp.where(lane < N_GROUPS, logits, neg)
    gmax = jnp.max(gl, axis=-1, keepdims=True)
    grp = jnp.min(jnp.where(gl == gmax, lane, big), axis=-1, keepdims=True)
    g_w = 1.0 / jnp.sum(jnp.exp(gl - gmax), axis=-1, keepdims=True)

    first = N_GROUPS + grp * EXPERTS_PER_GROUP
    el = jnp.where((lane >= first) & (lane < first + EXPERTS_PER_GROUP), logits, neg)
    m1 = jnp.max(el, axis=-1, keepdims=True)
    i1 = jnp.min(jnp.where(el == m1, lane, big), axis=-1, keepdims=True)
    el2 = jnp.where(lane == i1, neg, el)
    m2 = jnp.max(el2, axis=-1, keepdims=True)
    i2 = jnp.min(jnp.where(el2 == m2, lane, big), axis=-1, keepdims=True)
    e21 = jnp.exp(m2 - m1)
    w1 = g_w / (1.0 + e21)
    w2 = g_w * e21 / (1.0 + e21)

    a1 = i1 - first
    a2 = i2 - first
    lo = jnp.minimum(a1, a2)
    hi = jnp.maximum(a1, a2)
    pair = lo * 3 - jnp.right_shift(lo * (lo - 1), 1) + hi - lo - 1
    bucket = grp * N_PAIRS + pair
    w_lo = jnp.where(a1 < a2, w1, w2)
    w_hi = jnp.where(a1 < a2, w2, w1)
    routed = jnp.where(lane == 0, bucket.astype(F32),
                       jnp.where(lane == 1, w_lo, jnp.where(lane == 2, w_hi, 0.0)))
    return routed, (lane == bucket).astype(F32)


def _xattn_kernel(h_ref, g_ref, wq_ref, k_ref, v_ref, wo_ref, gm_ref, rw_ref, rb_ref,
                  out_ref, cnt_ref, xn_ref, o_ref):
    d = h_ref.shape[2]
    hd = d // N_HEADS
    scale = float(hd) ** -0.5
    xn_ref[...] = _rms(h_ref[0], g_ref[0]).astype(BF16)
    for c in range(0, d, hd):
        q = _dot(xn_ref[...], wq_ref[0, :, c:c + hd]).astype(BF16)
        s = lax.dot_general(q, k_ref[0, :, c:c + hd], (((1,), (1,)), ((), ())),
                            preferred_element_type=F32) * scale
        e = jnp.exp(s - jnp.max(s, axis=-1, keepdims=True))
        p = (e / jnp.sum(e, axis=-1, keepdims=True)).astype(BF16)
        o_ref[:, c:c + hd] = _dot(p, v_ref[0, :, c:c + hd]).astype(BF16)
    for c in range(0, d, N_CHUNK):
        out_ref[:, c:c + N_CHUNK] = h_ref[0, :, c:c + N_CHUNK] + _dot(o_ref[...], wo_ref[0, :, c:c + N_CHUNK])

    routed, onehot = _route(_rms(out_ref[:, 0:d], gm_ref[0]), rw_ref[...], rb_ref[...])
    out_ref[:, d:d + LANES] = routed

    @pl.when((pl.program_id(0) == 0) & (pl.program_id(1) == 0))
    def _():
        cnt_ref[...] = jnp.zeros_like(cnt_ref)

    cnt_ref[...] += jnp.sum(onehot, axis=0, keepdims=True)


def _cross_attention_route(h, g_all, wq, k, v, wo, g_moe_all, w_rg, b_rg, w_re, b_re, layer):
    bsz, s, d = h.shape
    n_mem = k.shape[1]
    tm = TM_DENSE
    per = s // tm
    n = w_rg.shape[1] + w_re.shape[1]
    w = jnp.zeros((d, LANES), F32).at[:, :n].set(jnp.concatenate([w_rg, w_re], axis=1))
    b = jnp.zeros((1, LANES), F32).at[0, :n].set(jnp.concatenate([b_rg, b_re]))
    whl = jnp.concatenate(_split_bf16(w), axis=1)
    tile = pl.BlockSpec((1, tm, d), lambda bb, i: (bb, i, 0))
    kv = pl.BlockSpec((1, n_mem, d), lambda bb, i: (bb, 0, 0))
    return pl.pallas_call(
        _xattn_kernel,
        grid=(bsz, per),
        in_specs=[tile, _layer_block(g_all.shape, layer), _layer_block(wq.shape, layer), kv, kv,
                  _layer_block(wo.shape, layer), _layer_block(g_moe_all.shape, layer),
                  _const_block((d, 2 * LANES)), _const_block((1, LANES))],
        out_specs=[pl.BlockSpec((tm, d + LANES), lambda bb, i: (bb * per + i, 0)),
                   pl.BlockSpec((1, LANES), lambda bb, i: (0, 0))],
        out_shape=[jax.ShapeDtypeStruct((bsz * s, d + LANES), F32),
                   jax.ShapeDtypeStruct((1, LANES), F32)],
        scratch_shapes=[pltpu.VMEM((tm, d), BF16)] * 2,
        compiler_params=_params(2),
        name="cross_attention_route",
    )(h, g_all, wq, k, v, wo, g_moe_all, whl, b)


def _wait_rows(count, block_copy, row_copy):
    n8 = pl.multiple_of(jnp.right_shift(count, 3) * SUBLANES, SUBLANES)

    @pl.when(n8 > 0)
    def _():
        block_copy(n8).wait()

    def one(r, carry):
        row_copy().wait()
        return carry

    lax.fori_loop(0, count - n8, one, 0)


def _expert_kernel(order_ref, dest_ref, row2_ref, col2_ref, c0_ref, nv_ref, ea_ref, eb_ref,
                   haug_ref, g_ref, gua_ref, gub_ref, da_ref, db_ref, gf_ref,
                   out_ref, out2_ref, xbuf, obuf, gsem, ssem, *, final_norm, dual):
    i = pl.program_id(0)
    n = pl.num_programs(0)
    parity = lax.rem(i, 2)
    tm = xbuf.shape[1]
    d = obuf.shape[2]
    f = da_ref.shape[2]
    copies_per_row = 2 if dual else 1

    nv = nv_ref[i]
    nxt = jnp.minimum(i + 1, n - 1)
    prev = jnp.maximum(i - 1, 0)
    nv_next = jnp.where(i + 1 < n, nv_ref[nxt], 0)
    nv_prev = jnp.where(i >= 1, nv_ref[prev], 0)
    nv_prev2 = jnp.where(i >= 2, nv_ref[jnp.maximum(i - 2, 0)], 0)

    def gather_row(tok, s, r):
        return pltpu.make_async_copy(haug_ref.at[pl.ds(tok, 1), :], xbuf.at[s, pl.ds(r, 1), :], gsem.at[s])

    def scatter_row(row, s, r):
        return pltpu.make_async_copy(obuf.at[s, pl.ds(r, 1), :], out_ref.at[pl.ds(row, 1), :], ssem.at[s])

    def scatter_row2(row, col, s, r):
        lanes = pl.ds(pl.multiple_of(col * d, d), d)
        return pltpu.make_async_copy(obuf.at[s, pl.ds(r, 1), :], out2_ref.at[pl.ds(row, 1), lanes], ssem.at[s])

    def start_scatter_row(p, s, r):
        scatter_row(dest_ref[p], s, r).start()
        if dual:
            scatter_row2(row2_ref[p], col2_ref[p], s, r).start()

    def start_gather(count, base, s):
        @pl.when(count > 0)
        def _():
            for r in range(tm):
                gather_row(order_ref[base + r], s, r).start()

    def start_scatter(count, base, s):
        @pl.when(count == tm)
        def _():
            for r in range(tm):
                start_scatter_row(base + r, s, r)

        @pl.when((count > 0) & (count < tm))
        def _():
            for r in range(tm):
                @pl.when(r < count)
                def _():
                    start_scatter_row(base + r, s, r)

    def wait_gather(count, s):
        @pl.when(count > 0)
        def _():
            pltpu.make_async_copy(haug_ref.at[pl.ds(0, tm), :], xbuf.at[s], gsem.at[s]).wait()

    def wait_scatter(count, s):
        for _ in range(copies_per_row):
            _wait_rows(count,
                       lambda m: pltpu.make_async_copy(obuf.at[s, pl.ds(0, m), :], out_ref.at[pl.ds(0, m), :],
                                                       ssem.at[s]),
                       lambda: scatter_row(0, s, 0))

    @pl.when(i == 0)
    def _():
        start_gather(nv, c0_ref[0], 0)
        if not dual:
            out2_ref[...] = jnp.zeros_like(out2_ref)

    for slot in range(2):
        @pl.when(parity == slot)
        def _():
            start_gather(nv_next, c0_ref[nxt], 1 - slot)
            start_scatter(nv_prev, c0_ref[prev], 1 - slot)
            wait_gather(nv, slot)

    def compute_tile(rows):
        x = xbuf[parity, 0:rows, 0:d]
        xn = _rms(x, g_ref[0]).astype(BF16)
        acc = x
        for lane, (gu_ref, dn_ref) in enumerate(((gua_ref, da_ref), (gub_ref, db_ref))):
            gu = _dot(xn, gu_ref[0, 0])
            gate, up = gu[:, :f], gu[:, f:]
            act = (gate * _sigmoid(gate) * up).astype(BF16)
            acc = acc + _dot(act, dn_ref[0, 0]) * xbuf[parity, 0:rows, d + 1 + lane:d + 2 + lane]
        if final_norm:
            acc = _rms(acc, gf_ref[...])
        wait_scatter(nv_prev2, parity)
        obuf[parity, 0:rows] = acc

    half = tm // 2

    @pl.when(nv > half)
    def _():
        compute_tile(tm)

    @pl.when((nv > 0) & (nv <= half))
    def _():
        compute_tile(half)

    @pl.when(nv == 0)
    def _():
        wait_scatter(nv_prev2, parity)

    @pl.when(i == n - 1)
    def _():
        wait_scatter(nv_prev, 1 - parity)


def _moe_experts(haug, counts, g_all, w_gate_up, w_down, layer, g_final, final_norm, row_of=None, layout2=None):
    t, da = haug.shape
    d = da - LANES
    f2 = w_gate_up.shape[3]
    f = w_down.shape[2]
    tm = TM_MOE
    n_tiles = t // tm + N_BUCKETS
    dual = layout2 is not None

    bucket = haug[:, d].astype(jnp.int32)
    order = jnp.argsort(bucket, stable=True).astype(jnp.int32)
    pad = lambda a: jnp.concatenate([a, jnp.zeros((tm,), jnp.int32)])
    dest = order if row_of is None else row_of[order]
    row2 = layout2[1][order] if dual else order
    col2 = layout2[2][order] if dual else order
    sizes = counts[0, :N_BUCKETS].astype(jnp.int32)
    tiles_per = (sizes + tm - 1) // tm
    tile_end = jnp.cumsum(tiles_per)
    tile_start = tile_end - tiles_per
    n_used = tile_end[-1]
    start_sorted = jnp.cumsum(sizes) - sizes
    ids = jnp.arange(n_tiles, dtype=jnp.int32)
    tb = jnp.sum((tile_end[None, :] <= jnp.minimum(ids, n_used - 1)[:, None]).astype(jnp.int32), axis=1)
    k = ids - tile_start[tb]
    nv = jnp.where(ids < n_used, jnp.clip(sizes[tb] - k * tm, 0, tm), 0).astype(jnp.int32)
    c0 = jnp.where(ids < n_used, start_sorted[tb] + k * tm, 0).astype(jnp.int32)
    grp = tb // N_PAIRS
    pair = tb % N_PAIRS
    tile_a = (grp * EXPERTS_PER_GROUP + jnp.asarray(PAIR_LO, jnp.int32)[pair]).astype(jnp.int32)
    tile_b = (grp * EXPERTS_PER_GROUP + jnp.asarray(PAIR_HI, jnp.int32)[pair]).astype(jnp.int32)

    any_space = pl.BlockSpec(memory_space=pl.ANY)
    expert_a = lambda i, o, ds, r2, c2, c, v, ea, eb: (layer, ea[i], 0, 0)
    expert_b = lambda i, o, ds, r2, c2, c, v, ea, eb: (layer, eb[i], 0, 0)
    grid_spec = pltpu.PrefetchScalarGridSpec(
        num_scalar_prefetch=8,
        grid=(n_tiles,),
        in_specs=[any_space,
                  pl.BlockSpec((1, 1, d), lambda i, *_: (layer, 0, 0)),
                  pl.BlockSpec((1, 1, d, f2), expert_a),
                  pl.BlockSpec((1, 1, d, f2), expert_b),
                  pl.BlockSpec((1, 1, f, d), expert_a),
                  pl.BlockSpec((1, 1, f, d), expert_b),
                  pl.BlockSpec((1, d), lambda i, *_: (0, 0))],
        out_specs=[any_space, any_space if dual else pl.BlockSpec((SUBLANES, LANES), lambda i, *_: (0, 0))],
        scratch_shapes=[pltpu.VMEM((2, tm, da), F32), pltpu.VMEM((2, tm, d), F32),
                        pltpu.SemaphoreType.DMA((2,)), pltpu.SemaphoreType.DMA((2,))],
    )
    shape2 = (layout2[0], SEQ_MINOR * d) if dual else (SUBLANES, LANES)
    out, out2 = pl.pallas_call(
        functools.partial(_expert_kernel, final_norm=final_norm, dual=dual),
        grid_spec=grid_spec,
        out_shape=[jax.ShapeDtypeStruct((t, d), F32), jax.ShapeDtypeStruct(shape2, F32)],
        compiler_params=_params(1),
        name="moe_experts",
    )(pad(order), pad(dest), pad(row2), pad(col2), c0, nv, tile_a, tile_b,
      haug, g_all, w_gate_up, w_gate_up, w_down, w_down, g_final.reshape(1, d))
    return (out, out2) if dual else out


def kernel(x, mem, g_mix, g_xattn, g_mem, g_moe, g_final, w_a_in, w_a_conv, b_a_conv, g_a_ln, b_a_ln, w_a_out, w_b_out, w_c_in, w_c_conv, w_c_out, w_xq, w_xk, w_xv, w_xo, w_route_group, b_route_group, w_route_expert, b_route_expert, w_gate_up, w_down):
    bsz, s, d = x.shape
    t = bsz * s
    depth = g_mix.shape[0]
    mem2d = mem.reshape(bsz * mem.shape[1], d)
    bf = lambda w: w.astype(BF16)
    g_mix, g_xattn, g_mem, g_moe = _rows3(g_mix), _rows3(g_xattn), _rows3(g_mem), _rows3(g_moe)
    b_a_conv, g_a_ln, b_a_ln = _rows3(b_a_conv), _rows3(g_a_ln), _rows3(b_a_ln)
    w_a_in, w_a_out, w_b_out, w_c_in, w_c_out = bf(w_a_in), bf(w_a_out), bf(w_b_out), bf(w_c_in), bf(w_c_out)
    w_xq, w_xk, w_xv, w_xo = bf(w_xq), bf(w_xk), bf(w_xv), bf(w_xo)
    w_gate_up, w_down = bf(w_gate_up), bf(w_down)

    fm_of_nat, nat_of_fm, row1, col1 = _fourier_row_orders(bsz, s)
    h = x
    fourier_in = None
    for i in range(depth):
        kind, j = i % N_MIXERS, i // N_MIXERS
        next_is_fourier = i + 1 < depth and (i + 1) % N_MIXERS == 1
        if kind == 0:
            u = _a_in(h.reshape(t, d), g_mix, i, w_a_in, j)
            h = _a_conv(u.reshape(bsz, s, d), h, w_a_conv, b_a_conv, g_a_ln, b_a_ln, w_a_out, j)
        elif kind == 1:
            x1, h_fm = fourier_in if fourier_in is not None else _fourier_layouts_from_natural(h)
            h = _fourier_mixer(x1, h_fm, g_mix, i, w_b_out, j)
        else:
            gate, cv = _c_in(h.reshape(t, d), g_mix, i, w_c_in, j)
            h = _c_conv(cv.reshape(bsz, s, d), gate.reshape(bsz, s, d), h, w_c_conv, w_c_out, j)
        k, v = _memory_kv(mem2d, g_mem, w_xk, w_xv, i)
        haug, counts = _cross_attention_route(
            h, g_xattn, w_xq, k.reshape(bsz, -1, d), v.reshape(bsz, -1, d), w_xo, g_moe,
            w_route_group[i], b_route_group[i], w_route_expert[i], b_route_expert[i], i)
        rows_are_fm = kind == 1
        final = i == depth - 1
        if next_is_fourier:
            row_of = fm_of_nat[nat_of_fm] if rows_are_fm else fm_of_nat
            layout2 = (bsz * (s // SEQ_MINOR),
                       row1[nat_of_fm] if rows_are_fm else row1, col1[nat_of_fm] if rows_are_fm else col1)
            h_fm, x1 = _moe_experts(haug, counts, g_moe, w_gate_up, w_down, i, g_final, final, row_of, layout2)
            fourier_in = (x1, h_fm.reshape(bsz, s, d))
            h = None
        else:
            row_of = nat_of_fm if rows_are_fm else None
            h = _moe_experts(haug, counts, g_moe, w_gate_up, w_down, i, g_final, final, row_of).reshape(bsz, s, d)
            fourier_in = None
    return h
```

```python
import functools
import math
from typing import NamedTuple

import numpy as np
import jax
import jax.numpy as jnp
from jax import lax
from jax.experimental import pallas as pl
from jax.experimental.pallas import tpu as pltpu

F32 = jnp.float32
BF16 = jnp.bfloat16
EPS = 1e-6

N_MIXERS = 3
N_FOURIER_GROUPS = 8
N_HEADS = 4
N_GROUPS = 8
EXPERTS_PER_GROUP = 4
N_PAIRS = 6
N_BUCKETS = N_GROUPS * N_PAIRS
PAIR_LO = (0, 0, 0, 1, 1, 2)
PAIR_HI = (1, 2, 3, 2, 3, 3)

LANES = 128
SUBLANES = 8
SEQ_MINOR = 128
HALO = 16
VMEM_LIMIT_BYTES = 56 * 1024 * 1024

TM_DENSE = 512
TM_MOE = 256
ROW_CHUNK = 64
N_CHUNK = 512
CONV_COLS = 256


def _params(n_axes):
    return pltpu.CompilerParams(
        dimension_semantics=("arbitrary",) * n_axes,
        vmem_limit_bytes=VMEM_LIMIT_BYTES)


def _layer_block(shape, layer):
    zeros = (0,) * (len(shape) - 1)
    return pl.BlockSpec((1,) + tuple(shape[1:]), lambda *_: (layer,) + zeros, pipeline_mode=pl.Buffered(1))


def _rows3(a):
    return a.reshape(a.shape[0], 1, a.shape[1])


def _rms(x, g):
    return x * lax.rsqrt(jnp.mean(x * x, axis=-1, keepdims=True) + EPS) * g


def _sigmoid(x):
    return 1.0 / (1.0 + jnp.exp(-x))


def _dot(a, b):
    return jnp.dot(a, b, preferred_element_type=F32)


class _CastJob(NamedTuple):
    src: jax.Array
    layer: int
    rows: int
    cols: int
    col_block: int


def _cast_io(job, n_steps, step_of):
    n_experts = job.src.shape[1]
    per_step = -(-n_experts // n_steps)
    assert n_experts % per_step == 0
    last = n_experts // per_step - 1
    block = lambda *g: jnp.minimum(step_of(*g), last)
    in_spec = pl.BlockSpec((1, per_step, job.rows, job.cols), lambda *g: (job.layer, block(*g), 0, job.col_block))
    out_spec = pl.BlockSpec((per_step, job.rows, job.cols), lambda *g: (block(*g), 0, 0))
    return in_spec, out_spec, jax.ShapeDtypeStruct((n_experts, job.rows, job.cols), BF16)


def _cast_step(src_ref, dst_ref):
    dst_ref[...] = src_ref[0].astype(BF16)


def _a_in_kernel(x_ref, g_ref, w_ref, cast_src_ref, u_ref, cast_dst_ref, xn_ref):
    d = x_ref.shape[1]
    xn_ref[...] = _rms(x_ref[...], g_ref[0]).astype(BF16)
    for c in range(0, d, N_CHUNK):
        a = _dot(xn_ref[...], w_ref[0, :, c:c + N_CHUNK])
        gate = _dot(xn_ref[...], w_ref[0, :, d + c:d + c + N_CHUNK])
        u_ref[:, c:c + N_CHUNK] = (a * _sigmoid(gate)).astype(BF16)
    _cast_step(cast_src_ref, cast_dst_ref)


def _a_in(h, g_all, layer, w_all, j, cast):
    t, d = h.shape
    steps = t // TM_DENSE
    cast_in, cast_out, cast_shape = _cast_io(cast, steps, lambda i: i)
    return pl.pallas_call(
        _a_in_kernel,
        grid=(steps,),
        in_specs=[pl.BlockSpec((TM_DENSE, d), lambda i: (i, 0)),
                  _layer_block(g_all.shape, layer),
                  _layer_block(w_all.shape, j),
                  cast_in],
        out_specs=[pl.BlockSpec((TM_DENSE, d), lambda i: (i, 0)), cast_out],
        out_shape=[jax.ShapeDtypeStruct((t, d), BF16), cast_shape],
        scratch_shapes=[pltpu.VMEM((TM_DENSE, d), BF16)],
        compiler_params=_params(1),
        name="conformer_in",
    )(h, g_all, w_all, cast.src)


def _c_in_kernel(x_ref, g_ref, w_ref, b_ref, cv_ref, xn_ref):
    d = x_ref.shape[1]
    xn_ref[...] = _rms(x_ref[...], g_ref[0]).astype(BF16)
    for c in range(0, d, N_CHUNK):
        b_ref[:, c:c + N_CHUNK] = _dot(xn_ref[...], w_ref[0, :, c:c + N_CHUNK]).astype(BF16)
        cg = _dot(xn_ref[...], w_ref[0, :, d + c:d + c + N_CHUNK])
        v = _dot(xn_ref[...], w_ref[0, :, 2 * d + c:2 * d + c + N_CHUNK])
        cv_ref[:, c:c + N_CHUNK] = (cg * v).astype(BF16)


def _c_in(h, g_all, layer, w_all, j):
    t, d = h.shape
    tile = pl.BlockSpec((TM_DENSE, d), lambda i: (i, 0))
    return pl.pallas_call(
        _c_in_kernel,
        grid=(t // TM_DENSE,),
        in_specs=[tile, _layer_block(g_all.shape, layer), _layer_block(w_all.shape, j)],
        out_specs=[tile, tile],
        out_shape=[jax.ShapeDtypeStruct((t, d), BF16)] * 2,
        scratch_shapes=[pltpu.VMEM((TM_DENSE, d), BF16)],
        compiler_params=_params(1),
        name="shortconv_in",
    )(h, g_all, w_all)


def _fill_ext(ext_ref, cur_ref, prev_ref, next_ref, tm):
    i = pl.program_id(1)
    n = pl.num_programs(1)
    prev = prev_ref[0].astype(F32)
    nxt = next_ref[0].astype(F32)
    ext_ref[0:HALO, :] = jnp.where(i > 0, prev, jnp.zeros_like(prev))
    ext_ref[HALO + tm:HALO + tm + HALO, :] = jnp.where(i < n - 1, nxt, jnp.zeros_like(nxt))

    def body(j, carry):
        r0 = pl.multiple_of(j * ROW_CHUNK, ROW_CHUNK)
        ext_ref[pl.ds(HALO + r0, ROW_CHUNK), :] = cur_ref[0, pl.ds(r0, ROW_CHUNK), :].astype(F32)
        return carry

    lax.fori_loop(0, tm // ROW_CHUNK, body, 0)


def _depthwise(ext_ref, w_ref, cv_ref, width, tm):
    d = cv_ref.shape[1]
    rows = ROW_CHUNK
    span = rows + 2 * HALO
    off = HALO - width // 2

    def body(j, carry):
        r0 = pl.multiple_of(j * rows, rows)
        for c in range(0, d, CONV_COLS):
            blk = ext_ref[pl.ds(r0, span), c:c + CONV_COLS]
            acc = jnp.zeros((rows, CONV_COLS), F32)
            for r in range(SUBLANES):
                taps = [k for k in range(width) if (off + k) % SUBLANES == r]
                if not taps:
                    continue
                shifted = blk if r == 0 else pltpu.roll(blk, span - r, axis=0)
                for k in taps:
                    q = (off + k) // SUBLANES * SUBLANES
                    acc = acc + shifted[q:q + rows, :] * w_ref[0, k:k + 1, c:c + CONV_COLS]
            cv_ref[pl.ds(r0, rows), c:c + CONV_COLS] = acc
        return carry

    lax.fori_loop(0, tm // rows, body, 0)


def _project_residual(out_ref, h_ref, v_ref, w_ref):
    d = v_ref.shape[1]
    for c in range(0, d, N_CHUNK):
        out_ref[0, :, c:c + N_CHUNK] = h_ref[0, :, c:c + N_CHUNK] + _dot(v_ref[...], w_ref[0, :, c:c + N_CHUNK])


def _a_conv_kernel(cur_ref, prev_ref, next_ref, h_ref, wc_ref, bc_ref, gl_ref, bl_ref, wo_ref, cast_src_ref,
                   out_ref, cast_dst_ref, ext_ref, cv_ref, v_ref, *, width):
    tm = cur_ref.shape[1]
    _fill_ext(ext_ref, cur_ref, prev_ref, next_ref, tm)
    _depthwise(ext_ref, wc_ref, cv_ref, width, tm)

    def body(j, carry):
        r0 = pl.multiple_of(j * ROW_CHUNK, ROW_CHUNK)
        u = cv_ref[pl.ds(r0, ROW_CHUNK), :] + bc_ref[0]
        mu = jnp.mean(u, axis=-1, keepdims=True)
        uc = u - mu
        var = jnp.mean(uc * uc, axis=-1, keepdims=True)
        y = uc * lax.rsqrt(var + EPS) * gl_ref[0] + bl_ref[0]
        v_ref[pl.ds(r0, ROW_CHUNK), :] = (y * _sigmoid(y)).astype(BF16)
        return carry

    lax.fori_loop(0, tm // ROW_CHUNK, body, 0)
    _project_residual(out_ref, h_ref, v_ref, wo_ref)
    _cast_step(cast_src_ref, cast_dst_ref)


def _c_conv_kernel(cur_ref, prev_ref, next_ref, gate_ref, h_ref, wc_ref, wo_ref, cast_src_ref,
                   out_ref, cast_dst_ref, ext_ref, cv_ref, v_ref, *, width):
    tm = cur_ref.shape[1]
    _fill_ext(ext_ref, cur_ref, prev_ref, next_ref, tm)
    _depthwise(ext_ref, wc_ref, cv_ref, width, tm)

    def body(j, carry):
        r0 = pl.multiple_of(j * ROW_CHUNK, ROW_CHUNK)
        gate = gate_ref[0, pl.ds(r0, ROW_CHUNK), :].astype(F32)
        v_ref[pl.ds(r0, ROW_CHUNK), :] = (gate * cv_ref[pl.ds(r0, ROW_CHUNK), :]).astype(BF16)
        return carry

    lax.fori_loop(0, tm // ROW_CHUNK, body, 0)
    _project_residual(out_ref, h_ref, v_ref, wo_ref)
    _cast_step(cast_src_ref, cast_dst_ref)


def _conv_specs(s, d, tm):
    per = tm // HALO
    last = s // HALO - 1
    cur = pl.BlockSpec((1, tm, d), lambda b, i: (b, i, 0))
    prev = pl.BlockSpec((1, HALO, d), lambda b, i: (b, jnp.maximum(i * per - 1, 0), 0))
    nxt = pl.BlockSpec((1, HALO, d), lambda b, i: (b, jnp.minimum((i + 1) * per, last), 0))
    return cur, prev, nxt


def _conv_scratch(tm, d):
    return [pltpu.VMEM((tm + 2 * HALO, d), F32), pltpu.VMEM((tm, d), F32), pltpu.VMEM((tm, d), BF16)]


def _a_conv(u, h, w_conv, b_conv, g_ln, b_ln, w_out, j, cast):
    bsz, s, d = h.shape
    tm = TM_DENSE
    per = s // tm
    cur, prev, nxt = _conv_specs(s, d, tm)
    cast_in, cast_out, cast_shape = _cast_io(cast, bsz * per, lambda b, i: b * per + i)
    return pl.pallas_call(
        functools.partial(_a_conv_kernel, width=w_conv.shape[1]),
        grid=(bsz, per),
        in_specs=[cur, prev, nxt, cur, _layer_block(w_conv.shape, j), _layer_block(b_conv.shape, j),
                  _layer_block(g_ln.shape, j), _layer_block(b_ln.shape, j), _layer_block(w_out.shape, j), cast_in],
        out_specs=[cur, cast_out],
        out_shape=[jax.ShapeDtypeStruct((bsz, s, d), F32), cast_shape],
        scratch_shapes=_conv_scratch(tm, d),
        compiler_params=_params(2),
        name="conformer_conv_out",
    )(u, u, u, h, w_conv, b_conv, g_ln, b_ln, w_out, cast.src)


def _c_conv(cv, gate, h, w_conv, w_out, j, cast):
    bsz, s, d = h.shape
    tm = TM_DENSE
    per = s // tm
    cur, prev, nxt = _conv_specs(s, d, tm)
    cast_in, cast_out, cast_shape = _cast_io(cast, bsz * per, lambda b, i: b * per + i)
    return pl.pallas_call(
        functools.partial(_c_conv_kernel, width=w_conv.shape[1]),
        grid=(bsz, per),
        in_specs=[cur, prev, nxt, cur, cur, _layer_block(w_conv.shape, j), _layer_block(w_out.shape, j), cast_in],
        out_specs=[cur, cast_out],
        out_shape=[jax.ShapeDtypeStruct((bsz, s, d), F32), cast_shape],
        scratch_shapes=_conv_scratch(tm, d),
        compiler_params=_params(2),
        name="shortconv_conv_out",
    )(cv, cv, cv, gate, h, w_conv, w_out, cast.src)


def _dft_tables(s, group):
    s1 = s // SEQ_MINOR

    def cs(n_rows, n_cols, period):
        m = (np.outer(np.arange(n_rows), np.arange(n_cols)) % period).astype(np.float64)
        ang = 2.0 * np.pi * m / period
        return np.cos(ang), -np.sin(ang)

    c1, i1 = cs(s1, s1, s1)
    w1 = np.concatenate([c1, i1], axis=0)
    twr, twi = cs(s1, SEQ_MINOR, s)
    c2, i2 = cs(SEQ_MINOR, SEQ_MINOR, SEQ_MINOR)
    w2 = np.block([[c2, -i2], [i2, c2]])
    cc, ic = cs(group, group, group)
    scale = 1.0 / math.sqrt(float(s) * float(group))
    return (jnp.asarray(w1, BF16), jnp.asarray(twr, F32), jnp.asarray(twi, F32),
            jnp.asarray(w2, BF16), jnp.asarray(cc * scale, BF16), jnp.asarray(-ic * scale, BF16))


def _b_stage1_kernel(x_ref, g_ref, w1_ref, twr_ref, twi_ref, br_ref, bi_ref, *, nb):
    s1 = x_ref.shape[1]
    d = g_ref.shape[2]
    brs, bis = [], []
    for n in range(nb):
        xn = _rms(x_ref[0, :, n * d:(n + 1) * d], g_ref[0]).astype(BF16)
        y = _dot(w1_ref[...], xn)
        yr, yi = y[:s1], y[s1:]
        tr = twr_ref[0, :, n:n + 1]
        ti = twi_ref[0, :, n:n + 1]
        brs.append(yr * tr - yi * ti)
        bis.append(yr * ti + yi * tr)
    br_ref[0] = jnp.swapaxes(jnp.stack(brs, axis=0), 0, 1).astype(BF16)
    bi_ref[0] = jnp.swapaxes(jnp.stack(bis, axis=0), 0, 1).astype(BF16)


def _b_stage2_kernel(br_ref, bi_ref, h_ref, w2_ref, cc_ref, sc_ref, wo_ref, cast_src_ref, out_ref, cast_dst_ref,
                     ar_ref, ai_ref, f_ref, *, ka, group):
    d = br_ref.shape[2]
    m = SEQ_MINOR
    for n in range(ka):
        cat = jnp.concatenate([br_ref[0, n * m:(n + 1) * m, :], bi_ref[0, n * m:(n + 1) * m, :]], axis=0)
        a = _dot(w2_ref[...], cat)
        ar_ref[n * m:(n + 1) * m, :] = a[:m].astype(BF16)
        ai_ref[n * m:(n + 1) * m, :] = a[m:].astype(BF16)
    for c in range(0, d, group):
        f = _dot(ar_ref[:, c:c + group], cc_ref[...]) + _dot(ai_ref[:, c:c + group], sc_ref[...])
        f_ref[:, c:c + group] = f.astype(BF16)
    for c in range(0, d, N_CHUNK):
        out_ref[0, :, c:c + N_CHUNK] = h_ref[0, :, c:c + N_CHUNK] + _dot(f_ref[...], wo_ref[0, :, c:c + N_CHUNK])
    _cast_step(cast_src_ref, cast_dst_ref)


def _const_block(shape):
    nd = len(shape)
    return pl.BlockSpec(shape, lambda *_: (0,) * nd, pipeline_mode=pl.Buffered(1))


def _fourier_row_orders(bsz, s):
    s1 = s // SEQ_MINOR
    k = np.arange(s)
    base = (np.arange(bsz) * s)[:, None]
    fm_of_nat = (base + ((k % s1) * SEQ_MINOR + k // s1)[None, :]).reshape(-1)
    nat_of_fm = np.argsort(fm_of_nat)
    row1 = ((np.arange(bsz) * s1)[:, None] + (k // SEQ_MINOR)[None, :]).reshape(-1)
    col1 = np.tile(k % SEQ_MINOR, bsz)
    as_i32 = lambda a: jnp.asarray(a, jnp.int32)
    return as_i32(fm_of_nat), as_i32(nat_of_fm), as_i32(row1), as_i32(col1)


def _fourier_layouts_from_natural(h):
    bsz, s, d = h.shape
    s1 = s // SEQ_MINOR
    x1 = h.reshape(bsz * s1, SEQ_MINOR * d)
    h_fm = h.reshape(bsz, SEQ_MINOR, s1, d).transpose(0, 2, 1, 3).reshape(bsz, s, d)
    return x1, h_fm


def _fourier_mixer(x1, h_fm, g_all, layer, w_out, j, cast):
    bsz, s, d = h_fm.shape
    s1 = s // SEQ_MINOR
    group = d // N_FOURIER_GROUPS
    nb = 16
    ka = 4
    w1, twr, twi, w2, cc, sc = _dft_tables(s, group)
    steps1 = SEQ_MINOR // nb
    twr = twr.reshape(s1, steps1, nb).transpose(1, 0, 2)
    twi = twi.reshape(s1, steps1, nb).transpose(1, 0, 2)

    x1 = x1.reshape(bsz, s1, SEQ_MINOR * d)
    blk1 = pl.BlockSpec((1, s1, nb * d), lambda b, i: (b, 0, i))
    tw_spec = pl.BlockSpec((1, s1, nb), lambda b, i: (i, 0, 0))
    br, bi = pl.pallas_call(
        functools.partial(_b_stage1_kernel, nb=nb),
        grid=(bsz, steps1),
        in_specs=[blk1, _layer_block(g_all.shape, layer), _const_block((2 * s1, s1)), tw_spec, tw_spec],
        out_specs=[pl.BlockSpec((1, s1, nb, d), lambda b, i: (b, 0, i, 0))] * 2,
        out_shape=[jax.ShapeDtypeStruct((bsz, s1, SEQ_MINOR, d), BF16)] * 2,
        compiler_params=_params(2),
        name="fourier_stage1",
    )(x1, g_all, w1, twr, twi)

    br = br.reshape(bsz, s, d)
    bi = bi.reshape(bsz, s, d)
    rows = pl.BlockSpec((1, ka * SEQ_MINOR, d), lambda b, i: (b, i, 0))
    per = s1 // ka
    cast_in, cast_out, cast_shape = _cast_io(cast, bsz * per, lambda b, i: b * per + i)
    return pl.pallas_call(
        functools.partial(_b_stage2_kernel, ka=ka, group=group),
        grid=(bsz, per),
        in_specs=[rows, rows, rows, _const_block((2 * SEQ_MINOR, 2 * SEQ_MINOR)),
                  _const_block((group, group)), _const_block((group, group)), _layer_block(w_out.shape, j), cast_in],
        out_specs=[rows, cast_out],
        out_shape=[jax.ShapeDtypeStruct((bsz, s, d), F32), cast_shape],
        scratch_shapes=[pltpu.VMEM((ka * SEQ_MINOR, d), BF16)] * 3,
        compiler_params=_params(2),
        name="fourier_stage2_out",
    )(br, bi, h_fm, w2, cc, sc, w_out, cast.src)


def _kv_kernel(mem_ref, g_ref, wk_ref, wv_ref, k_ref, v_ref):
    memn = _rms(mem_ref[...], g_ref[0]).astype(BF16)
    k_ref[...] = _dot(memn, wk_ref[0]).astype(BF16)
    v_ref[...] = _dot(memn, wv_ref[0]).astype(BF16)


def _memory_kv(mem2d, g_all, wk, wv, layer):
    r, d = mem2d.shape
    col = pl.BlockSpec((1, d, N_CHUNK), lambda c: (layer, 0, c))
    out = pl.BlockSpec((r, N_CHUNK), lambda c: (0, c))
    return pl.pallas_call(
        _kv_kernel,
        grid=(d // N_CHUNK,),
        in_specs=[_const_block((r, d)), _layer_block(g_all.shape, layer), col, col],
        out_specs=[out, out],
        out_shape=[jax.ShapeDtypeStruct((r, d), BF16)] * 2,
        compiler_params=_params(1),
        name="memory_kv",
    )(mem2d, g_all, wk, wv)


def _split_bf16(a):
    hi = a.astype(BF16)
    lo = (a - hi.astype(F32)).astype(BF16)
    return hi, lo


def _route(xn, whl, bias):
    xh, xl = _split_bf16(xn)
    both = _dot(xh, whl)
    logits = both[:, :LANES] + both[:, LANES:] + _dot(xl, whl[:, :LANES]) + bias
    lane = lax.broadcasted_iota(jnp.int32, logits.shape, 1)
    neg = jnp.float32(-jnp.inf)
    big = jnp.int32(LANES)

    gl = jnp.where(lane < N_GROUPS, logits, neg)
    gmax = jnp.max(gl, axis=-1, keepdims=True)
    grp = jnp.min(jnp.where(gl == gmax, lane, big), axis=-1, keepdims=True)
    g_w = 1.0 / jnp.sum(jnp.exp(gl - gmax), axis=-1, keepdims=True)

    first = N_GROUPS + grp * EXPERTS_PER_GROUP
    el = jnp.where((lane >= first) & (lane < first + EXPERTS_PER_GROUP), logits, neg)
    m1 = jnp.max(el, axis=-1, keepdims=True)
    i1 = jnp.min(jnp.where(el == m1, lane, big), axis=-1, keepdims=True)
    el2 = jnp.where(lane == i1, neg, el)
    m2 = jnp.max(el2, axis=-1, keepdims=True)
    i2 = jnp.min(jnp.where(el2 == m2, lane, big), axis=-1, keepdims=True)
    e21 = jnp.exp(m2 - m1)
    w1 = g_w / (1.0 + e21)
    w2 = g_w * e21 / (1.0 + e21)

    a1 = i1 - first
    a2 = i2 - first
    lo = jnp.minimum(a1, a2)
    hi = jnp.maximum(a1, a2)
    pair = lo * 3 - jnp.right_shift(lo * (lo - 1), 1) + hi - lo - 1
    bucket = grp * N_PAIRS + pair
    w_lo = jnp.where(a1 < a2, w1, w2)
    w_hi = jnp.where(a1 < a2, w2, w1)
    routed = jnp.where(lane == 0, bucket.astype(F32),
                       jnp.where(lane == 1, w_lo, jnp.where(lane == 2, w_hi, 0.0)))
    return routed, (lane == bucket).astype(F32)


def _xattn_kernel(h_ref, g_ref, wq_ref, k_ref, v_ref, wo_ref, gm_ref, rw_ref, rb_ref, cast_src_ref,
                  out_ref, cnt_ref, cast_dst_ref, xn_ref, o_ref):
    d = h_ref.shape[2]
    hd = d // N_HEADS
    scale = float(hd) ** -0.5
    xn_ref[...] = _rms(h_ref[0], g_ref[0]).astype(BF16)
    for c in range(0, d, hd):
        q = _dot(xn_ref[...], wq_ref[0, :, c:c + hd]).astype(BF16)
        s = lax.dot_general(q, k_ref[0, :, c:c + hd], (((1,), (1,)), ((), ())),
                            preferred_element_type=F32) * scale
        e = jnp.exp(s - jnp.max(s, axis=-1, keepdims=True))
        p = (e / jnp.sum(e, axis=-1, keepdims=True)).astype(BF16)
        o_ref[:, c:c + hd] = _dot(p, v_ref[0, :, c:c + hd]).astype(BF16)
    for c in range(0, d, N_CHUNK):
        out_ref[:, c:c + N_CHUNK] = h_ref[0, :, c:c + N_CHUNK] + _dot(o_ref[...], wo_ref[0, :, c:c + N_CHUNK])

    routed, onehot = _route(_rms(out_ref[:, 0:d], gm_ref[0]), rw_ref[...], rb_ref[...])
    out_ref[:, d:d + LANES] = routed

    @pl.when((pl.program_id(0) == 0) & (pl.program_id(1) == 0))
    def _():
        cnt_ref[...] = jnp.zeros_like(cnt_ref)

    cnt_ref[...] += jnp.sum(onehot, axis=0, keepdims=True)
    _cast_step(cast_src_ref, cast_dst_ref)


def _cross_attention_route(h, g_all, wq, k, v, wo, g_moe_all, w_rg, b_rg, w_re, b_re, layer, cast):
    bsz, s, d = h.shape
    n_mem = k.shape[1]
    tm = TM_DENSE
    per = s // tm
    n = w_rg.shape[1] + w_re.shape[1]
    w = jnp.zeros((d, LANES), F32).at[:, :n].set(jnp.concatenate([w_rg, w_re], axis=1))
    b = jnp.zeros((1, LANES), F32).at[0, :n].set(jnp.concatenate([b_rg, b_re]))
    whl = jnp.concatenate(_split_bf16(w), axis=1)
    cast_in, cast_out, cast_shape = _cast_io(cast, bsz * per, lambda bb, i: bb * per + i)
    tile = pl.BlockSpec((1, tm, d), lambda bb, i: (bb, i, 0))
    kv = pl.BlockSpec((1, n_mem, d), lambda bb, i: (bb, 0, 0))
    return pl.pallas_call(
        _xattn_kernel,
        grid=(bsz, per),
        in_specs=[tile, _layer_block(g_all.shape, layer), _layer_block(wq.shape, layer), kv, kv,
                  _layer_block(wo.shape, layer), _layer_block(g_moe_all.shape, layer),
                  _const_block((d, 2 * LANES)), _const_block((1, LANES)), cast_in],
        out_specs=[pl.BlockSpec((tm, d + LANES), lambda bb, i: (bb * per + i, 0)),
                   pl.BlockSpec((1, LANES), lambda bb, i: (0, 0)), cast_out],
        out_shape=[jax.ShapeDtypeStruct((bsz * s, d + LANES), F32),
                   jax.ShapeDtypeStruct((1, LANES), F32), cast_shape],
        scratch_shapes=[pltpu.VMEM((tm, d), BF16)] * 2,
        compiler_params=_params(2),
        name="cross_attention_route",
    )(h, g_all, wq, k, v, wo, g_moe_all, whl, b, cast.src)


def _wait_rows(count, block_copy, row_copy):
    n8 = pl.multiple_of(jnp.right_shift(count, 3) * SUBLANES, SUBLANES)

    @pl.when(n8 > 0)
    def _():
        block_copy(n8).wait()

    def one(r, carry):
        row_copy().wait()
        return carry

    lax.fori_loop(0, count - n8, one, 0)


def _expert_kernel(order_ref, dest_ref, row2_ref, col2_ref, c0_ref, nv_ref, ea_ref, eb_ref,
                   haug_ref, g_ref, ga_ref, gb_ref, ua_ref, ub_ref, da_ref, db_ref, gf_ref, *rest,
                   final_norm, dual, hosts_cast):
    if hosts_cast:
        cast_src_ref, out_ref, out2_ref, cast_dst_ref, xbuf, obuf, gsem, ssem = rest
        _cast_step(cast_src_ref, cast_dst_ref)
    else:
        out_ref, out2_ref, xbuf, obuf, gsem, ssem = rest
    i = pl.program_id(0)
    n = pl.num_programs(0)
    parity = lax.rem(i, 2)
    tm = xbuf.shape[1]
    d = obuf.shape[2]
    copies_per_row = 2 if dual else 1

    nv = nv_ref[i]
    nxt = jnp.minimum(i + 1, n - 1)
    prev = jnp.maximum(i - 1, 0)
    nv_next = jnp.where(i + 1 < n, nv_ref[nxt], 0)
    nv_prev = jnp.where(i >= 1, nv_ref[prev], 0)
    nv_prev2 = jnp.where(i >= 2, nv_ref[jnp.maximum(i - 2, 0)], 0)

    def gather_row(tok, s, r):
        return pltpu.make_async_copy(haug_ref.at[pl.ds(tok, 1), :], xbuf.at[s, pl.ds(r, 1), :], gsem.at[s])

    def scatter_row(row, s, r):
        return pltpu.make_async_copy(obuf.at[s, pl.ds(r, 1), :], out_ref.at[pl.ds(row, 1), :], ssem.at[s])

    def scatter_row2(row, col, s, r):
        lanes = pl.ds(pl.multiple_of(col * d, d), d)
        return pltpu.make_async_copy(obuf.at[s, pl.ds(r, 1), :], out2_ref.at[pl.ds(row, 1), lanes], ssem.at[s])

    def start_scatter_row(p, s, r):
        scatter_row(dest_ref[p], s, r).start()
        if dual:
            scatter_row2(row2_ref[p], col2_ref[p], s, r).start()

    def start_gather(count, base, s):
        @pl.when(count > 0)
        def _():
            for r in range(tm):
                gather_row(order_ref[base + r], s, r).start()

    def start_scatter(count, base, s):
        @pl.when(count == tm)
        def _():
            for r in range(tm):
                start_scatter_row(base + r, s, r)

        @pl.when((count > 0) & (count < tm))
        def _():
            for r in range(tm):
                @pl.when(r < count)
                def _():
                    start_scatter_row(base + r, s, r)

    def wait_gather(count, s):
        @pl.when(count > 0)
        def _():
            pltpu.make_async_copy(haug_ref.at[pl.ds(0, tm), :], xbuf.at[s], gsem.at[s]).wait()

    def wait_scatter(count, s):
        for _ in range(copies_per_row):
            _wait_rows(count,
                       lambda m: pltpu.make_async_copy(obuf.at[s, pl.ds(0, m), :], out_ref.at[pl.ds(0, m), :],
                                                       ssem.at[s]),
                       lambda: scatter_row(0, s, 0))

    @pl.when(i == 0)
    def _():
        start_gather(nv, c0_ref[0], 0)
        if not dual:
            out2_ref[...] = jnp.zeros_like(out2_ref)

    for slot in range(2):
        @pl.when(parity == slot)
        def _():
            start_gather(nv_next, c0_ref[nxt], 1 - slot)
            start_scatter(nv_prev, c0_ref[prev], 1 - slot)
            wait_gather(nv, slot)

    def compute_tile(rows):
        x = xbuf[parity, 0:rows, 0:d]
        xn = _rms(x, g_ref[0]).astype(BF16)
        acc = x
        for lane, (gate_ref, up_ref, dn_ref) in enumerate(((ga_ref, ua_ref, da_ref), (gb_ref, ub_ref, db_ref))):
            gate = _dot(xn, gate_ref[0])
            up = _dot(xn, up_ref[0])
            act = (gate * _sigmoid(gate) * up).astype(BF16)
            acc = acc + _dot(act, dn_ref[0]) * xbuf[parity, 0:rows, d + 1 + lane:d + 2 + lane]
        if final_norm:
            acc = _rms(acc, gf_ref[...])
        wait_scatter(nv_prev2, parity)
        obuf[parity, 0:rows] = acc

    half = tm // 2

    @pl.when(nv > half)
    def _():
        compute_tile(tm)

    @pl.when((nv > 0) & (nv <= half))
    def _():
        compute_tile(half)

    @pl.when(nv == 0)
    def _():
        wait_scatter(nv_prev2, parity)

    @pl.when(i == n - 1)
    def _():
        wait_scatter(nv_prev, 1 - parity)


def _moe_experts(haug, counts, g_all, w_gate, w_up, w_dn, layer, g_final, final_norm,
                 row_of=None, layout2=None, cast=None):
    t, da = haug.shape
    d = da - LANES
    f = w_dn.shape[1]
    tm = TM_MOE
    n_tiles = t // tm + N_BUCKETS
    dual = layout2 is not None

    bucket = haug[:, d].astype(jnp.int32)
    order = jnp.argsort(bucket, stable=True).astype(jnp.int32)
    pad = lambda a: jnp.concatenate([a, jnp.zeros((tm,), jnp.int32)])
    dest = order if row_of is None else row_of[order]
    row2 = layout2[1][order] if dual else order
    col2 = layout2[2][order] if dual else order
    sizes = counts[0, :N_BUCKETS].astype(jnp.int32)
    tiles_per = (sizes + tm - 1) // tm
    tile_end = jnp.cumsum(tiles_per)
    tile_start = tile_end - tiles_per
    n_used = tile_end[-1]
    start_sorted = jnp.cumsum(sizes) - sizes
    ids = jnp.arange(n_tiles, dtype=jnp.int32)
    tb = jnp.sum((tile_end[None, :] <= jnp.minimum(ids, n_used - 1)[:, None]).astype(jnp.int32), axis=1)
    k = ids - tile_start[tb]
    nv = jnp.where(ids < n_used, jnp.clip(sizes[tb] - k * tm, 0, tm), 0).astype(jnp.int32)
    c0 = jnp.where(ids < n_used, start_sorted[tb] + k * tm, 0).astype(jnp.int32)
    grp = tb // N_PAIRS
    pair = tb % N_PAIRS
    tile_a = (grp * EXPERTS_PER_GROUP + jnp.asarray(PAIR_LO, jnp.int32)[pair]).astype(jnp.int32)
    tile_b = (grp * EXPERTS_PER_GROUP + jnp.asarray(PAIR_HI, jnp.int32)[pair]).astype(jnp.int32)

    any_space = pl.BlockSpec(memory_space=pl.ANY)
    expert_a = lambda i, o, ds, r2, c2, c, v, ea, eb: (ea[i], 0, 0)
    expert_b = lambda i, o, ds, r2, c2, c, v, ea, eb: (eb[i], 0, 0)
    in_specs = [any_space,
                pl.BlockSpec((1, 1, d), lambda i, *_: (layer, 0, 0)),
                pl.BlockSpec((1, d, f), expert_a), pl.BlockSpec((1, d, f), expert_b),
                pl.BlockSpec((1, d, f), expert_a), pl.BlockSpec((1, d, f), expert_b),
                pl.BlockSpec((1, f, d), expert_a), pl.BlockSpec((1, f, d), expert_b),
                pl.BlockSpec((1, d), lambda i, *_: (0, 0))]
    shape2 = (layout2[0], SEQ_MINOR * d) if dual else (SUBLANES, LANES)
    out_specs = [any_space, any_space if dual else pl.BlockSpec((SUBLANES, LANES), lambda i, *_: (0, 0))]
    out_shape = [jax.ShapeDtypeStruct((t, d), F32), jax.ShapeDtypeStruct(shape2, F32)]
    operands = [haug, g_all, w_gate, w_gate, w_up, w_up, w_dn, w_dn, g_final.reshape(1, d)]
    if cast is not None:
        cast_in, cast_out, cast_shape = _cast_io(cast, n_tiles, lambda i, *_: i)
        in_specs.append(cast_in)
        out_specs.append(cast_out)
        out_shape.append(cast_shape)
        operands.append(cast.src)
    grid_spec = pltpu.PrefetchScalarGridSpec(
        num_scalar_prefetch=8,
        grid=(n_tiles,),
        in_specs=in_specs,
        out_specs=out_specs,
        scratch_shapes=[pltpu.VMEM((2, tm, da), F32), pltpu.VMEM((2, tm, d), F32),
                        pltpu.SemaphoreType.DMA((2,)), pltpu.SemaphoreType.DMA((2,))],
    )
    outs = pl.pallas_call(
        functools.partial(_expert_kernel, final_norm=final_norm, dual=dual, hosts_cast=cast is not None),
        grid_spec=grid_spec,
        out_shape=out_shape,
        compiler_params=_params(1),
        name="moe_experts",
    )(pad(order), pad(dest), pad(row2), pad(col2), c0, nv, tile_a, tile_b, *operands)
    return outs[0], (outs[1] if dual else None), (outs[2] if cast is not None else None)


def kernel(x, mem, g_mix, g_xattn, g_mem, g_moe, g_final, w_a_in, w_a_conv, b_a_conv, g_a_ln, b_a_ln, w_a_out, w_b_out, w_c_in, w_c_conv, w_c_out, w_xq, w_xk, w_xv, w_xo, w_route_group, b_route_group, w_route_expert, b_route_expert, w_gate_up, w_down):
    bsz, s, d = x.shape
    t = bsz * s
    depth = g_mix.shape[0]
    mem2d = mem.reshape(bsz * mem.shape[1], d)
    bf = lambda w: w.astype(BF16)
    g_mix, g_xattn, g_mem, g_moe = _rows3(g_mix), _rows3(g_xattn), _rows3(g_mem), _rows3(g_moe)
    b_a_conv, g_a_ln, b_a_ln = _rows3(b_a_conv), _rows3(g_a_ln), _rows3(b_a_ln)
    w_a_in, w_a_out, w_b_out, w_c_in, w_c_out = bf(w_a_in), bf(w_a_out), bf(w_b_out), bf(w_c_in), bf(w_c_out)
    w_xq, w_xk, w_xv, w_xo = bf(w_xq), bf(w_xk), bf(w_xv), bf(w_xo)

    d_exp = w_down.shape[2]
    gate_job = lambda layer: _CastJob(w_gate_up, layer, d, d_exp, 0)
    up_job = lambda layer: _CastJob(w_gate_up, layer, d, d_exp, 1)
    down_job = lambda layer: _CastJob(w_down, layer, d_exp, d, 0)
    mixer_hosts_two = lambda layer: layer % N_MIXERS == 0

    fm_of_nat, nat_of_fm, row1, col1 = _fourier_row_orders(bsz, s)
    h = x
    fourier_in = None
    w_dn = None
    for i in range(depth):
        kind, j = i % N_MIXERS, i // N_MIXERS
        next_is_fourier = i + 1 < depth and (i + 1) % N_MIXERS == 1
        if kind == 0:
            u, w_gate = _a_in(h.reshape(t, d), g_mix, i, w_a_in, j, gate_job(i))
            h, w_up = _a_conv(u.reshape(bsz, s, d), h, w_a_conv, b_a_conv, g_a_ln, b_a_ln, w_a_out, j, up_job(i))
        elif kind == 1:
            x1, h_fm = fourier_in if fourier_in is not None else _fourier_layouts_from_natural(h)
            h, w_gate = _fourier_mixer(x1, h_fm, g_mix, i, w_b_out, j, gate_job(i))
        else:
            gate, cv = _c_in(h.reshape(t, d), g_mix, i, w_c_in, j)
            h, w_gate = _c_conv(cv.reshape(bsz, s, d), gate.reshape(bsz, s, d), h, w_c_conv, w_c_out, j, gate_job(i))
        k, v = _memory_kv(mem2d, g_mem, w_xk, w_xv, i)
        xattn_job = down_job(i) if mixer_hosts_two(i) else up_job(i)
        haug, counts, w_cast = _cross_attention_route(
            h, g_xattn, w_xq, k.reshape(bsz, -1, d), v.reshape(bsz, -1, d), w_xo, g_moe,
            w_route_group[i], b_route_group[i], w_route_expert[i], b_route_expert[i], i, xattn_job)
        if mixer_hosts_two(i):
            w_dn = w_cast
        else:
            w_up = w_cast
            if w_dn is None:
                w_dn = bf(w_down[i])
        rows_are_fm = kind == 1
        final = i == depth - 1
        next_cast = down_job(i + 1) if (not final and not mixer_hosts_two(i + 1)) else None
        if next_is_fourier:
            row_of = fm_of_nat[nat_of_fm] if rows_are_fm else fm_of_nat
            layout2 = (bsz * (s // SEQ_MINOR),
                       row1[nat_of_fm] if rows_are_fm else row1, col1[nat_of_fm] if rows_are_fm else col1)
        else:
            row_of = nat_of_fm if rows_are_fm else None
            layout2 = None
        out, out2, w_dn = _moe_experts(haug, counts, g_moe, w_gate, w_up, w_dn, i, g_final, final,
                                       row_of, layout2, next_cast)
        if next_is_fourier:
            fourier_in, h = (out2, out.reshape(bsz, s, d)), None
        else:
            fourier_in, h = None, out.reshape(bsz, s, d)
    return h
```

```python
import functools
import math
from typing import NamedTuple

import numpy as np
import jax
import jax.numpy as jnp
from jax import lax
from jax.experimental import pallas as pl
from jax.experimental.pallas import tpu as pltpu

F32 = jnp.float32
BF16 = jnp.bfloat16
EPS = 1e-6

N_MIXERS = 3
N_FOURIER_GROUPS = 8
N_HEADS = 4
N_GROUPS = 8
EXPERTS_PER_GROUP = 4
N_PAIRS = 6
N_BUCKETS = N_GROUPS * N_PAIRS
PAIR_LO = (0, 0, 0, 1, 1, 2)
PAIR_HI = (1, 2, 3, 2, 3, 3)

LANES = 128
SUBLANES = 8
SEQ_MINOR = 128
HALO = 16
VMEM_LIMIT_BYTES = 56 * 1024 * 1024

TM_DENSE = 512
TM_MOE = 256
ROW_CHUNK = 64
N_CHUNK = 512
CONV_COLS = 256


def _params(n_axes):
    return pltpu.CompilerParams(
        dimension_semantics=("arbitrary",) * n_axes,
        vmem_limit_bytes=VMEM_LIMIT_BYTES)


def _layer_block(shape, layer):
    zeros = (0,) * (len(shape) - 1)
    return pl.BlockSpec((1,) + tuple(shape[1:]), lambda *_: (layer,) + zeros, pipeline_mode=pl.Buffered(1))


def _rows3(a):
    return a.reshape(a.shape[0], 1, a.shape[1])


def _rms(x, g):
    return x * lax.rsqrt(jnp.mean(x * x, axis=-1, keepdims=True) + EPS) * g


def _sigmoid(x):
    return 1.0 / (1.0 + jnp.exp(-x))


def _dot(a, b):
    return jnp.dot(a, b, preferred_element_type=F32)


class _CastJob(NamedTuple):
    src: jax.Array
    layer: int
    rows: int
    cols: int
    col_block: int


def _cast_io(job, n_steps, step_of):
    n_experts = job.src.shape[1]
    per_step = -(-n_experts // n_steps)
    assert n_experts % per_step == 0
    last = n_experts // per_step - 1
    block = lambda *g: jnp.minimum(step_of(*g), last)
    in_spec = pl.BlockSpec((1, per_step, job.rows, job.cols), lambda *g: (job.layer, block(*g), 0, job.col_block))
    out_spec = pl.BlockSpec((per_step, job.rows, job.cols), lambda *g: (block(*g), 0, 0))
    return in_spec, out_spec, jax.ShapeDtypeStruct((n_experts, job.rows, job.cols), BF16)


def _cast_step(src_ref, dst_ref):
    dst_ref[...] = src_ref[0].astype(BF16)


def _a_in_kernel(x_ref, g_ref, w_ref, cast_src_ref, u_ref, cast_dst_ref, xn_ref):
    d = x_ref.shape[1]
    xn_ref[...] = _rms(x_ref[...], g_ref[0]).astype(BF16)
    for c in range(0, d, N_CHUNK):
        a = _dot(xn_ref[...], w_ref[0, :, c:c + N_CHUNK])
        gate = _dot(xn_ref[...], w_ref[0, :, d + c:d + c + N_CHUNK])
        u_ref[:, c:c + N_CHUNK] = (a * _sigmoid(gate)).astype(BF16)
    _cast_step(cast_src_ref, cast_dst_ref)


def _a_in(h, g_all, layer, w_all, j, cast):
    t, d = h.shape
    steps = t // TM_DENSE
    cast_in, cast_out, cast_shape = _cast_io(cast, steps, lambda i: i)
    return pl.pallas_call(
        _a_in_kernel,
        grid=(steps,),
        in_specs=[pl.BlockSpec((TM_DENSE, d), lambda i: (i, 0)),
                  _layer_block(g_all.shape, layer),
                  _layer_block(w_all.shape, j),
                  cast_in],
        out_specs=[pl.BlockSpec((TM_DENSE, d), lambda i: (i, 0)), cast_out],
        out_shape=[jax.ShapeDtypeStruct((t, d), BF16), cast_shape],
        scratch_shapes=[pltpu.VMEM((TM_DENSE, d), BF16)],
        compiler_params=_params(1),
        name="conformer_in",
    )(h, g_all, w_all, cast.src)


def _c_in_kernel(x_ref, g_ref, w_ref, b_ref, cv_ref, xn_ref):
    d = x_ref.shape[1]
    xn_ref[...] = _rms(x_ref[...], g_ref[0]).astype(BF16)
    for c in range(0, d, N_CHUNK):
        b_ref[:, c:c + N_CHUNK] = _dot(xn_ref[...], w_ref[0, :, c:c + N_CHUNK]).astype(BF16)
        cg = _dot(xn_ref[...], w_ref[0, :, d + c:d + c + N_CHUNK])
        v = _dot(xn_ref[...], w_ref[0, :, 2 * d + c:2 * d + c + N_CHUNK])
        cv_ref[:, c:c + N_CHUNK] = (cg * v).astype(BF16)


def _c_in(h, g_all, layer, w_all, j):
    t, d = h.shape
    tile = pl.BlockSpec((TM_DENSE, d), lambda i: (i, 0))
    return pl.pallas_call(
        _c_in_kernel,
        grid=(t // TM_DENSE,),
        in_specs=[tile, _layer_block(g_all.shape, layer), _layer_block(w_all.shape, j)],
        out_specs=[tile, tile],
        out_shape=[jax.ShapeDtypeStruct((t, d), BF16)] * 2,
        scratch_shapes=[pltpu.VMEM((TM_DENSE, d), BF16)],
        compiler_params=_params(1),
        name="shortconv_in",
    )(h, g_all, w_all)


def _fill_ext(ext_ref, cur_ref, prev_ref, next_ref, tm):
    i = pl.program_id(1)
    n = pl.num_programs(1)
    prev = prev_ref[0].astype(F32)
    nxt = next_ref[0].astype(F32)
    ext_ref[0:HALO, :] = jnp.where(i > 0, prev, jnp.zeros_like(prev))
    ext_ref[HALO + tm:HALO + tm + HALO, :] = jnp.where(i < n - 1, nxt, jnp.zeros_like(nxt))

    def body(j, carry):
        r0 = pl.multiple_of(j * ROW_CHUNK, ROW_CHUNK)
        ext_ref[pl.ds(HALO + r0, ROW_CHUNK), :] = cur_ref[0, pl.ds(r0, ROW_CHUNK), :].astype(F32)
        return carry

    lax.fori_loop(0, tm // ROW_CHUNK, body, 0)


def _depthwise(ext_ref, w_ref, cv_ref, width, tm):
    d = cv_ref.shape[1]
    rows = ROW_CHUNK
    span = rows + 2 * HALO
    off = HALO - width // 2

    def body(j, carry):
        r0 = pl.multiple_of(j * rows, rows)
        for c in range(0, d, CONV_COLS):
            blk = ext_ref[pl.ds(r0, span), c:c + CONV_COLS]
            acc = jnp.zeros((rows, CONV_COLS), F32)
            for r in range(SUBLANES):
                taps = [k for k in range(width) if (off + k) % SUBLANES == r]
                if not taps:
                    continue
                shifted = blk if r == 0 else pltpu.roll(blk, span - r, axis=0)
                for k in taps:
                    q = (off + k) // SUBLANES * SUBLANES
                    acc = acc + shifted[q:q + rows, :] * w_ref[0, k:k + 1, c:c + CONV_COLS]
            cv_ref[pl.ds(r0, rows), c:c + CONV_COLS] = acc
        return carry

    lax.fori_loop(0, tm // rows, body, 0)


def _project_residual(out_ref, h_ref, v_ref, w_ref):
    d = v_ref.shape[1]
    for c in range(0, d, N_CHUNK):
        out_ref[0, :, c:c + N_CHUNK] = h_ref[0, :, c:c + N_CHUNK] + _dot(v_ref[...], w_ref[0, :, c:c + N_CHUNK])


def _a_conv_kernel(cur_ref, prev_ref, next_ref, h_ref, wc_ref, bc_ref, gl_ref, bl_ref, wo_ref, cast_src_ref,
                   out_ref, cast_dst_ref, ext_ref, cv_ref, v_ref, *, width):
    tm = cur_ref.shape[1]
    _fill_ext(ext_ref, cur_ref, prev_ref, next_ref, tm)
    _depthwise(ext_ref, wc_ref, cv_ref, width, tm)

    def body(j, carry):
        r0 = pl.multiple_of(j * ROW_CHUNK, ROW_CHUNK)
        u = cv_ref[pl.ds(r0, ROW_CHUNK), :] + bc_ref[0]
        mu = jnp.mean(u, axis=-1, keepdims=True)
        uc = u - mu
        var = jnp.mean(uc * uc, axis=-1, keepdims=True)
        y = uc * lax.rsqrt(var + EPS) * gl_ref[0] + bl_ref[0]
        v_ref[pl.ds(r0, ROW_CHUNK), :] = (y * _sigmoid(y)).astype(BF16)
        return carry

    lax.fori_loop(0, tm // ROW_CHUNK, body, 0)
    _project_residual(out_ref, h_ref, v_ref, wo_ref)
    _cast_step(cast_src_ref, cast_dst_ref)


def _c_conv_kernel(cur_ref, prev_ref, next_ref, gate_ref, h_ref, wc_ref, wo_ref, cast_src_ref,
                   out_ref, cast_dst_ref, ext_ref, cv_ref, v_ref, *, width):
    tm = cur_ref.shape[1]
    _fill_ext(ext_ref, cur_ref, prev_ref, next_ref, tm)
    _depthwise(ext_ref, wc_ref, cv_ref, width, tm)

    def body(j, carry):
        r0 = pl.multiple_of(j * ROW_CHUNK, ROW_CHUNK)
        gate = gate_ref[0, pl.ds(r0, ROW_CHUNK), :].astype(F32)
        v_ref[pl.ds(r0, ROW_CHUNK), :] = (gate * cv_ref[pl.ds(r0, ROW_CHUNK), :]).astype(BF16)
        return carry

    lax.fori_loop(0, tm // ROW_CHUNK, body, 0)
    _project_residual(out_ref, h_ref, v_ref, wo_ref)
    _cast_step(cast_src_ref, cast_dst_ref)


def _conv_specs(s, d, tm):
    per = tm // HALO
    last = s // HALO - 1
    cur = pl.BlockSpec((1, tm, d), lambda b, i: (b, i, 0))
    prev = pl.BlockSpec((1, HALO, d), lambda b, i: (b, jnp.maximum(i * per - 1, 0), 0))
    nxt = pl.BlockSpec((1, HALO, d), lambda b, i: (b, jnp.minimum((i + 1) * per, last), 0))
    return cur, prev, nxt


def _conv_scratch(tm, d):
    return [pltpu.VMEM((tm + 2 * HALO, d), F32), pltpu.VMEM((tm, d), F32), pltpu.VMEM((tm, d), BF16)]


def _a_conv(u, h, w_conv, b_conv, g_ln, b_ln, w_out, j, cast):
    bsz, s, d = h.shape
    tm = TM_DENSE
    per = s // tm
    cur, prev, nxt = _conv_specs(s, d, tm)
    cast_in, cast_out, cast_shape = _cast_io(cast, bsz * per, lambda b, i: b * per + i)
    return pl.pallas_call(
        functools.partial(_a_conv_kernel, width=w_conv.shape[1]),
        grid=(bsz, per),
        in_specs=[cur, prev, nxt, cur, _layer_block(w_conv.shape, j), _layer_block(b_conv.shape, j),
                  _layer_block(g_ln.shape, j), _layer_block(b_ln.shape, j), _layer_block(w_out.shape, j), cast_in],
        out_specs=[cur, cast_out],
        out_shape=[jax.ShapeDtypeStruct((bsz, s, d), F32), cast_shape],
        scratch_shapes=_conv_scratch(tm, d),
        compiler_params=_params(2),
        name="conformer_conv_out",
    )(u, u, u, h, w_conv, b_conv, g_ln, b_ln, w_out, cast.src)


def _c_conv(cv, gate, h, w_conv, w_out, j, cast):
    bsz, s, d = h.shape
    tm = TM_DENSE
    per = s // tm
    cur, prev, nxt = _conv_specs(s, d, tm)
    cast_in, cast_out, cast_shape = _cast_io(cast, bsz * per, lambda b, i: b * per + i)
    return pl.pallas_call(
        functools.partial(_c_conv_kernel, width=w_conv.shape[1]),
        grid=(bsz, per),
        in_specs=[cur, prev, nxt, cur, cur, _layer_block(w_conv.shape, j), _layer_block(w_out.shape, j), cast_in],
        out_specs=[cur, cast_out],
        out_shape=[jax.ShapeDtypeStruct((bsz, s, d), F32), cast_shape],
        scratch_shapes=_conv_scratch(tm, d),
        compiler_params=_params(2),
        name="shortconv_conv_out",
    )(cv, cv, cv, gate, h, w_conv, w_out, cast.src)


def _dft_tables(s, group):
    s1 = s // SEQ_MINOR

    def cs(n_rows, n_cols, period):
        m = (np.outer(np.arange(n_rows), np.arange(n_cols)) % period).astype(np.float64)
        ang = 2.0 * np.pi * m / period
        return np.cos(ang), -np.sin(ang)

    c1, i1 = cs(s1, s1, s1)
    w1 = np.concatenate([c1, i1], axis=0)
    twr, twi = cs(s1, SEQ_MINOR, s)
    c2, i2 = cs(SEQ_MINOR, SEQ_MINOR, SEQ_MINOR)
    w2 = np.block([[c2, -i2], [i2, c2]])
    cc, ic = cs(group, group, group)
    scale = 1.0 / math.sqrt(float(s) * float(group))
    return (jnp.asarray(w1, BF16), jnp.asarray(twr, F32), jnp.asarray(twi, F32),
            jnp.asarray(w2, BF16), jnp.asarray(cc * scale, BF16), jnp.asarray(-ic * scale, BF16))


def _b_stage1_kernel(x_ref, g_ref, w1_ref, twr_ref, twi_ref, br_ref, bi_ref, *, nb):
    s1 = x_ref.shape[1]
    d = g_ref.shape[2]
    brs, bis = [], []
    for n in range(nb):
        xn = _rms(x_ref[0, :, n * d:(n + 1) * d], g_ref[0]).astype(BF16)
        y = _dot(w1_ref[...], xn)
        yr, yi = y[:s1], y[s1:]
        tr = twr_ref[0, :, n:n + 1]
        ti = twi_ref[0, :, n:n + 1]
        brs.append(yr * tr - yi * ti)
        bis.append(yr * ti + yi * tr)
    br_ref[0] = jnp.swapaxes(jnp.stack(brs, axis=0), 0, 1).astype(BF16)
    bi_ref[0] = jnp.swapaxes(jnp.stack(bis, axis=0), 0, 1).astype(BF16)


def _b_stage2_kernel(br_ref, bi_ref, h_ref, w2_ref, cc_ref, sc_ref, wo_ref, cast_src_ref, out_ref, cast_dst_ref,
                     ar_ref, ai_ref, f_ref, *, ka, group):
    d = br_ref.shape[2]
    m = SEQ_MINOR
    for n in range(ka):
        cat = jnp.concatenate([br_ref[0, n * m:(n + 1) * m, :], bi_ref[0, n * m:(n + 1) * m, :]], axis=0)
        a = _dot(w2_ref[...], cat)
        ar_ref[n * m:(n + 1) * m, :] = a[:m].astype(BF16)
        ai_ref[n * m:(n + 1) * m, :] = a[m:].astype(BF16)
    for c in range(0, d, group):
        f = _dot(ar_ref[:, c:c + group], cc_ref[...]) + _dot(ai_ref[:, c:c + group], sc_ref[...])
        f_ref[:, c:c + group] = f.astype(BF16)
    for c in range(0, d, N_CHUNK):
        out_ref[0, :, c:c + N_CHUNK] = h_ref[0, :, c:c + N_CHUNK] + _dot(f_ref[...], wo_ref[0, :, c:c + N_CHUNK])
    _cast_step(cast_src_ref, cast_dst_ref)


def _const_block(shape):
    nd = len(shape)
    return pl.BlockSpec(shape, lambda *_: (0,) * nd, pipeline_mode=pl.Buffered(1))


def _fourier_row_orders(bsz, s):
    s1 = s // SEQ_MINOR
    k = np.arange(s)
    base = (np.arange(bsz) * s)[:, None]
    fm_of_nat = (base + ((k % s1) * SEQ_MINOR + k // s1)[None, :]).reshape(-1)
    nat_of_fm = np.argsort(fm_of_nat)
    row1 = ((np.arange(bsz) * s1)[:, None] + (k // SEQ_MINOR)[None, :]).reshape(-1)
    col1 = np.tile(k % SEQ_MINOR, bsz)
    as_i32 = lambda a: jnp.asarray(a, jnp.int32)
    return as_i32(fm_of_nat), as_i32(nat_of_fm), as_i32(row1), as_i32(col1)


def _fourier_layouts_from_natural(h):
    bsz, s, d = h.shape
    s1 = s // SEQ_MINOR
    x1 = h.reshape(bsz * s1, SEQ_MINOR * d)
    h_fm = h.reshape(bsz, SEQ_MINOR, s1, d).transpose(0, 2, 1, 3).reshape(bsz, s, d)
    return x1, h_fm


def _fourier_mixer(x1, h_fm, g_all, layer, w_out, j, cast):
    bsz, s, d = h_fm.shape
    s1 = s // SEQ_MINOR
    group = d // N_FOURIER_GROUPS
    nb = 16
    ka = 4
    w1, twr, twi, w2, cc, sc = _dft_tables(s, group)
    steps1 = SEQ_MINOR // nb
    twr = twr.reshape(s1, steps1, nb).transpose(1, 0, 2)
    twi = twi.reshape(s1, steps1, nb).transpose(1, 0, 2)

    x1 = x1.reshape(bsz, s1, SEQ_MINOR * d)
    blk1 = pl.BlockSpec((1, s1, nb * d), lambda b, i: (b, 0, i))
    tw_spec = pl.BlockSpec((1, s1, nb), lambda b, i: (i, 0, 0))
    br, bi = pl.pallas_call(
        functools.partial(_b_stage1_kernel, nb=nb),
        grid=(bsz, steps1),
        in_specs=[blk1, _layer_block(g_all.shape, layer), _const_block((2 * s1, s1)), tw_spec, tw_spec],
        out_specs=[pl.BlockSpec((1, s1, nb, d), lambda b, i: (b, 0, i, 0))] * 2,
        out_shape=[jax.ShapeDtypeStruct((bsz, s1, SEQ_MINOR, d), BF16)] * 2,
        compiler_params=_params(2),
        name="fourier_stage1",
    )(x1, g_all, w1, twr, twi)

    br = br.reshape(bsz, s, d)
    bi = bi.reshape(bsz, s, d)
    rows = pl.BlockSpec((1, ka * SEQ_MINOR, d), lambda b, i: (b, i, 0))
    per = s1 // ka
    cast_in, cast_out, cast_shape = _cast_io(cast, bsz * per, lambda b, i: b * per + i)
    return pl.pallas_call(
        functools.partial(_b_stage2_kernel, ka=ka, group=group),
        grid=(bsz, per),
        in_specs=[rows, rows, rows, _const_block((2 * SEQ_MINOR, 2 * SEQ_MINOR)),
                  _const_block((group, group)), _const_block((group, group)), _layer_block(w_out.shape, j), cast_in],
        out_specs=[rows, cast_out],
        out_shape=[jax.ShapeDtypeStruct((bsz, s, d), F32), cast_shape],
        scratch_shapes=[pltpu.VMEM((ka * SEQ_MINOR, d), BF16)] * 3,
        compiler_params=_params(2),
        name="fourier_stage2_out",
    )(br, bi, h_fm, w2, cc, sc, w_out, cast.src)


def _kv_kernel(mem_ref, g_ref, wk_ref, wv_ref, k_ref, v_ref):
    memn = _rms(mem_ref[...], g_ref[0]).astype(BF16)
    k_ref[...] = _dot(memn, wk_ref[0]).astype(BF16)
    v_ref[...] = _dot(memn, wv_ref[0]).astype(BF16)


def _memory_kv(mem2d, g_all, wk, wv, layer):
    r, d = mem2d.shape
    col = pl.BlockSpec((1, d, N_CHUNK), lambda c: (layer, 0, c))
    out = pl.BlockSpec((r, N_CHUNK), lambda c: (0, c))
    return pl.pallas_call(
        _kv_kernel,
        grid=(d // N_CHUNK,),
        in_specs=[_const_block((r, d)), _layer_block(g_all.shape, layer), col, col],
        out_specs=[out, out],
        out_shape=[jax.ShapeDtypeStruct((r, d), BF16)] * 2,
        compiler_params=_params(1),
        name="memory_kv",
    )(mem2d, g_all, wk, wv)


def _split_bf16(a):
    hi = a.astype(BF16)
    lo = (a - hi.astype(F32)).astype(BF16)
    return hi, lo


def _route(xn, whl, bias):
    xh, xl = _split_bf16(xn)
    both = _dot(xh, whl)
    logits = both[:, :LANES] + both[:, LANES:] + _dot(xl, whl[:, :LANES]) + bias
    lane = lax.broadcasted_iota(jnp.int32, logits.shape, 1)
    neg = jnp.float32(-jnp.inf)
    big = jnp.int32(LANES)

    gl = jnp.where(lane < N_GROUPS, logits, neg)
    gmax = jnp.max(gl, axis=-1, keepdims=True)
    grp = jnp.min(jnp.where(gl == gmax, lane, big), axis=-1, keepdims=True)
    g_w = 1.0 / jnp.sum(jnp.exp(gl - gmax), axis=-1, keepdims=True)

    first = N_GROUPS + grp * EXPERTS_PER_GROUP
    el = jnp.where((lane >= first) & (lane < first + EXPERTS_PER_GROUP), logits, neg)
    m1 = jnp.max(el, axis=-1, keepdims=True)
    i1 = jnp.min(jnp.where(el == m1, lane, big), axis=-1, keepdims=True)
    el2 = jnp.where(lane == i1, neg, el)
    m2 = jnp.max(el2, axis=-1, keepdims=True)
    i2 = jnp.min(jnp.where(el2 == m2, lane, big), axis=-1, keepdims=True)
    e21 = jnp.exp(m2 - m1)
    w1 = g_w / (1.0 + e21)
    w2 = g_w * e21 / (1.0 + e21)

    a1 = i1 - first
    a2 = i2 - first
    lo = jnp.minimum(a1, a2)
    hi = jnp.maximum(a1, a2)
    pair = lo * 3 - jnp.right_shift(lo * (lo - 1), 1) + hi - lo - 1
    bucket = grp * N_PAIRS + pair
    w_lo = jnp.where(a1 < a2, w1, w2)
    w_hi = jnp.where(a1 < a2, w2, w1)
    routed = jnp.where(lane == 0, bucket.astype(F32),
                       jnp.where(lane == 1, w_lo, jnp.where(lane == 2, w_hi, 0.0)))
    return routed, (lane == bucket).astype(F32)


def _xattn_kernel(h_ref, g_ref, wq_ref, k_ref, v_ref, wo_ref, gm_ref, rw_ref, rb_ref, cast_src_ref,
                  out_ref, cnt_ref, cast_dst_ref, xn_ref, o_ref):
    d = h_ref.shape[2]
    hd = d // N_HEADS
    scale = float(hd) ** -0.5
    xn_ref[...] = _rms(h_ref[0], g_ref[0]).astype(BF16)
    for c in range(0, d, hd):
        q = _dot(xn_ref[...], wq_ref[0, :, c:c + hd]).astype(BF16)
        s = lax.dot_general(q, k_ref[0, :, c:c + hd], (((1,), (1,)), ((), ())),
                            preferred_element_type=F32) * scale
        e = jnp.exp(s - jnp.max(s, axis=-1, keepdims=True))
        p = (e / jnp.sum(e, axis=-1, keepdims=True)).astype(BF16)
        o_ref[:, c:c + hd] = _dot(p, v_ref[0, :, c:c + hd]).astype(BF16)
    for c in range(0, d, N_CHUNK):
        out_ref[:, c:c + N_CHUNK] = h_ref[0, :, c:c + N_CHUNK] + _dot(o_ref[...], wo_ref[0, :, c:c + N_CHUNK])

    routed, onehot = _route(_rms(out_ref[:, 0:d], gm_ref[0]), rw_ref[...], rb_ref[...])
    out_ref[:, d:d + LANES] = routed

    @pl.when((pl.program_id(0) == 0) & (pl.program_id(1) == 0))
    def _():
        cnt_ref[...] = jnp.zeros_like(cnt_ref)

    cnt_ref[...] += jnp.sum(onehot, axis=0, keepdims=True)
    _cast_step(cast_src_ref, cast_dst_ref)


def _cross_attention_route(h, g_all, wq, k, v, wo, g_moe_all, w_rg, b_rg, w_re, b_re, layer, cast):
    bsz, s, d = h.shape
    n_mem = k.shape[1]
    tm = TM_DENSE
    per = s // tm
    n = w_rg.shape[1] + w_re.shape[1]
    w = jnp.zeros((d, LANES), F32).at[:, :n].set(jnp.concatenate([w_rg, w_re], axis=1))
    b = jnp.zeros((1, LANES), F32).at[0, :n].set(jnp.concatenate([b_rg, b_re]))
    whl = jnp.concatenate(_split_bf16(w), axis=1)
    cast_in, cast_out, cast_shape = _cast_io(cast, bsz * per, lambda bb, i: bb * per + i)
    tile = pl.BlockSpec((1, tm, d), lambda bb, i: (bb, i, 0))
    kv = pl.BlockSpec((1, n_mem, d), lambda bb, i: (bb, 0, 0))
    return pl.pallas_call(
        _xattn_kernel,
        grid=(bsz, per),
        in_specs=[tile, _layer_block(g_all.shape, layer), _layer_block(wq.shape, layer), kv, kv,
                  _layer_block(wo.shape, layer), _layer_block(g_moe_all.shape, layer),
                  _const_block((d, 2 * LANES)), _const_block((1, LANES)), cast_in],
        out_specs=[pl.BlockSpec((tm, d + LANES), lambda bb, i: (bb * per + i, 0)),
                   pl.BlockSpec((1, LANES), lambda bb, i: (0, 0)), cast_out],
        out_shape=[jax.ShapeDtypeStruct((bsz * s, d + LANES), F32),
                   jax.ShapeDtypeStruct((1, LANES), F32), cast_shape],
        scratch_shapes=[pltpu.VMEM((tm, d), BF16)] * 2,
        compiler_params=_params(2),
        name="cross_attention_route",
    )(h, g_all, wq, k, v, wo, g_moe_all, whl, b, cast.src)


def _wait_rows(count, block_copy, row_copy):
    n8 = pl.multiple_of(jnp.right_shift(count, 3) * SUBLANES, SUBLANES)

    @pl.when(n8 > 0)
    def _():
        block_copy(n8).wait()

    def one(r, carry):
        row_copy().wait()
        return carry

    lax.fori_loop(0, count - n8, one, 0)


def _expert_kernel(order_ref, dest_ref, row2_ref, col2_ref, c0_ref, nv_ref, ea_ref, eb_ref,
                   haug_ref, g_ref, ga_ref, gb_ref, ua_ref, ub_ref, da_ref, db_ref, gf_ref, *rest,
                   final_norm, dual, hosts_cast):
    if hosts_cast:
        cast_src_ref, out_ref, out2_ref, cast_dst_ref, xbuf, obuf, gsem, ssem = rest
        _cast_step(cast_src_ref, cast_dst_ref)
    else:
        out_ref, out2_ref, xbuf, obuf, gsem, ssem = rest
    i = pl.program_id(0)
    n = pl.num_programs(0)
    parity = lax.rem(i, 2)
    tm = xbuf.shape[1]
    d = obuf.shape[2]
    copies_per_row = 2 if dual else 1

    nv = nv_ref[i]
    nxt = jnp.minimum(i + 1, n - 1)
    prev = jnp.maximum(i - 1, 0)
    nv_next = jnp.where(i + 1 < n, nv_ref[nxt], 0)
    nv_prev = jnp.where(i >= 1, nv_ref[prev], 0)
    nv_prev2 = jnp.where(i >= 2, nv_ref[jnp.maximum(i - 2, 0)], 0)

    def gather_row(tok, s, r):
        return pltpu.make_async_copy(haug_ref.at[pl.ds(tok, 1), :], xbuf.at[s, pl.ds(r, 1), :], gsem.at[s])

    def scatter_row(row, s, r):
        return pltpu.make_async_copy(obuf.at[s, pl.ds(r, 1), :], out_ref.at[pl.ds(row, 1), :], ssem.at[s])

    def scatter_row2(row, col, s, r):
        lanes = pl.ds(pl.multiple_of(col * d, d), d)
        return pltpu.make_async_copy(obuf.at[s, pl.ds(r, 1), :], out2_ref.at[pl.ds(row, 1), lanes], ssem.at[s])

    def start_scatter_row(p, s, r):
        scatter_row(dest_ref[p], s, r).start()
        if dual:
            scatter_row2(row2_ref[p], col2_ref[p], s, r).start()

    def start_rows(count, base, s, start_row, exact):
        @pl.when(count == tm)
        def _():
            for r in range(tm):
                start_row(base + r, s, r)

        @pl.when((count > 0) & (count < tm))
        def _():
            groups = jnp.right_shift(count if exact else count + (SUBLANES - 1), 3)
            for gi in range(tm // SUBLANES):
                @pl.when(gi < groups)
                def _():
                    for r in range(gi * SUBLANES, (gi + 1) * SUBLANES):
                        start_row(base + r, s, r)
            if exact:
                def one(r, carry):
                    start_row(base + r, s, r)
                    return carry

                lax.fori_loop(groups * SUBLANES, count, one, 0)

    def start_gather(count, base, s):
        start_rows(count, base, s, lambda p, s_, r: gather_row(order_ref[p], s_, r).start(), exact=False)

    def start_scatter(count, base, s):
        start_rows(count, base, s, start_scatter_row, exact=True)

    def wait_gather(count, s):
        m = pl.multiple_of(jnp.right_shift(count + (SUBLANES - 1), 3) * SUBLANES, SUBLANES)

        @pl.when(m > 0)
        def _():
            pltpu.make_async_copy(haug_ref.at[pl.ds(0, m), :], xbuf.at[s, pl.ds(0, m), :], gsem.at[s]).wait()

    def wait_scatter(count, s):
        for _ in range(copies_per_row):
            _wait_rows(count,
                       lambda m: pltpu.make_async_copy(obuf.at[s, pl.ds(0, m), :], out_ref.at[pl.ds(0, m), :],
                                                       ssem.at[s]),
                       lambda: scatter_row(0, s, 0))

    @pl.when(i == 0)
    def _():
        xbuf[...] = jnp.zeros_like(xbuf)
        start_gather(nv, c0_ref[0], 0)
        if not dual:
            out2_ref[...] = jnp.zeros_like(out2_ref)

    for slot in range(2):
        @pl.when(parity == slot)
        def _():
            start_gather(nv_next, c0_ref[nxt], 1 - slot)
            start_scatter(nv_prev, c0_ref[prev], 1 - slot)
            wait_gather(nv, slot)

    def compute_tile(rows):
        x = xbuf[parity, 0:rows, 0:d]
        xn = _rms(x, g_ref[0]).astype(BF16)
        acc = x
        for lane, (gate_ref, up_ref, dn_ref) in enumerate(((ga_ref, ua_ref, da_ref), (gb_ref, ub_ref, db_ref))):
            gate = _dot(xn, gate_ref[0])
            up = _dot(xn, up_ref[0])
            act = (gate * _sigmoid(gate) * up).astype(BF16)
            acc = acc + _dot(act, dn_ref[0]) * xbuf[parity, 0:rows, d + 1 + lane:d + 2 + lane]
        if final_norm:
            acc = _rms(acc, gf_ref[...])
        wait_scatter(nv_prev2, parity)
        obuf[parity, 0:rows] = acc

    half = tm // 2

    @pl.when(nv > half)
    def _():
        compute_tile(tm)

    @pl.when((nv > 0) & (nv <= half))
    def _():
        compute_tile(half)

    @pl.when(nv == 0)
    def _():
        wait_scatter(nv_prev2, parity)

    @pl.when(i == n - 1)
    def _():
        wait_scatter(nv_prev, 1 - parity)


def _moe_experts(haug, counts, g_all, w_gate, w_up, w_dn, layer, g_final, final_norm,
                 row_of=None, layout2=None, cast=None):
    t, da = haug.shape
    d = da - LANES
    f = w_dn.shape[1]
    tm = TM_MOE
    n_tiles = t // tm + N_BUCKETS
    dual = layout2 is not None

    bucket = haug[:, d].astype(jnp.int32)
    order = jnp.argsort(bucket, stable=True).astype(jnp.int32)
    pad = lambda a: jnp.concatenate([a, jnp.zeros((tm,), jnp.int32)])
    dest = order if row_of is None else row_of[order]
    row2 = layout2[1][order] if dual else order
    col2 = layout2[2][order] if dual else order
    sizes = counts[0, :N_BUCKETS].astype(jnp.int32)
    tiles_per = (sizes + tm - 1) // tm
    tile_end = jnp.cumsum(tiles_per)
    tile_start = tile_end - tiles_per
    n_used = tile_end[-1]
    start_sorted = jnp.cumsum(sizes) - sizes
    ids = jnp.arange(n_tiles, dtype=jnp.int32)
    tb = jnp.sum((tile_end[None, :] <= jnp.minimum(ids, n_used - 1)[:, None]).astype(jnp.int32), axis=1)
    k = ids - tile_start[tb]
    nv = jnp.where(ids < n_used, jnp.clip(sizes[tb] - k * tm, 0, tm), 0).astype(jnp.int32)
    c0 = jnp.where(ids < n_used, start_sorted[tb] + k * tm, 0).astype(jnp.int32)
    grp = tb // N_PAIRS
    pair = tb % N_PAIRS
    tile_a = (grp * EXPERTS_PER_GROUP + jnp.asarray(PAIR_LO, jnp.int32)[pair]).astype(jnp.int32)
    tile_b = (grp * EXPERTS_PER_GROUP + jnp.asarray(PAIR_HI, jnp.int32)[pair]).astype(jnp.int32)

    any_space = pl.BlockSpec(memory_space=pl.ANY)
    expert_a = lambda i, o, ds, r2, c2, c, v, ea, eb: (ea[i], 0, 0)
    expert_b = lambda i, o, ds, r2, c2, c, v, ea, eb: (eb[i], 0, 0)
    in_specs = [any_space,
                pl.BlockSpec((1, 1, d), lambda i, *_: (layer, 0, 0)),
                pl.BlockSpec((1, d, f), expert_a), pl.BlockSpec((1, d, f), expert_b),
                pl.BlockSpec((1, d, f), expert_a), pl.BlockSpec((1, d, f), expert_b),
                pl.BlockSpec((1, f, d), expert_a), pl.BlockSpec((1, f, d), expert_b),
                pl.BlockSpec((1, d), lambda i, *_: (0, 0))]
    shape2 = (layout2[0], SEQ_MINOR * d) if dual else (SUBLANES, LANES)
    out_specs = [any_space, any_space if dual else pl.BlockSpec((SUBLANES, LANES), lambda i, *_: (0, 0))]
    out_shape = [jax.ShapeDtypeStruct((t, d), F32), jax.ShapeDtypeStruct(shape2, F32)]
    operands = [haug, g_all, w_gate, w_gate, w_up, w_up, w_dn, w_dn, g_final.reshape(1, d)]
    if cast is not None:
        cast_in, cast_out, cast_shape = _cast_io(cast, n_tiles, lambda i, *_: i)
        in_specs.append(cast_in)
        out_specs.append(cast_out)
        out_shape.append(cast_shape)
        operands.append(cast.src)
    grid_spec = pltpu.PrefetchScalarGridSpec(
        num_scalar_prefetch=8,
        grid=(n_tiles,),
        in_specs=in_specs,
        out_specs=out_specs,
        scratch_shapes=[pltpu.VMEM((2, tm, da), F32), pltpu.VMEM((2, tm, d), F32),
                        pltpu.SemaphoreType.DMA((2,)), pltpu.SemaphoreType.DMA((2,))],
    )
    outs = pl.pallas_call(
        functools.partial(_expert_kernel, final_norm=final_norm, dual=dual, hosts_cast=cast is not None),
        grid_spec=grid_spec,
        out_shape=out_shape,
        compiler_params=_params(1),
        name="moe_experts",
    )(pad(order), pad(dest), pad(row2), pad(col2), c0, nv, tile_a, tile_b, *operands)
    return outs[0], (outs[1] if dual else None), (outs[2] if cast is not None else None)


def kernel(x, mem, g_mix, g_xattn, g_mem, g_moe, g_final, w_a_in, w_a_conv, b_a_conv, g_a_ln, b_a_ln, w_a_out, w_b_out, w_c_in, w_c_conv, w_c_out, w_xq, w_xk, w_xv, w_xo, w_route_group, b_route_group, w_route_expert, b_route_expert, w_gate_up, w_down):
    bsz, s, d = x.shape
    t = bsz * s
    depth = g_mix.shape[0]
    mem2d = mem.reshape(bsz * mem.shape[1], d)
    bf = lambda w: w.astype(BF16)
    g_mix, g_xattn, g_mem, g_moe = _rows3(g_mix), _rows3(g_xattn), _rows3(g_mem), _rows3(g_moe)
    b_a_conv, g_a_ln, b_a_ln = _rows3(b_a_conv), _rows3(g_a_ln), _rows3(b_a_ln)
    w_a_in, w_a_out, w_b_out, w_c_in, w_c_out = bf(w_a_in), bf(w_a_out), bf(w_b_out), bf(w_c_in), bf(w_c_out)
    w_xq, w_xk, w_xv, w_xo = bf(w_xq), bf(w_xk), bf(w_xv), bf(w_xo)

    d_exp = w_down.shape[2]
    gate_job = lambda layer: _CastJob(w_gate_up, layer, d, d_exp, 0)
    up_job = lambda layer: _CastJob(w_gate_up, layer, d, d_exp, 1)
    down_job = lambda layer: _CastJob(w_down, layer, d_exp, d, 0)
    mixer_hosts_two = lambda layer: layer % N_MIXERS == 0

    fm_of_nat, nat_of_fm, row1, col1 = _fourier_row_orders(bsz, s)
    h = x
    fourier_in = None
    w_dn = None
    for i in range(depth):
        kind, j = i % N_MIXERS, i // N_MIXERS
        next_is_fourier = i + 1 < depth and (i + 1) % N_MIXERS == 1
        if kind == 0:
            u, w_gate = _a_in(h.reshape(t, d), g_mix, i, w_a_in, j, gate_job(i))
            h, w_up = _a_conv(u.reshape(bsz, s, d), h, w_a_conv, b_a_conv, g_a_ln, b_a_ln, w_a_out, j, up_job(i))
        elif kind == 1:
            x1, h_fm = fourier_in if fourier_in is not None else _fourier_layouts_from_natural(h)
            h, w_gate = _fourier_mixer(x1, h_fm, g_mix, i, w_b_out, j, gate_job(i))
        else:
            gate, cv = _c_in(h.reshape(t, d), g_mix, i, w_c_in, j)
            h, w_gate = _c_conv(cv.reshape(bsz, s, d), gate.reshape(bsz, s, d), h, w_c_conv, w_c_out, j, gate_job(i))
        k, v = _memory_kv(mem2d, g_mem, w_xk, w_xv, i)
        xattn_job = down_job(i) if mixer_hosts_two(i) else up_job(i)
        haug, counts, w_cast = _cross_attention_route(
            h, g_xattn, w_xq, k.reshape(bsz, -1, d), v.reshape(bsz, -1, d), w_xo, g_moe,
            w_route_group[i], b_route_group[i], w_route_expert[i], b_route_expert[i], i, xattn_job)
        if mixer_hosts_two(i):
            w_dn = w_cast
        else:
            w_up = w_cast
            if w_dn is None:
                w_dn = bf(w_down[i])
        rows_are_fm = kind == 1
        final = i == depth - 1
        next_cast = down_job(i + 1) if (not final and not mixer_hosts_two(i + 1)) else None
        if next_is_fourier:
            row_of = fm_of_nat[nat_of_fm] if rows_are_fm else fm_of_nat
            layout2 = (bsz * (s // SEQ_MINOR),
                       row1[nat_of_fm] if rows_are_fm else row1, col1[nat_of_fm] if rows_are_fm else col1)
        else:
            row_of = nat_of_fm if rows_are_fm else None
            layout2 = None
        out, out2, w_dn = _moe_experts(haug, counts, g_moe, w_gate, w_up, w_dn, i, g_final, final,
                                       row_of, layout2, next_cast)
        if next_is_fourier:
            fourier_in, h = (out2, out.reshape(bsz, s, d)), None
        else:
            fourier_in, h = None, out.reshape(bsz, s, d)
    return h
```

```python
import functools
import math
from typing import NamedTuple

import numpy as np
import jax
import jax.numpy as jnp
from jax import lax
from jax.experimental import pallas as pl
from jax.experimental.pallas import tpu as pltpu

F32 = jnp.float32
BF16 = jnp.bfloat16
EPS = 1e-6

N_MIXERS = 3
N_FOURIER_GROUPS = 8
N_HEADS = 4
N_GROUPS = 8
EXPERTS_PER_GROUP = 4
N_PAIRS = 6
N_BUCKETS = N_GROUPS * N_PAIRS
PAIR_LO = (0, 0, 0, 1, 1, 2)
PAIR_HI = (1, 2, 3, 2, 3, 3)

LANES = 128
SUBLANES = 8
SEQ_MINOR = 128
HALO = 16
VMEM_LIMIT_BYTES = 56 * 1024 * 1024

TM_DENSE = 512
TM_MOE = 384
ROW_CHUNK = 64
N_CHUNK = 512
CONV_COLS = 256


def _params(n_axes):
    return pltpu.CompilerParams(
        dimension_semantics=("arbitrary",) * n_axes,
        vmem_limit_bytes=VMEM_LIMIT_BYTES)


def _layer_block(shape, layer):
    zeros = (0,) * (len(shape) - 1)
    return pl.BlockSpec((1,) + tuple(shape[1:]), lambda *_: (layer,) + zeros, pipeline_mode=pl.Buffered(1))


def _rows3(a):
    return a.reshape(a.shape[0], 1, a.shape[1])


def _rms(x, g):
    return x * lax.rsqrt(jnp.mean(x * x, axis=-1, keepdims=True) + EPS) * g


def _sigmoid(x):
    return 1.0 / (1.0 + jnp.exp(-x))


def _dot(a, b):
    return jnp.dot(a, b, preferred_element_type=F32)


class _CastJob(NamedTuple):
    src: jax.Array
    layer: int
    rows: int
    cols: int
    col_block: int


def _cast_io(job, n_steps, step_of):
    n_experts = job.src.shape[1]
    per_step = -(-n_experts // n_steps)
    assert n_experts % per_step == 0
    last = n_experts // per_step - 1
    block = lambda *g: jnp.minimum(step_of(*g), last)
    in_spec = pl.BlockSpec((1, per_step, job.rows, job.cols), lambda *g: (job.layer, block(*g), 0, job.col_block))
    out_spec = pl.BlockSpec((per_step, job.rows, job.cols), lambda *g: (block(*g), 0, 0))
    return in_spec, out_spec, jax.ShapeDtypeStruct((n_experts, job.rows, job.cols), BF16)


def _cast_step(src_ref, dst_ref):
    dst_ref[...] = src_ref[0].astype(BF16)


def _a_in_kernel(x_ref, g_ref, w_ref, cast_src_ref, u_ref, cast_dst_ref, xn_ref):
    d = x_ref.shape[1]
    xn_ref[...] = _rms(x_ref[...], g_ref[0]).astype(BF16)
    for c in range(0, d, N_CHUNK):
        a = _dot(xn_ref[...], w_ref[0, :, c:c + N_CHUNK])
        gate = _dot(xn_ref[...], w_ref[0, :, d + c:d + c + N_CHUNK])
        u_ref[:, c:c + N_CHUNK] = (a * _sigmoid(gate)).astype(BF16)
    _cast_step(cast_src_ref, cast_dst_ref)


def _a_in(h, g_all, layer, w_all, j, cast):
    t, d = h.shape
    steps = t // TM_DENSE
    cast_in, cast_out, cast_shape = _cast_io(cast, steps, lambda i: i)
    return pl.pallas_call(
        _a_in_kernel,
        grid=(steps,),
        in_specs=[pl.BlockSpec((TM_DENSE, d), lambda i: (i, 0)),
                  _layer_block(g_all.shape, layer),
                  _layer_block(w_all.shape, j),
                  cast_in],
        out_specs=[pl.BlockSpec((TM_DENSE, d), lambda i: (i, 0)), cast_out],
        out_shape=[jax.ShapeDtypeStruct((t, d), BF16), cast_shape],
        scratch_shapes=[pltpu.VMEM((TM_DENSE, d), BF16)],
        compiler_params=_params(1),
        name="conformer_in",
    )(h, g_all, w_all, cast.src)


def _c_in_kernel(x_ref, g_ref, w_ref, b_ref, cv_ref, xn_ref):
    d = x_ref.shape[1]
    xn_ref[...] = _rms(x_ref[...], g_ref[0]).astype(BF16)
    for c in range(0, d, N_CHUNK):
        b_ref[:, c:c + N_CHUNK] = _dot(xn_ref[...], w_ref[0, :, c:c + N_CHUNK]).astype(BF16)
        cg = _dot(xn_ref[...], w_ref[0, :, d + c:d + c + N_CHUNK])
        v = _dot(xn_ref[...], w_ref[0, :, 2 * d + c:2 * d + c + N_CHUNK])
        cv_ref[:, c:c + N_CHUNK] = (cg * v).astype(BF16)


def _c_in(h, g_all, layer, w_all, j):
    t, d = h.shape
    tile = pl.BlockSpec((TM_DENSE, d), lambda i: (i, 0))
    return pl.pallas_call(
        _c_in_kernel,
        grid=(t // TM_DENSE,),
        in_specs=[tile, _layer_block(g_all.shape, layer), _layer_block(w_all.shape, j)],
        out_specs=[tile, tile],
        out_shape=[jax.ShapeDtypeStruct((t, d), BF16)] * 2,
        scratch_shapes=[pltpu.VMEM((TM_DENSE, d), BF16)],
        compiler_params=_params(1),
        name="shortconv_in",
    )(h, g_all, w_all)


def _fill_ext(ext_ref, cur_ref, prev_ref, next_ref, tm):
    i = pl.program_id(1)
    n = pl.num_programs(1)
    prev = prev_ref[0].astype(F32)
    nxt = next_ref[0].astype(F32)
    ext_ref[0:HALO, :] = jnp.where(i > 0, prev, jnp.zeros_like(prev))
    ext_ref[HALO + tm:HALO + tm + HALO, :] = jnp.where(i < n - 1, nxt, jnp.zeros_like(nxt))

    def body(j, carry):
        r0 = pl.multiple_of(j * ROW_CHUNK, ROW_CHUNK)
        ext_ref[pl.ds(HALO + r0, ROW_CHUNK), :] = cur_ref[0, pl.ds(r0, ROW_CHUNK), :].astype(F32)
        return carry

    lax.fori_loop(0, tm // ROW_CHUNK, body, 0)


def _depthwise(ext_ref, w_ref, cv_ref, width, tm):
    d = cv_ref.shape[1]
    rows = ROW_CHUNK
    span = rows + 2 * HALO
    off = HALO - width // 2

    def body(j, carry):
        r0 = pl.multiple_of(j * rows, rows)
        for c in range(0, d, CONV_COLS):
            blk = ext_ref[pl.ds(r0, span), c:c + CONV_COLS]
            acc = jnp.zeros((rows, CONV_COLS), F32)
            for r in range(SUBLANES):
                taps = [k for k in range(width) if (off + k) % SUBLANES == r]
                if not taps:
                    continue
                shifted = blk if r == 0 else pltpu.roll(blk, span - r, axis=0)
                for k in taps:
                    q = (off + k) // SUBLANES * SUBLANES
                    acc = acc + shifted[q:q + rows, :] * w_ref[0, k:k + 1, c:c + CONV_COLS]
            cv_ref[pl.ds(r0, rows), c:c + CONV_COLS] = acc
        return carry

    lax.fori_loop(0, tm // rows, body, 0)


def _project_residual(out_ref, h_ref, v_ref, w_ref):
    d = v_ref.shape[1]
    for c in range(0, d, N_CHUNK):
        out_ref[0, :, c:c + N_CHUNK] = h_ref[0, :, c:c + N_CHUNK] + _dot(v_ref[...], w_ref[0, :, c:c + N_CHUNK])


def _a_conv_kernel(cur_ref, prev_ref, next_ref, h_ref, wc_ref, bc_ref, gl_ref, bl_ref, wo_ref, cast_src_ref,
                   out_ref, cast_dst_ref, ext_ref, cv_ref, v_ref, *, width):
    tm = cur_ref.shape[1]
    _fill_ext(ext_ref, cur_ref, prev_ref, next_ref, tm)
    _depthwise(ext_ref, wc_ref, cv_ref, width, tm)

    def body(j, carry):
        r0 = pl.multiple_of(j * ROW_CHUNK, ROW_CHUNK)
        u = cv_ref[pl.ds(r0, ROW_CHUNK), :] + bc_ref[0]
        mu = jnp.mean(u, axis=-1, keepdims=True)
        uc = u - mu
        var = jnp.mean(uc * uc, axis=-1, keepdims=True)
        y = uc * lax.rsqrt(var + EPS) * gl_ref[0] + bl_ref[0]
        v_ref[pl.ds(r0, ROW_CHUNK), :] = (y * _sigmoid(y)).astype(BF16)
        return carry

    lax.fori_loop(0, tm // ROW_CHUNK, body, 0)
    _project_residual(out_ref, h_ref, v_ref, wo_ref)
    _cast_step(cast_src_ref, cast_dst_ref)


def _c_conv_kernel(cur_ref, prev_ref, next_ref, gate_ref, h_ref, wc_ref, wo_ref, cast_src_ref,
                   out_ref, cast_dst_ref, ext_ref, cv_ref, v_ref, *, width):
    tm = cur_ref.shape[1]
    _fill_ext(ext_ref, cur_ref, prev_ref, next_ref, tm)
    _depthwise(ext_ref, wc_ref, cv_ref, width, tm)

    def body(j, carry):
        r0 = pl.multiple_of(j * ROW_CHUNK, ROW_CHUNK)
        gate = gate_ref[0, pl.ds(r0, ROW_CHUNK), :].astype(F32)
        v_ref[pl.ds(r0, ROW_CHUNK), :] = (gate * cv_ref[pl.ds(r0, ROW_CHUNK), :]).astype(BF16)
        return carry

    lax.fori_loop(0, tm // ROW_CHUNK, body, 0)
    _project_residual(out_ref, h_ref, v_ref, wo_ref)
    _cast_step(cast_src_ref, cast_dst_ref)


def _conv_specs(s, d, tm):
    per = tm // HALO
    last = s // HALO - 1
    cur = pl.BlockSpec((1, tm, d), lambda b, i: (b, i, 0))
    prev = pl.BlockSpec((1, HALO, d), lambda b, i: (b, jnp.maximum(i * per - 1, 0), 0))
    nxt = pl.BlockSpec((1, HALO, d), lambda b, i: (b, jnp.minimum((i + 1) * per, last), 0))
    return cur, prev, nxt


def _conv_scratch(tm, d):
    return [pltpu.VMEM((tm + 2 * HALO, d), F32), pltpu.VMEM((tm, d), F32), pltpu.VMEM((tm, d), BF16)]


def _a_conv(u, h, w_conv, b_conv, g_ln, b_ln, w_out, j, cast):
    bsz, s, d = h.shape
    tm = TM_DENSE
    per = s // tm
    cur, prev, nxt = _conv_specs(s, d, tm)
    cast_in, cast_out, cast_shape = _cast_io(cast, bsz * per, lambda b, i: b * per + i)
    return pl.pallas_call(
        functools.partial(_a_conv_kernel, width=w_conv.shape[1]),
        grid=(bsz, per),
        in_specs=[cur, prev, nxt, cur, _layer_block(w_conv.shape, j), _layer_block(b_conv.shape, j),
                  _layer_block(g_ln.shape, j), _layer_block(b_ln.shape, j), _layer_block(w_out.shape, j), cast_in],
        out_specs=[cur, cast_out],
        out_shape=[jax.ShapeDtypeStruct((bsz, s, d), F32), cast_shape],
        scratch_shapes=_conv_scratch(tm, d),
        compiler_params=_params(2),
        name="conformer_conv_out",
    )(u, u, u, h, w_conv, b_conv, g_ln, b_ln, w_out, cast.src)


def _c_conv(cv, gate, h, w_conv, w_out, j, cast):
    bsz, s, d = h.shape
    tm = TM_DENSE
    per = s // tm
    cur, prev, nxt = _conv_specs(s, d, tm)
    cast_in, cast_out, cast_shape = _cast_io(cast, bsz * per, lambda b, i: b * per + i)
    return pl.pallas_call(
        functools.partial(_c_conv_kernel, width=w_conv.shape[1]),
        grid=(bsz, per),
        in_specs=[cur, prev, nxt, cur, cur, _layer_block(w_conv.shape, j), _layer_block(w_out.shape, j), cast_in],
        out_specs=[cur, cast_out],
        out_shape=[jax.ShapeDtypeStruct((bsz, s, d), F32), cast_shape],
        scratch_shapes=_conv_scratch(tm, d),
        compiler_params=_params(2),
        name="shortconv_conv_out",
    )(cv, cv, cv, gate, h, w_conv, w_out, cast.src)


def _dft_tables(s, group):
    s1 = s // SEQ_MINOR

    def cs(n_rows, n_cols, period):
        m = (np.outer(np.arange(n_rows), np.arange(n_cols)) % period).astype(np.float64)
        ang = 2.0 * np.pi * m / period
        return np.cos(ang), -np.sin(ang)

    c1, i1 = cs(s1, s1, s1)
    w1 = np.concatenate([c1, i1], axis=0)
    twr, twi = cs(s1, SEQ_MINOR, s)
    c2, i2 = cs(SEQ_MINOR, SEQ_MINOR, SEQ_MINOR)
    w2 = np.block([[c2, -i2], [i2, c2]])
    cc, ic = cs(group, group, group)
    scale = 1.0 / math.sqrt(float(s) * float(group))
    return (jnp.asarray(w1, BF16), jnp.asarray(twr, F32), jnp.asarray(twi, F32),
            jnp.asarray(w2, BF16), jnp.asarray(cc * scale, BF16), jnp.asarray(-ic * scale, BF16))


def _b_stage1_kernel(x_ref, g_ref, w1_ref, twr_ref, twi_ref, br_ref, bi_ref, *, nb):
    s1 = x_ref.shape[1]
    d = g_ref.shape[2]
    brs, bis = [], []
    for n in range(nb):
        xn = _rms(x_ref[0, :, n * d:(n + 1) * d], g_ref[0]).astype(BF16)
        y = _dot(w1_ref[...], xn)
        yr, yi = y[:s1], y[s1:]
        tr = twr_ref[0, :, n:n + 1]
        ti = twi_ref[0, :, n:n + 1]
        brs.append(yr * tr - yi * ti)
        bis.append(yr * ti + yi * tr)
    br_ref[0] = jnp.swapaxes(jnp.stack(brs, axis=0), 0, 1).astype(BF16)
    bi_ref[0] = jnp.swapaxes(jnp.stack(bis, axis=0), 0, 1).astype(BF16)


def _b_stage2_kernel(br_ref, bi_ref, h_ref, w2_ref, cc_ref, sc_ref, wo_ref, cast_src_ref, out_ref, cast_dst_ref,
                     ar_ref, ai_ref, f_ref, *, ka, group):
    d = br_ref.shape[2]
    m = SEQ_MINOR
    for n in range(ka):
        cat = jnp.concatenate([br_ref[0, n * m:(n + 1) * m, :], bi_ref[0, n * m:(n + 1) * m, :]], axis=0)
        a = _dot(w2_ref[...], cat)
        ar_ref[n * m:(n + 1) * m, :] = a[:m].astype(BF16)
        ai_ref[n * m:(n + 1) * m, :] = a[m:].astype(BF16)
    for c in range(0, d, group):
        f = _dot(ar_ref[:, c:c + group], cc_ref[...]) + _dot(ai_ref[:, c:c + group], sc_ref[...])
        f_ref[:, c:c + group] = f.astype(BF16)
    for c in range(0, d, N_CHUNK):
        out_ref[0, :, c:c + N_CHUNK] = h_ref[0, :, c:c + N_CHUNK] + _dot(f_ref[...], wo_ref[0, :, c:c + N_CHUNK])
    _cast_step(cast_src_ref, cast_dst_ref)


def _const_block(shape):
    nd = len(shape)
    return pl.BlockSpec(shape, lambda *_: (0,) * nd, pipeline_mode=pl.Buffered(1))


def _fourier_row_orders(bsz, s):
    s1 = s // SEQ_MINOR
    k = np.arange(s)
    base = (np.arange(bsz) * s)[:, None]
    fm_of_nat = (base + ((k % s1) * SEQ_MINOR + k // s1)[None, :]).reshape(-1)
    nat_of_fm = np.argsort(fm_of_nat)
    row1 = ((np.arange(bsz) * s1)[:, None] + (k // SEQ_MINOR)[None, :]).reshape(-1)
    col1 = np.tile(k % SEQ_MINOR, bsz)
    as_i32 = lambda a: jnp.asarray(a, jnp.int32)
    return as_i32(fm_of_nat), as_i32(nat_of_fm), as_i32(row1), as_i32(col1)


def _fourier_layouts_from_natural(h):
    bsz, s, d = h.shape
    s1 = s // SEQ_MINOR
    x1 = h.reshape(bsz * s1, SEQ_MINOR * d)
    h_fm = h.reshape(bsz, SEQ_MINOR, s1, d).transpose(0, 2, 1, 3).reshape(bsz, s, d)
    return x1, h_fm


def _fourier_mixer(x1, h_fm, g_all, layer, w_out, j, cast):
    bsz, s, d = h_fm.shape
    s1 = s // SEQ_MINOR
    group = d // N_FOURIER_GROUPS
    nb = 16
    ka = 4
    w1, twr, twi, w2, cc, sc = _dft_tables(s, group)
    steps1 = SEQ_MINOR // nb
    twr = twr.reshape(s1, steps1, nb).transpose(1, 0, 2)
    twi = twi.reshape(s1, steps1, nb).transpose(1, 0, 2)

    x1 = x1.reshape(bsz, s1, SEQ_MINOR * d)
    blk1 = pl.BlockSpec((1, s1, nb * d), lambda b, i: (b, 0, i))
    tw_spec = pl.BlockSpec((1, s1, nb), lambda b, i: (i, 0, 0))
    br, bi = pl.pallas_call(
        functools.partial(_b_stage1_kernel, nb=nb),
        grid=(bsz, steps1),
        in_specs=[blk1, _layer_block(g_all.shape, layer), _const_block((2 * s1, s1)), tw_spec, tw_spec],
        out_specs=[pl.BlockSpec((1, s1, nb, d), lambda b, i: (b, 0, i, 0))] * 2,
        out_shape=[jax.ShapeDtypeStruct((bsz, s1, SEQ_MINOR, d), BF16)] * 2,
        compiler_params=_params(2),
        name="fourier_stage1",
    )(x1, g_all, w1, twr, twi)

    br = br.reshape(bsz, s, d)
    bi = bi.reshape(bsz, s, d)
    rows = pl.BlockSpec((1, ka * SEQ_MINOR, d), lambda b, i: (b, i, 0))
    per = s1 // ka
    cast_in, cast_out, cast_shape = _cast_io(cast, bsz * per, lambda b, i: b * per + i)
    return pl.pallas_call(
        functools.partial(_b_stage2_kernel, ka=ka, group=group),
        grid=(bsz, per),
        in_specs=[rows, rows, rows, _const_block((2 * SEQ_MINOR, 2 * SEQ_MINOR)),
                  _const_block((group, group)), _const_block((group, group)), _layer_block(w_out.shape, j), cast_in],
        out_specs=[rows, cast_out],
        out_shape=[jax.ShapeDtypeStruct((bsz, s, d), F32), cast_shape],
        scratch_shapes=[pltpu.VMEM((ka * SEQ_MINOR, d), BF16)] * 3,
        compiler_params=_params(2),
        name="fourier_stage2_out",
    )(br, bi, h_fm, w2, cc, sc, w_out, cast.src)


def _kv_kernel(mem_ref, g_ref, wk_ref, wv_ref, k_ref, v_ref):
    memn = _rms(mem_ref[...], g_ref[0]).astype(BF16)
    k_ref[...] = _dot(memn, wk_ref[0].astype(BF16)).astype(BF16)
    v_ref[...] = _dot(memn, wv_ref[0].astype(BF16)).astype(BF16)


def _memory_kv(mem2d, g_all, wk, wv, layer):
    r, d = mem2d.shape
    col = pl.BlockSpec((1, d, N_CHUNK), lambda c: (layer, 0, c))
    out = pl.BlockSpec((r, N_CHUNK), lambda c: (0, c))
    return pl.pallas_call(
        _kv_kernel,
        grid=(d // N_CHUNK,),
        in_specs=[_const_block((r, d)), _layer_block(g_all.shape, layer), col, col],
        out_specs=[out, out],
        out_shape=[jax.ShapeDtypeStruct((r, d), BF16)] * 2,
        compiler_params=_params(1),
        name="memory_kv",
    )(mem2d, g_all, wk, wv)


def _split_bf16(a):
    hi = a.astype(BF16)
    lo = (a - hi.astype(F32)).astype(BF16)
    return hi, lo


def _route(xn, whl, bias):
    xh, xl = _split_bf16(xn)
    both = _dot(xh, whl)
    logits = both[:, :LANES] + both[:, LANES:] + _dot(xl, whl[:, :LANES]) + bias
    lane = lax.broadcasted_iota(jnp.int32, logits.shape, 1)
    neg = jnp.float32(-jnp.inf)
    big = jnp.int32(LANES)

    gl = jnp.where(lane < N_GROUPS, logits, neg)
    gmax = jnp.max(gl, axis=-1, keepdims=True)
    grp = jnp.min(jnp.where(gl == gmax, lane, big), axis=-1, keepdims=True)
    g_w = 1.0 / jnp.sum(jnp.exp(gl - gmax), axis=-1, keepdims=True)

    first = N_GROUPS + grp * EXPERTS_PER_GROUP
    el = jnp.where((lane >= first) & (lane < first + EXPERTS_PER_GROUP), logits, neg)
    m1 = jnp.max(el, axis=-1, keepdims=True)
    i1 = jnp.min(jnp.where(el == m1, lane, big), axis=-1, keepdims=True)
    el2 = jnp.where(lane == i1, neg, el)
    m2 = jnp.max(el2, axis=-1, keepdims=True)
    i2 = jnp.min(jnp.where(el2 == m2, lane, big), axis=-1, keepdims=True)
    e21 = jnp.exp(m2 - m1)
    w1 = g_w / (1.0 + e21)
    w2 = g_w * e21 / (1.0 + e21)

    a1 = i1 - first
    a2 = i2 - first
    lo = jnp.minimum(a1, a2)
    hi = jnp.maximum(a1, a2)
    pair = lo * 3 - jnp.right_shift(lo * (lo - 1), 1) + hi - lo - 1
    bucket = grp * N_PAIRS + pair
    w_lo = jnp.where(a1 < a2, w1, w2)
    w_hi = jnp.where(a1 < a2, w2, w1)
    routed = jnp.where(lane == 0, bucket.astype(F32),
                       jnp.where(lane == 1, w_lo, jnp.where(lane == 2, w_hi, 0.0)))
    return routed, (lane == bucket).astype(F32)


def _xattn_kernel(h_ref, g_ref, wq_ref, k_ref, v_ref, wo_ref, gm_ref, rw_ref, rb_ref, cast_src_ref,
                  out_ref, cnt_ref, cast_dst_ref, xn_ref, o_ref):
    d = h_ref.shape[2]
    hd = d // N_HEADS
    scale = float(hd) ** -0.5
    xn_ref[...] = _rms(h_ref[0], g_ref[0]).astype(BF16)
    for c in range(0, d, hd):
        q = _dot(xn_ref[...], wq_ref[0, :, c:c + hd]).astype(BF16)
        s = lax.dot_general(q, k_ref[0, :, c:c + hd], (((1,), (1,)), ((), ())),
                            preferred_element_type=F32) * scale
        e = jnp.exp(s - jnp.max(s, axis=-1, keepdims=True))
        p = (e / jnp.sum(e, axis=-1, keepdims=True)).astype(BF16)
        o_ref[:, c:c + hd] = _dot(p, v_ref[0, :, c:c + hd]).astype(BF16)
    for c in range(0, d, N_CHUNK):
        out_ref[:, c:c + N_CHUNK] = h_ref[0, :, c:c + N_CHUNK] + _dot(o_ref[...], wo_ref[0, :, c:c + N_CHUNK])

    routed, onehot = _route(_rms(out_ref[:, 0:d], gm_ref[0]), rw_ref[...], rb_ref[...])
    out_ref[:, d:d + LANES] = routed

    @pl.when((pl.program_id(0) == 0) & (pl.program_id(1) == 0))
    def _():
        cnt_ref[...] = jnp.zeros_like(cnt_ref)

    cnt_ref[...] += jnp.sum(onehot, axis=0, keepdims=True)
    _cast_step(cast_src_ref, cast_dst_ref)


def _cross_attention_route(h, g_all, wq, k, v, wo, g_moe_all, w_rg, b_rg, w_re, b_re, layer, cast):
    bsz, s, d = h.shape
    n_mem = k.shape[1]
    tm = TM_DENSE
    per = s // tm
    n = w_rg.shape[1] + w_re.shape[1]
    w = jnp.zeros((d, LANES), F32).at[:, :n].set(jnp.concatenate([w_rg, w_re], axis=1))
    b = jnp.zeros((1, LANES), F32).at[0, :n].set(jnp.concatenate([b_rg, b_re]))
    whl = jnp.concatenate(_split_bf16(w), axis=1)
    cast_in, cast_out, cast_shape = _cast_io(cast, bsz * per, lambda bb, i: bb * per + i)
    tile = pl.BlockSpec((1, tm, d), lambda bb, i: (bb, i, 0))
    kv = pl.BlockSpec((1, n_mem, d), lambda bb, i: (bb, 0, 0))
    return pl.pallas_call(
        _xattn_kernel,
        grid=(bsz, per),
        in_specs=[tile, _layer_block(g_all.shape, layer), _layer_block(wq.shape, layer), kv, kv,
                  _layer_block(wo.shape, layer), _layer_block(g_moe_all.shape, layer),
                  _const_block((d, 2 * LANES)), _const_block((1, LANES)), cast_in],
        out_specs=[pl.BlockSpec((tm, d + LANES), lambda bb, i: (bb * per + i, 0)),
                   pl.BlockSpec((1, LANES), lambda bb, i: (0, 0)), cast_out],
        out_shape=[jax.ShapeDtypeStruct((bsz * s, d + LANES), F32),
                   jax.ShapeDtypeStruct((1, LANES), F32), cast_shape],
        scratch_shapes=[pltpu.VMEM((tm, d), BF16)] * 2,
        compiler_params=_params(2),
        name="cross_attention_route",
    )(h, g_all, wq, k, v, wo, g_moe_all, whl, b, cast.src)


def _wait_rows(count, block_copy, row_copy):
    n8 = pl.multiple_of(jnp.right_shift(count, 3) * SUBLANES, SUBLANES)

    @pl.when(n8 > 0)
    def _():
        block_copy(n8).wait()

    def one(r, carry):
        row_copy().wait()
        return carry

    lax.fori_loop(0, count - n8, one, 0)


def _expert_kernel(order_ref, dest_ref, row2_ref, col2_ref, c0_ref, nv_ref, ea_ref, eb_ref,
                   haug_ref, g_ref, ga_ref, gb_ref, ua_ref, ub_ref, da_ref, db_ref, gf_ref, *rest,
                   final_norm, dual, hosts_cast):
    if hosts_cast:
        cast_src_ref, out_ref, out2_ref, cast_dst_ref, xbuf, obuf, gsem, ssem = rest
        _cast_step(cast_src_ref, cast_dst_ref)
    else:
        out_ref, out2_ref, xbuf, obuf, gsem, ssem = rest
    i = pl.program_id(0)
    n = pl.num_programs(0)
    parity = lax.rem(i, 2)
    tm = xbuf.shape[1]
    d = obuf.shape[2]
    copies_per_row = 2 if dual else 1

    nv = nv_ref[i]
    nxt = jnp.minimum(i + 1, n - 1)
    prev = jnp.maximum(i - 1, 0)
    nv_next = jnp.where(i + 1 < n, nv_ref[nxt], 0)
    nv_prev = jnp.where(i >= 1, nv_ref[prev], 0)
    nv_prev2 = jnp.where(i >= 2, nv_ref[jnp.maximum(i - 2, 0)], 0)

    def gather_row(tok, s, r):
        return pltpu.make_async_copy(haug_ref.at[pl.ds(tok, 1), :], xbuf.at[s, pl.ds(r, 1), :], gsem.at[s])

    def scatter_row(row, s, r):
        return pltpu.make_async_copy(obuf.at[s, pl.ds(r, 1), :], out_ref.at[pl.ds(row, 1), :], ssem.at[s])

    def scatter_row2(row, col, s, r):
        lanes = pl.ds(pl.multiple_of(col * d, d), d)
        return pltpu.make_async_copy(obuf.at[s, pl.ds(r, 1), :], out2_ref.at[pl.ds(row, 1), lanes], ssem.at[s])

    def start_scatter_row(p, s, r):
        scatter_row(dest_ref[p], s, r).start()
        if dual:
            scatter_row2(row2_ref[p], col2_ref[p], s, r).start()

    def start_rows(count, base, s, start_row, exact):
        @pl.when(count == tm)
        def _():
            for r in range(tm):
                start_row(base + r, s, r)

        @pl.when((count > 0) & (count < tm))
        def _():
            groups = jnp.right_shift(count if exact else count + (SUBLANES - 1), 3)
            for gi in range(tm // SUBLANES):
                @pl.when(gi < groups)
                def _():
                    for r in range(gi * SUBLANES, (gi + 1) * SUBLANES):
                        start_row(base + r, s, r)
            if exact:
                def one(r, carry):
                    start_row(base + r, s, r)
                    return carry

                lax.fori_loop(groups * SUBLANES, count, one, 0)

    def start_gather(count, base, s):
        start_rows(count, base, s, lambda p, s_, r: gather_row(order_ref[p], s_, r).start(), exact=False)

    def start_scatter(count, base, s):
        start_rows(count, base, s, start_scatter_row, exact=True)

    def wait_gather(count, s):
        m = pl.multiple_of(jnp.right_shift(count + (SUBLANES - 1), 3) * SUBLANES, SUBLANES)

        @pl.when(m > 0)
        def _():
            pltpu.make_async_copy(haug_ref.at[pl.ds(0, m), :], xbuf.at[s, pl.ds(0, m), :], gsem.at[s]).wait()

    def wait_scatter(count, s):
        for _ in range(copies_per_row):
            _wait_rows(count,
                       lambda m: pltpu.make_async_copy(obuf.at[s, pl.ds(0, m), :], out_ref.at[pl.ds(0, m), :],
                                                       ssem.at[s]),
                       lambda: scatter_row(0, s, 0))

    @pl.when(i == 0)
    def _():
        xbuf[...] = jnp.zeros_like(xbuf)
        start_gather(nv, c0_ref[0], 0)
        if not dual:
            out2_ref[...] = jnp.zeros_like(out2_ref)

    for slot in range(2):
        @pl.when(parity == slot)
        def _():
            start_gather(nv_next, c0_ref[nxt], 1 - slot)
            start_scatter(nv_prev, c0_ref[prev], 1 - slot)
            wait_gather(nv, slot)

    def compute_tile(rows):
        x = xbuf[parity, 0:rows, 0:d]
        xn = _rms(x, g_ref[0]).astype(BF16)
        acc = x
        for lane, (gate_ref, up_ref, dn_ref) in enumerate(((ga_ref, ua_ref, da_ref), (gb_ref, ub_ref, db_ref))):
            gate = _dot(xn, gate_ref[0])
            up = _dot(xn, up_ref[0])
            act = (gate * _sigmoid(gate) * up).astype(BF16)
            acc = acc + _dot(act, dn_ref[0]) * xbuf[parity, 0:rows, d + 1 + lane:d + 2 + lane]
        if final_norm:
            acc = _rms(acc, gf_ref[...])
        wait_scatter(nv_prev2, parity)
        obuf[parity, 0:rows] = acc

    half = tm // 2

    @pl.when(nv > half)
    def _():
        compute_tile(tm)

    @pl.when((nv > 0) & (nv <= half))
    def _():
        compute_tile(half)

    @pl.when(nv == 0)
    def _():
        wait_scatter(nv_prev2, parity)

    @pl.when(i == n - 1)
    def _():
        wait_scatter(nv_prev, 1 - parity)


def _moe_experts(haug, counts, g_all, w_gate, w_up, w_dn, layer, g_final, final_norm,
                 row_of=None, layout2=None, cast=None):
    t, da = haug.shape
    d = da - LANES
    f = w_dn.shape[1]
    tm = TM_MOE
    n_tiles = (t + N_BUCKETS * (tm - 1)) // tm + 1
    dual = layout2 is not None

    bucket = haug[:, d].astype(jnp.int32)
    order = jnp.argsort(bucket, stable=True).astype(jnp.int32)
    pad = lambda a: jnp.concatenate([a, jnp.zeros((tm,), jnp.int32)])
    dest = order if row_of is None else row_of[order]
    row2 = layout2[1][order] if dual else order
    col2 = layout2[2][order] if dual else order
    sizes = counts[0, :N_BUCKETS].astype(jnp.int32)
    tiles_per = (sizes + tm - 1) // tm
    tile_end = jnp.cumsum(tiles_per)
    tile_start = tile_end - tiles_per
    n_used = tile_end[-1]
    start_sorted = jnp.cumsum(sizes) - sizes
    ids = jnp.arange(n_tiles, dtype=jnp.int32)
    tb = jnp.sum((tile_end[None, :] <= jnp.minimum(ids, n_used - 1)[:, None]).astype(jnp.int32), axis=1)
    k = ids - tile_start[tb]
    nv = jnp.where(ids < n_used, jnp.clip(sizes[tb] - k * tm, 0, tm), 0).astype(jnp.int32)
    c0 = jnp.where(ids < n_used, start_sorted[tb] + k * tm, 0).astype(jnp.int32)
    grp = tb // N_PAIRS
    pair = tb % N_PAIRS
    tile_a = (grp * EXPERTS_PER_GROUP + jnp.asarray(PAIR_LO, jnp.int32)[pair]).astype(jnp.int32)
    tile_b = (grp * EXPERTS_PER_GROUP + jnp.asarray(PAIR_HI, jnp.int32)[pair]).astype(jnp.int32)

    any_space = pl.BlockSpec(memory_space=pl.ANY)
    expert_a = lambda i, o, ds, r2, c2, c, v, ea, eb: (ea[i], 0, 0)
    expert_b = lambda i, o, ds, r2, c2, c, v, ea, eb: (eb[i], 0, 0)
    in_specs = [any_space,
                pl.BlockSpec((1, 1, d), lambda i, *_: (layer, 0, 0)),
                pl.BlockSpec((1, d, f), expert_a), pl.BlockSpec((1, d, f), expert_b),
                pl.BlockSpec((1, d, f), expert_a), pl.BlockSpec((1, d, f), expert_b),
                pl.BlockSpec((1, f, d), expert_a), pl.BlockSpec((1, f, d), expert_b),
                pl.BlockSpec((1, d), lambda i, *_: (0, 0))]
    shape2 = (layout2[0], SEQ_MINOR * d) if dual else (SUBLANES, LANES)
    out_specs = [any_space, any_space if dual else pl.BlockSpec((SUBLANES, LANES), lambda i, *_: (0, 0))]
    out_shape = [jax.ShapeDtypeStruct((t, d), F32), jax.ShapeDtypeStruct(shape2, F32)]
    operands = [haug, g_all, w_gate, w_gate, w_up, w_up, w_dn, w_dn, g_final.reshape(1, d)]
    if cast is not None:
        cast_in, cast_out, cast_shape = _cast_io(cast, n_tiles, lambda i, *_: i)
        in_specs.append(cast_in)
        out_specs.append(cast_out)
        out_shape.append(cast_shape)
        operands.append(cast.src)
    grid_spec = pltpu.PrefetchScalarGridSpec(
        num_scalar_prefetch=8,
        grid=(n_tiles,),
        in_specs=in_specs,
        out_specs=out_specs,
        scratch_shapes=[pltpu.VMEM((2, tm, da), F32), pltpu.VMEM((2, tm, d), F32),
                        pltpu.SemaphoreType.DMA((2,)), pltpu.SemaphoreType.DMA((2,))],
    )
    outs = pl.pallas_call(
        functools.partial(_expert_kernel, final_norm=final_norm, dual=dual, hosts_cast=cast is not None),
        grid_spec=grid_spec,
        out_shape=out_shape,
        compiler_params=_params(1),
        name="moe_experts",
    )(pad(order), pad(dest), pad(row2), pad(col2), c0, nv, tile_a, tile_b, *operands)
    return outs[0], (outs[1] if dual else None), (outs[2] if cast is not None else None)


def kernel(x, mem, g_mix, g_xattn, g_mem, g_moe, g_final, w_a_in, w_a_conv, b_a_conv, g_a_ln, b_a_ln, w_a_out, w_b_out, w_c_in, w_c_conv, w_c_out, w_xq, w_xk, w_xv, w_xo, w_route_group, b_route_group, w_route_expert, b_route_expert, w_gate_up, w_down):
    bsz, s, d = x.shape
    t = bsz * s
    depth = g_mix.shape[0]
    mem2d = mem.reshape(bsz * mem.shape[1], d)
    bf = lambda w: w.astype(BF16)
    g_mix, g_xattn, g_mem, g_moe = _rows3(g_mix), _rows3(g_xattn), _rows3(g_mem), _rows3(g_moe)
    b_a_conv, g_a_ln, b_a_ln = _rows3(b_a_conv), _rows3(g_a_ln), _rows3(b_a_ln)
    w_a_in, w_a_out, w_b_out, w_c_in, w_c_out = bf(w_a_in), bf(w_a_out), bf(w_b_out), bf(w_c_in), bf(w_c_out)
    w_xq, w_xo = bf(w_xq), bf(w_xo)

    d_exp = w_down.shape[2]
    gate_job = lambda layer: _CastJob(w_gate_up, layer, d, d_exp, 0)
    up_job = lambda layer: _CastJob(w_gate_up, layer, d, d_exp, 1)
    down_job = lambda layer: _CastJob(w_down, layer, d_exp, d, 0)
    mixer_hosts_two = lambda layer: layer % N_MIXERS == 0

    fm_of_nat, nat_of_fm, row1, col1 = _fourier_row_orders(bsz, s)
    h = x
    fourier_in = None
    w_dn = None
    for i in range(depth):
        kind, j = i % N_MIXERS, i // N_MIXERS
        next_is_fourier = i + 1 < depth and (i + 1) % N_MIXERS == 1
        if kind == 0:
            u, w_gate = _a_in(h.reshape(t, d), g_mix, i, w_a_in, j, gate_job(i))
            h, w_up = _a_conv(u.reshape(bsz, s, d), h, w_a_conv, b_a_conv, g_a_ln, b_a_ln, w_a_out, j, up_job(i))
        elif kind == 1:
            x1, h_fm = fourier_in if fourier_in is not None else _fourier_layouts_from_natural(h)
            h, w_gate = _fourier_mixer(x1, h_fm, g_mix, i, w_b_out, j, gate_job(i))
        else:
            gate, cv = _c_in(h.reshape(t, d), g_mix, i, w_c_in, j)
            h, w_gate = _c_conv(cv.reshape(bsz, s, d), gate.reshape(bsz, s, d), h, w_c_conv, w_c_out, j, gate_job(i))
        k, v = _memory_kv(mem2d, g_mem, w_xk, w_xv, i)
        xattn_job = down_job(i) if mixer_hosts_two(i) else up_job(i)
        haug, counts, w_cast = _cross_attention_route(
            h, g_xattn, w_xq, k.reshape(bsz, -1, d), v.reshape(bsz, -1, d), w_xo, g_moe,
            w_route_group[i], b_route_group[i], w_route_expert[i], b_route_expert[i], i, xattn_job)
        if mixer_hosts_two(i):
            w_dn = w_cast
        else:
            w_up = w_cast
            if w_dn is None:
                w_dn = bf(w_down[i])
        rows_are_fm = kind == 1
        final = i == depth - 1
        next_cast = down_job(i + 1) if (not final and not mixer_hosts_two(i + 1)) else None
        if next_is_fourier:
            row_of = fm_of_nat[nat_of_fm] if rows_are_fm else fm_of_nat
            layout2 = (bsz * (s // SEQ_MINOR),
                       row1[nat_of_fm] if rows_are_fm else row1, col1[nat_of_fm] if rows_are_fm else col1)
        else:
            row_of = nat_of_fm if rows_are_fm else None
            layout2 = None
        out, out2, w_dn = _moe_experts(haug, counts, g_moe, w_gate, w_up, w_dn, i, g_final, final,
                                       row_of, layout2, next_cast)
        if next_is_fourier:
            fourier_in, h = (out2, out.reshape(bsz, s, d)), None
        else:
            fourier_in, h = None, out.reshape(bsz, s, d)
    return h
```

```python
import functools
import math
from typing import NamedTuple

import numpy as np
import jax
import jax.numpy as jnp
from jax import lax
from jax.experimental import pallas as pl
from jax.experimental.pallas import tpu as pltpu

F32 = jnp.float32
BF16 = jnp.bfloat16
EPS = 1e-6

N_MIXERS = 3
N_FOURIER_GROUPS = 8
N_HEADS = 4
N_GROUPS = 8
EXPERTS_PER_GROUP = 4
N_PAIRS = 6
N_BUCKETS = N_GROUPS * N_PAIRS
PAIR_LO = (0, 0, 0, 1, 1, 2)
PAIR_HI = (1, 2, 3, 2, 3, 3)

LANES = 128
SUBLANES = 8
SEQ_MINOR = 128
HALO = 16
VMEM_LIMIT_BYTES = 56 * 1024 * 1024

TM_DENSE = 512
TM_MOE = 256
ROW_CHUNK = 64
N_CHUNK = 512
CONV_ROWS = 128
CONV_COLS = 128


def _params(n_axes):
    return pltpu.CompilerParams(
        dimension_semantics=("arbitrary",) * n_axes,
        vmem_limit_bytes=VMEM_LIMIT_BYTES)


def _layer_block(shape, layer):
    zeros = (0,) * (len(shape) - 1)
    return pl.BlockSpec((1,) + tuple(shape[1:]), lambda *_: (layer,) + zeros, pipeline_mode=pl.Buffered(1))


def _rows3(a):
    return a.reshape(a.shape[0], 1, a.shape[1])


def _rms(x, g):
    return x * lax.rsqrt(jnp.mean(x * x, axis=-1, keepdims=True) + EPS) * g


def _sigmoid(x):
    return 1.0 / (1.0 + jnp.exp(-x))


def _dot(a, b):
    return jnp.dot(a, b, preferred_element_type=F32)


class _CastJob(NamedTuple):
    src: jax.Array
    layer: int
    rows: int
    cols: int
    col_block: int


def _cast_io(job, n_steps, step_of):
    n_experts = job.src.shape[1]
    per_step = -(-n_experts // n_steps)
    assert n_experts % per_step == 0
    last = n_experts // per_step - 1
    block = lambda *g: jnp.minimum(step_of(*g), last)
    in_spec = pl.BlockSpec((1, per_step, job.rows, job.cols), lambda *g: (job.layer, block(*g), 0, job.col_block))
    out_spec = pl.BlockSpec((per_step, job.rows, job.cols), lambda *g: (block(*g), 0, 0))
    return in_spec, out_spec, jax.ShapeDtypeStruct((n_experts, job.rows, job.cols), BF16)


def _cast_step(src_ref, dst_ref):
    dst_ref[...] = src_ref[0].astype(BF16)


def _a_in_kernel(x_ref, g_ref, w_ref, cast_src_ref, u_ref, cast_dst_ref, xn_ref):
    d = x_ref.shape[1]
    xn_ref[...] = _rms(x_ref[...], g_ref[0]).astype(BF16)
    for c in range(0, d, N_CHUNK):
        a = _dot(xn_ref[...], w_ref[0, :, c:c + N_CHUNK])
        gate = _dot(xn_ref[...], w_ref[0, :, d + c:d + c + N_CHUNK])
        u_ref[:, c:c + N_CHUNK] = (a * _sigmoid(gate)).astype(BF16)
    _cast_step(cast_src_ref, cast_dst_ref)


def _a_in(h, g_all, layer, w_all, j, cast):
    t, d = h.shape
    steps = t // TM_DENSE
    cast_in, cast_out, cast_shape = _cast_io(cast, steps, lambda i: i)
    return pl.pallas_call(
        _a_in_kernel,
        grid=(steps,),
        in_specs=[pl.BlockSpec((TM_DENSE, d), lambda i: (i, 0)),
                  _layer_block(g_all.shape, layer),
                  _layer_block(w_all.shape, j),
                  cast_in],
        out_specs=[pl.BlockSpec((TM_DENSE, d), lambda i: (i, 0)), cast_out],
        out_shape=[jax.ShapeDtypeStruct((t, d), BF16), cast_shape],
        scratch_shapes=[pltpu.VMEM((TM_DENSE, d), BF16)],
        compiler_params=_params(1),
        name="conformer_in",
    )(h, g_all, w_all, cast.src)


def _c_in_kernel(x_ref, g_ref, w_ref, b_ref, cv_ref, xn_ref):
    d = x_ref.shape[1]
    xn_ref[...] = _rms(x_ref[...], g_ref[0]).astype(BF16)
    for c in range(0, d, N_CHUNK):
        b_ref[:, c:c + N_CHUNK] = _dot(xn_ref[...], w_ref[0, :, c:c + N_CHUNK]).astype(BF16)
        cg = _dot(xn_ref[...], w_ref[0, :, d + c:d + c + N_CHUNK])
        v = _dot(xn_ref[...], w_ref[0, :, 2 * d + c:2 * d + c + N_CHUNK])
        cv_ref[:, c:c + N_CHUNK] = (cg * v).astype(BF16)


def _c_in(h, g_all, layer, w_all, j):
    t, d = h.shape
    tile = pl.BlockSpec((TM_DENSE, d), lambda i: (i, 0))
    return pl.pallas_call(
        _c_in_kernel,
        grid=(t // TM_DENSE,),
        in_specs=[tile, _layer_block(g_all.shape, layer), _layer_block(w_all.shape, j)],
        out_specs=[tile, tile],
        out_shape=[jax.ShapeDtypeStruct((t, d), BF16)] * 2,
        scratch_shapes=[pltpu.VMEM((TM_DENSE, d), BF16)],
        compiler_params=_params(1),
        name="shortconv_in",
    )(h, g_all, w_all)


def _fill_ext(ext_ref, cur_ref, prev_ref, next_ref, tm):
    i = pl.program_id(1)
    n = pl.num_programs(1)
    prev = prev_ref[0].astype(F32)
    nxt = next_ref[0].astype(F32)
    ext_ref[0:HALO, :] = jnp.where(i > 0, prev, jnp.zeros_like(prev))
    ext_ref[HALO + tm:HALO + tm + HALO, :] = jnp.where(i < n - 1, nxt, jnp.zeros_like(nxt))

    def body(j, carry):
        r0 = pl.multiple_of(j * ROW_CHUNK, ROW_CHUNK)
        ext_ref[pl.ds(HALO + r0, ROW_CHUNK), :] = cur_ref[0, pl.ds(r0, ROW_CHUNK), :].astype(F32)
        return carry

    lax.fori_loop(0, tm // ROW_CHUNK, body, 0)


def _depthwise(ext_ref, w_ref, cv_ref, width, tm):
    d = cv_ref.shape[1]
    rows = CONV_ROWS
    span = rows + 2 * HALO
    off = HALO - width // 2

    def body(j, carry):
        r0 = pl.multiple_of(j * rows, rows)
        for c in range(0, d, CONV_COLS):
            blk = ext_ref[pl.ds(r0, span), c:c + CONV_COLS]
            acc = jnp.zeros((rows, CONV_COLS), F32)
            for r in range(SUBLANES):
                taps = [k for k in range(width) if (off + k) % SUBLANES == r]
                if not taps:
                    continue
                shifted = blk if r == 0 else pltpu.roll(blk, span - r, axis=0)
                for k in taps:
                    q = (off + k) // SUBLANES * SUBLANES
                    acc = acc + shifted[q:q + rows, :] * w_ref[0, k:k + 1, c:c + CONV_COLS]
            cv_ref[pl.ds(r0, rows), c:c + CONV_COLS] = acc
        return carry

    lax.fori_loop(0, tm // rows, body, 0)


def _project_residual(out_ref, h_ref, v_ref, w_ref):
    d = v_ref.shape[1]
    for c in range(0, d, N_CHUNK):
        out_ref[0, :, c:c + N_CHUNK] = h_ref[0, :, c:c + N_CHUNK] + _dot(v_ref[...], w_ref[0, :, c:c + N_CHUNK])


def _a_conv_kernel(cur_ref, prev_ref, next_ref, h_ref, wc_ref, bc_ref, gl_ref, bl_ref, wo_ref, cast_src_ref,
                   out_ref, cast_dst_ref, ext_ref, cv_ref, v_ref, *, width):
    tm = cur_ref.shape[1]
    _fill_ext(ext_ref, cur_ref, prev_ref, next_ref, tm)
    _depthwise(ext_ref, wc_ref, cv_ref, width, tm)

    def body(j, carry):
        r0 = pl.multiple_of(j * ROW_CHUNK, ROW_CHUNK)
        u = cv_ref[pl.ds(r0, ROW_CHUNK), :] + bc_ref[0]
        mu = jnp.mean(u, axis=-1, keepdims=True)
        uc = u - mu
        var = jnp.mean(uc * uc, axis=-1, keepdims=True)
        y = uc * lax.rsqrt(var + EPS) * gl_ref[0] + bl_ref[0]
        v_ref[pl.ds(r0, ROW_CHUNK), :] = (y * _sigmoid(y)).astype(BF16)
        return carry

    lax.fori_loop(0, tm // ROW_CHUNK, body, 0)
    _project_residual(out_ref, h_ref, v_ref, wo_ref)
    _cast_step(cast_src_ref, cast_dst_ref)


def _c_conv_kernel(cur_ref, prev_ref, next_ref, gate_ref, h_ref, wc_ref, wo_ref, cast_src_ref,
                   out_ref, cast_dst_ref, ext_ref, cv_ref, v_ref, *, width):
    tm = cur_ref.shape[1]
    _fill_ext(ext_ref, cur_ref, prev_ref, next_ref, tm)
    _depthwise(ext_ref, wc_ref, cv_ref, width, tm)

    def body(j, carry):
        r0 = pl.multiple_of(j * ROW_CHUNK, ROW_CHUNK)
        gate = gate_ref[0, pl.ds(r0, ROW_CHUNK), :].astype(F32)
        v_ref[pl.ds(r0, ROW_CHUNK), :] = (gate * cv_ref[pl.ds(r0, ROW_CHUNK), :]).astype(BF16)
        return carry

    lax.fori_loop(0, tm // ROW_CHUNK, body, 0)
    _project_residual(out_ref, h_ref, v_ref, wo_ref)
    _cast_step(cast_src_ref, cast_dst_ref)


def _conv_specs(s, d, tm):
    per = tm // HALO
    last = s // HALO - 1
    cur = pl.BlockSpec((1, tm, d), lambda b, i: (b, i, 0))
    prev = pl.BlockSpec((1, HALO, d), lambda b, i: (b, jnp.maximum(i * per - 1, 0), 0))
    nxt = pl.BlockSpec((1, HALO, d), lambda b, i: (b, jnp.minimum((i + 1) * per, last), 0))
    return cur, prev, nxt


def _conv_scratch(tm, d):
    return [pltpu.VMEM((tm + 2 * HALO, d), F32), pltpu.VMEM((tm, d), F32), pltpu.VMEM((tm, d), BF16)]


def _a_conv(u, h, w_conv, b_conv, g_ln, b_ln, w_out, j, cast):
    bsz, s, d = h.shape
    tm = TM_DENSE
    per = s // tm
    cur, prev, nxt = _conv_specs(s, d, tm)
    cast_in, cast_out, cast_shape = _cast_io(cast, bsz * per, lambda b, i: b * per + i)
    return pl.pallas_call(
        functools.partial(_a_conv_kernel, width=w_conv.shape[1]),
        grid=(bsz, per),
        in_specs=[cur, prev, nxt, cur, _layer_block(w_conv.shape, j), _layer_block(b_conv.shape, j),
                  _layer_block(g_ln.shape, j), _layer_block(b_ln.shape, j), _layer_block(w_out.shape, j), cast_in],
        out_specs=[cur, cast_out],
        out_shape=[jax.ShapeDtypeStruct((bsz, s, d), F32), cast_shape],
        scratch_shapes=_conv_scratch(tm, d),
        compiler_params=_params(2),
        name="conformer_conv_out",
    )(u, u, u, h, w_conv, b_conv, g_ln, b_ln, w_out, cast.src)


def _c_conv(cv, gate, h, w_conv, w_out, j, cast):
    bsz, s, d = h.shape
    tm = TM_DENSE
    per = s // tm
    cur, prev, nxt = _conv_specs(s, d, tm)
    cast_in, cast_out, cast_shape = _cast_io(cast, bsz * per, lambda b, i: b * per + i)
    return pl.pallas_call(
        functools.partial(_c_conv_kernel, width=w_conv.shape[1]),
        grid=(bsz, per),
        in_specs=[cur, prev, nxt, cur, cur, _layer_block(w_conv.shape, j), _layer_block(w_out.shape, j), cast_in],
        out_specs=[cur, cast_out],
        out_shape=[jax.ShapeDtypeStruct((bsz, s, d), F32), cast_shape],
        scratch_shapes=_conv_scratch(tm, d),
        compiler_params=_params(2),
        name="shortconv_conv_out",
    )(cv, cv, cv, gate, h, w_conv, w_out, cast.src)


def _dft_tables(s, group):
    s1 = s // SEQ_MINOR

    def cs(n_rows, n_cols, period):
        m = (np.outer(np.arange(n_rows), np.arange(n_cols)) % period).astype(np.float64)
        ang = 2.0 * np.pi * m / period
        return np.cos(ang), -np.sin(ang)

    c1, i1 = cs(s1, s1, s1)
    w1 = np.concatenate([c1, i1], axis=0)
    twr, twi = cs(s1, SEQ_MINOR, s)
    c2, i2 = cs(SEQ_MINOR, SEQ_MINOR, SEQ_MINOR)
    w2 = np.block([[c2, -i2], [i2, c2]])
    cc, ic = cs(group, group, group)
    scale = 1.0 / math.sqrt(float(s) * float(group))
    return (jnp.asarray(w1, BF16), jnp.asarray(twr, F32), jnp.asarray(twi, F32),
            jnp.asarray(w2, BF16), jnp.asarray(cc * scale, BF16), jnp.asarray(-ic * scale, BF16))


def _b_stage1_kernel(x_ref, g_ref, w1_ref, twr_ref, twi_ref, br_ref, bi_ref, *, nb):
    s1 = x_ref.shape[1]
    d = g_ref.shape[2]
    brs, bis = [], []
    for n in range(nb):
        xn = _rms(x_ref[0, :, n * d:(n + 1) * d], g_ref[0]).astype(BF16)
        y = _dot(w1_ref[...], xn)
        yr, yi = y[:s1], y[s1:]
        tr = twr_ref[0, :, n:n + 1]
        ti = twi_ref[0, :, n:n + 1]
        brs.append(yr * tr - yi * ti)
        bis.append(yr * ti + yi * tr)
    br_ref[0] = jnp.swapaxes(jnp.stack(brs, axis=0), 0, 1).astype(BF16)
    bi_ref[0] = jnp.swapaxes(jnp.stack(bis, axis=0), 0, 1).astype(BF16)


def _b_stage2_kernel(br_ref, bi_ref, h_ref, w2_ref, cc_ref, sc_ref, wo_ref, cast_src_ref, out_ref, cast_dst_ref,
                     ar_ref, ai_ref, f_ref, *, ka, group):
    d = br_ref.shape[2]
    m = SEQ_MINOR
    for n in range(ka):
        cat = jnp.concatenate([br_ref[0, n * m:(n + 1) * m, :], bi_ref[0, n * m:(n + 1) * m, :]], axis=0)
        a = _dot(w2_ref[...], cat)
        ar_ref[n * m:(n + 1) * m, :] = a[:m].astype(BF16)
        ai_ref[n * m:(n + 1) * m, :] = a[m:].astype(BF16)
    for c in range(0, d, group):
        f = _dot(ar_ref[:, c:c + group], cc_ref[...]) + _dot(ai_ref[:, c:c + group], sc_ref[...])
        f_ref[:, c:c + group] = f.astype(BF16)
    for c in range(0, d, N_CHUNK):
        out_ref[0, :, c:c + N_CHUNK] = h_ref[0, :, c:c + N_CHUNK] + _dot(f_ref[...], wo_ref[0, :, c:c + N_CHUNK])
    _cast_step(cast_src_ref, cast_dst_ref)


def _const_block(shape):
    nd = len(shape)
    return pl.BlockSpec(shape, lambda *_: (0,) * nd, pipeline_mode=pl.Buffered(1))


def _fourier_row_orders(bsz, s):
    s1 = s // SEQ_MINOR
    k = np.arange(s)
    base = (np.arange(bsz) * s)[:, None]
    fm_of_nat = (base + ((k % s1) * SEQ_MINOR + k // s1)[None, :]).reshape(-1)
    nat_of_fm = np.argsort(fm_of_nat)
    row1 = ((np.arange(bsz) * s1)[:, None] + (k // SEQ_MINOR)[None, :]).reshape(-1)
    col1 = np.tile(k % SEQ_MINOR, bsz)
    as_i32 = lambda a: jnp.asarray(a, jnp.int32)
    return as_i32(fm_of_nat), as_i32(nat_of_fm), as_i32(row1), as_i32(col1)


def _fourier_layouts_from_natural(h):
    bsz, s, d = h.shape
    s1 = s // SEQ_MINOR
    x1 = h.reshape(bsz * s1, SEQ_MINOR * d)
    h_fm = h.reshape(bsz, SEQ_MINOR, s1, d).transpose(0, 2, 1, 3).reshape(bsz, s, d)
    return x1, h_fm


def _fourier_mixer(x1, h_fm, g_all, layer, w_out, j, cast):
    bsz, s, d = h_fm.shape
    s1 = s // SEQ_MINOR
    group = d // N_FOURIER_GROUPS
    nb = 16
    ka = 4
    w1, twr, twi, w2, cc, sc = _dft_tables(s, group)
    steps1 = SEQ_MINOR // nb
    twr = twr.reshape(s1, steps1, nb).transpose(1, 0, 2)
    twi = twi.reshape(s1, steps1, nb).transpose(1, 0, 2)

    x1 = x1.reshape(bsz, s1, SEQ_MINOR * d)
    blk1 = pl.BlockSpec((1, s1, nb * d), lambda b, i: (b, 0, i))
    tw_spec = pl.BlockSpec((1, s1, nb), lambda b, i: (i, 0, 0))
    br, bi = pl.pallas_call(
        functools.partial(_b_stage1_kernel, nb=nb),
        grid=(bsz, steps1),
        in_specs=[blk1, _layer_block(g_all.shape, layer), _const_block((2 * s1, s1)), tw_spec, tw_spec],
        out_specs=[pl.BlockSpec((1, s1, nb, d), lambda b, i: (b, 0, i, 0))] * 2,
        out_shape=[jax.ShapeDtypeStruct((bsz, s1, SEQ_MINOR, d), BF16)] * 2,
        compiler_params=_params(2),
        name="fourier_stage1",
    )(x1, g_all, w1, twr, twi)

    br = br.reshape(bsz, s, d)
    bi = bi.reshape(bsz, s, d)
    rows = pl.BlockSpec((1, ka * SEQ_MINOR, d), lambda b, i: (b, i, 0))
    per = s1 // ka
    cast_in, cast_out, cast_shape = _cast_io(cast, bsz * per, lambda b, i: b * per + i)
    return pl.pallas_call(
        functools.partial(_b_stage2_kernel, ka=ka, group=group),
        grid=(bsz, per),
        in_specs=[rows, rows, rows, _const_block((2 * SEQ_MINOR, 2 * SEQ_MINOR)),
                  _const_block((group, group)), _const_block((group, group)), _layer_block(w_out.shape, j), cast_in],
        out_specs=[rows, cast_out],
        out_shape=[jax.ShapeDtypeStruct((bsz, s, d), F32), cast_shape],
        scratch_shapes=[pltpu.VMEM((ka * SEQ_MINOR, d), BF16)] * 3,
        compiler_params=_params(2),
        name="fourier_stage2_out",
    )(br, bi, h_fm, w2, cc, sc, w_out, cast.src)


def _kv_kernel(mem_ref, g_ref, wk_ref, wv_ref, k_ref, v_ref):
    memn = _rms(mem_ref[...], g_ref[0]).astype(BF16)
    k_ref[...] = _dot(memn, wk_ref[0].astype(BF16)).astype(BF16)
    v_ref[...] = _dot(memn, wv_ref[0].astype(BF16)).astype(BF16)


def _memory_kv(mem2d, g_all, wk, wv, layer):
    r, d = mem2d.shape
    col = pl.BlockSpec((1, d, N_CHUNK), lambda c: (layer, 0, c))
    out = pl.BlockSpec((r, N_CHUNK), lambda c: (0, c))
    return pl.pallas_call(
        _kv_kernel,
        grid=(d // N_CHUNK,),
        in_specs=[_const_block((r, d)), _layer_block(g_all.shape, layer), col, col],
        out_specs=[out, out],
        out_shape=[jax.ShapeDtypeStruct((r, d), BF16)] * 2,
        compiler_params=_params(1),
        name="memory_kv",
    )(mem2d, g_all, wk, wv)


def _split_bf16(a):
    hi = a.astype(BF16)
    lo = (a - hi.astype(F32)).astype(BF16)
    return hi, lo


def _route(xn, whl, bias):
    xh, xl = _split_bf16(xn)
    both = _dot(xh, whl)
    logits = both[:, :LANES] + both[:, LANES:] + _dot(xl, whl[:, :LANES]) + bias
    lane = lax.broadcasted_iota(jnp.int32, logits.shape, 1)
    neg = jnp.float32(-jnp.inf)
    big = jnp.int32(LANES)

    gl = jnp.where(lane < N_GROUPS, logits, neg)
    gmax = jnp.max(gl, axis=-1, keepdims=True)
    grp = jnp.min(jnp.where(gl == gmax, lane, big), axis=-1, keepdims=True)
    g_w = 1.0 / jnp.sum(jnp.exp(gl - gmax), axis=-1, keepdims=True)

    first = N_GROUPS + grp * EXPERTS_PER_GROUP
    el = jnp.where((lane >= first) & (lane < first + EXPERTS_PER_GROUP), logits, neg)
    m1 = jnp.max(el, axis=-1, keepdims=True)
    i1 = jnp.min(jnp.where(el == m1, lane, big), axis=-1, keepdims=True)
    el2 = jnp.where(lane == i1, neg, el)
    m2 = jnp.max(el2, axis=-1, keepdims=True)
    i2 = jnp.min(jnp.where(el2 == m2, lane, big), axis=-1, keepdims=True)
    e21 = jnp.exp(m2 - m1)
    w1 = g_w / (1.0 + e21)
    w2 = g_w * e21 / (1.0 + e21)

    a1 = i1 - first
    a2 = i2 - first
    lo = jnp.minimum(a1, a2)
    hi = jnp.maximum(a1, a2)
    pair = lo * 3 - jnp.right_shift(lo * (lo - 1), 1) + hi - lo - 1
    bucket = grp * N_PAIRS + pair
    w_lo = jnp.where(a1 < a2, w1, w2)
    w_hi = jnp.where(a1 < a2, w2, w1)
    routed = jnp.where(lane == 0, bucket.astype(F32),
                       jnp.where(lane == 1, w_lo, jnp.where(lane == 2, w_hi, 0.0)))
    return routed, (lane == bucket).astype(F32)


def _xattn_kernel(h_ref, g_ref, wq_ref, k_ref, v_ref, wo_ref, gm_ref, rw_ref, rb_ref, cast_src_ref,
                  out_ref, cnt_ref, cast_dst_ref, xn_ref, o_ref):
    d = h_ref.shape[2]
    hd = d // N_HEADS
    scale = float(hd) ** -0.5
    xn_ref[...] = _rms(h_ref[0], g_ref[0]).astype(BF16)
    for c in range(0, d, hd):
        q = _dot(xn_ref[...], wq_ref[0, :, c:c + hd]).astype(BF16)
        s = lax.dot_general(q, k_ref[0, :, c:c + hd], (((1,), (1,)), ((), ())),
                            preferred_element_type=F32) * scale
        e = jnp.exp(s - jnp.max(s, axis=-1, keepdims=True))
        p = (e / jnp.sum(e, axis=-1, keepdims=True)).astype(BF16)
        o_ref[:, c:c + hd] = _dot(p, v_ref[0, :, c:c + hd]).astype(BF16)
    for c in range(0, d, N_CHUNK):
        out_ref[:, c:c + N_CHUNK] = h_ref[0, :, c:c + N_CHUNK] + _dot(o_ref[...], wo_ref[0, :, c:c + N_CHUNK])

    routed, onehot = _route(_rms(out_ref[:, 0:d], gm_ref[0]), rw_ref[...], rb_ref[...])
    out_ref[:, d:d + LANES] = routed

    @pl.when((pl.program_id(0) == 0) & (pl.program_id(1) == 0))
    def _():
        cnt_ref[...] = jnp.zeros_like(cnt_ref)

    cnt_ref[...] += jnp.sum(onehot, axis=0, keepdims=True)
    _cast_step(cast_src_ref, cast_dst_ref)


def _cross_attention_route(h, g_all, wq, k, v, wo, g_moe_all, w_rg, b_rg, w_re, b_re, layer, cast):
    bsz, s, d = h.shape
    n_mem = k.shape[1]
    tm = TM_DENSE
    per = s // tm
    n = w_rg.shape[1] + w_re.shape[1]
    w = jnp.zeros((d, LANES), F32).at[:, :n].set(jnp.concatenate([w_rg, w_re], axis=1))
    b = jnp.zeros((1, LANES), F32).at[0, :n].set(jnp.concatenate([b_rg, b_re]))
    whl = jnp.concatenate(_split_bf16(w), axis=1)
    cast_in, cast_out, cast_shape = _cast_io(cast, bsz * per, lambda bb, i: bb * per + i)
    tile = pl.BlockSpec((1, tm, d), lambda bb, i: (bb, i, 0))
    kv = pl.BlockSpec((1, n_mem, d), lambda bb, i: (bb, 0, 0))
    return pl.pallas_call(
        _xattn_kernel,
        grid=(bsz, per),
        in_specs=[tile, _layer_block(g_all.shape, layer), _layer_block(wq.shape, layer), kv, kv,
                  _layer_block(wo.shape, layer), _layer_block(g_moe_all.shape, layer),
                  _const_block((d, 2 * LANES)), _const_block((1, LANES)), cast_in],
        out_specs=[pl.BlockSpec((tm, d + LANES), lambda bb, i: (bb * per + i, 0)),
                   pl.BlockSpec((1, LANES), lambda bb, i: (0, 0)), cast_out],
        out_shape=[jax.ShapeDtypeStruct((bsz * s, d + LANES), F32),
                   jax.ShapeDtypeStruct((1, LANES), F32), cast_shape],
        scratch_shapes=[pltpu.VMEM((tm, d), BF16)] * 2,
        compiler_params=_params(2),
        name="cross_attention_route",
    )(h, g_all, wq, k, v, wo, g_moe_all, whl, b, cast.src)


def _wait_rows(count, block_copy, row_copy):
    n8 = pl.multiple_of(jnp.right_shift(count, 3) * SUBLANES, SUBLANES)

    @pl.when(n8 > 0)
    def _():
        block_copy(n8).wait()

    def one(r, carry):
        row_copy().wait()
        return carry

    lax.fori_loop(0, count - n8, one, 0)


def _expert_kernel(order_ref, dest_ref, row2_ref, col2_ref, c0_ref, nv_ref, ea_ref, eb_ref,
                   haug_ref, g_ref, ga_ref, gb_ref, ua_ref, ub_ref, da_ref, db_ref, gf_ref, *rest,
                   final_norm, dual, hosts_cast):
    if hosts_cast:
        cast_src_ref, out_ref, out2_ref, cast_dst_ref, xbuf, obuf, gsem, ssem = rest
        _cast_step(cast_src_ref, cast_dst_ref)
    else:
        out_ref, out2_ref, xbuf, obuf, gsem, ssem = rest
    i = pl.program_id(0)
    n = pl.num_programs(0)
    parity = lax.rem(i, 2)
    tm = xbuf.shape[1]
    d = obuf.shape[2]
    copies_per_row = 2 if dual else 1

    nv = nv_ref[i]
    nxt = jnp.minimum(i + 1, n - 1)
    prev = jnp.maximum(i - 1, 0)
    nv_next = jnp.where(i + 1 < n, nv_ref[nxt], 0)
    nv_prev = jnp.where(i >= 1, nv_ref[prev], 0)
    nv_prev2 = jnp.where(i >= 2, nv_ref[jnp.maximum(i - 2, 0)], 0)

    def gather_row(tok, s, r):
        return pltpu.make_async_copy(haug_ref.at[pl.ds(tok, 1), :], xbuf.at[s, pl.ds(r, 1), :], gsem.at[s])

    def scatter_row(row, s, r):
        return pltpu.make_async_copy(obuf.at[s, pl.ds(r, 1), :], out_ref.at[pl.ds(row, 1), :], ssem.at[s])

    def scatter_row2(row, col, s, r):
        lanes = pl.ds(pl.multiple_of(col * d, d), d)
        return pltpu.make_async_copy(obuf.at[s, pl.ds(r, 1), :], out2_ref.at[pl.ds(row, 1), lanes], ssem.at[s])

    def start_scatter_row(p, s, r):
        scatter_row(dest_ref[p], s, r).start()
        if dual:
            scatter_row2(row2_ref[p], col2_ref[p], s, r).start()

    def start_rows(count, base, s, start_row, exact):
        @pl.when(count == tm)
        def _():
            for r in range(tm):
                start_row(base + r, s, r)

        @pl.when((count > 0) & (count < tm))
        def _():
            groups = jnp.right_shift(count if exact else count + (SUBLANES - 1), 3)
            for gi in range(tm // SUBLANES):
                @pl.when(gi < groups)
                def _():
                    for r in range(gi * SUBLANES, (gi + 1) * SUBLANES):
                        start_row(base + r, s, r)
            if exact:
                def one(r, carry):
                    start_row(base + r, s, r)
                    return carry

                lax.fori_loop(groups * SUBLANES, count, one, 0)

    def start_gather(count, base, s):
        start_rows(count, base, s, lambda p, s_, r: gather_row(order_ref[p], s_, r).start(), exact=False)

    def start_scatter(count, base, s):
        start_rows(count, base, s, start_scatter_row, exact=True)

    def wait_gather(count, s):
        m = pl.multiple_of(jnp.right_shift(count + (SUBLANES - 1), 3) * SUBLANES, SUBLANES)

        @pl.when(m > 0)
        def _():
            pltpu.make_async_copy(haug_ref.at[pl.ds(0, m), :], xbuf.at[s, pl.ds(0, m), :], gsem.at[s]).wait()

    def wait_scatter(count, s):
        for _ in range(copies_per_row):
            _wait_rows(count,
                       lambda m: pltpu.make_async_copy(obuf.at[s, pl.ds(0, m), :], out_ref.at[pl.ds(0, m), :],
                                                       ssem.at[s]),
                       lambda: scatter_row(0, s, 0))

    @pl.when(i == 0)
    def _():
        xbuf[...] = jnp.zeros_like(xbuf)
        start_gather(nv, c0_ref[0], 0)
        if not dual:
            out2_ref[...] = jnp.zeros_like(out2_ref)

    for slot in range(2):
        @pl.when(parity == slot)
        def _():
            start_gather(nv_next, c0_ref[nxt], 1 - slot)
            start_scatter(nv_prev, c0_ref[prev], 1 - slot)
            wait_gather(nv, slot)

    def compute_tile(rows):
        x = xbuf[parity, 0:rows, 0:d]
        xn = _rms(x, g_ref[0]).astype(BF16)
        acc = x
        for lane, (gate_ref, up_ref, dn_ref) in enumerate(((ga_ref, ua_ref, da_ref), (gb_ref, ub_ref, db_ref))):
            gate = _dot(xn, gate_ref[0])
            up = _dot(xn, up_ref[0])
            act = (gate * _sigmoid(gate) * up).astype(BF16)
            acc = acc + _dot(act, dn_ref[0]) * xbuf[parity, 0:rows, d + 1 + lane:d + 2 + lane]
        if final_norm:
            acc = _rms(acc, gf_ref[...])
        wait_scatter(nv_prev2, parity)
        obuf[parity, 0:rows] = acc

    half = tm // 2

    @pl.when(nv > half)
    def _():
        compute_tile(tm)

    @pl.when((nv > 0) & (nv <= half))
    def _():
        compute_tile(half)

    @pl.when(nv == 0)
    def _():
        wait_scatter(nv_prev2, parity)

    @pl.when(i == n - 1)
    def _():
        wait_scatter(nv_prev, 1 - parity)


def _moe_experts(haug, counts, g_all, w_gate, w_up, w_dn, layer, g_final, final_norm,
                 row_of=None, layout2=None, cast=None):
    t, da = haug.shape
    d = da - LANES
    f = w_dn.shape[1]
    tm = TM_MOE
    n_tiles = (t + N_BUCKETS * (tm - 1)) // tm + 1
    dual = layout2 is not None

    bucket = haug[:, d].astype(jnp.int32)
    order = jnp.argsort(bucket, stable=True).astype(jnp.int32)
    pad = lambda a: jnp.concatenate([a, jnp.zeros((tm,), jnp.int32)])
    dest = order if row_of is None else row_of[order]
    row2 = layout2[1][order] if dual else order
    col2 = layout2[2][order] if dual else order
    sizes = counts[0, :N_BUCKETS].astype(jnp.int32)
    tiles_per = (sizes + tm - 1) // tm
    tile_end = jnp.cumsum(tiles_per)
    tile_start = tile_end - tiles_per
    n_used = tile_end[-1]
    start_sorted = jnp.cumsum(sizes) - sizes
    ids = jnp.arange(n_tiles, dtype=jnp.int32)
    tb = jnp.sum((tile_end[None, :] <= jnp.minimum(ids, n_used - 1)[:, None]).astype(jnp.int32), axis=1)
    k = ids - tile_start[tb]
    nv = jnp.where(ids < n_used, jnp.clip(sizes[tb] - k * tm, 0, tm), 0).astype(jnp.int32)
    c0 = jnp.where(ids < n_used, start_sorted[tb] + k * tm, 0).astype(jnp.int32)
    grp = tb // N_PAIRS
    pair = tb % N_PAIRS
    tile_a = (grp * EXPERTS_PER_GROUP + jnp.asarray(PAIR_LO, jnp.int32)[pair]).astype(jnp.int32)
    tile_b = (grp * EXPERTS_PER_GROUP + jnp.asarray(PAIR_HI, jnp.int32)[pair]).astype(jnp.int32)

    any_space = pl.BlockSpec(memory_space=pl.ANY)
    expert_a = lambda i, o, ds, r2, c2, c, v, ea, eb: (ea[i], 0, 0)
    expert_b = lambda i, o, ds, r2, c2, c, v, ea, eb: (eb[i], 0, 0)
    in_specs = [any_space,
                pl.BlockSpec((1, 1, d), lambda i, *_: (layer, 0, 0)),
                pl.BlockSpec((1, d, f), expert_a), pl.BlockSpec((1, d, f), expert_b),
                pl.BlockSpec((1, d, f), expert_a), pl.BlockSpec((1, d, f), expert_b),
                pl.BlockSpec((1, f, d), expert_a), pl.BlockSpec((1, f, d), expert_b),
                pl.BlockSpec((1, d), lambda i, *_: (0, 0))]
    shape2 = (layout2[0], SEQ_MINOR * d) if dual else (SUBLANES, LANES)
    out_specs = [any_space, any_space if dual else pl.BlockSpec((SUBLANES, LANES), lambda i, *_: (0, 0))]
    out_shape = [jax.ShapeDtypeStruct((t, d), F32), jax.ShapeDtypeStruct(shape2, F32)]
    operands = [haug, g_all, w_gate, w_gate, w_up, w_up, w_dn, w_dn, g_final.reshape(1, d)]
    if cast is not None:
        cast_in, cast_out, cast_shape = _cast_io(cast, n_tiles, lambda i, *_: i)
        in_specs.append(cast_in)
        out_specs.append(cast_out)
        out_shape.append(cast_shape)
        operands.append(cast.src)
    grid_spec = pltpu.PrefetchScalarGridSpec(
        num_scalar_prefetch=8,
        grid=(n_tiles,),
        in_specs=in_specs,
        out_specs=out_specs,
        scratch_shapes=[pltpu.VMEM((2, tm, da), F32), pltpu.VMEM((2, tm, d), F32),
                        pltpu.SemaphoreType.DMA((2,)), pltpu.SemaphoreType.DMA((2,))],
    )
    outs = pl.pallas_call(
        functools.partial(_expert_kernel, final_norm=final_norm, dual=dual, hosts_cast=cast is not None),
        grid_spec=grid_spec,
        out_shape=out_shape,
        compiler_params=_params(1),
        name="moe_experts",
    )(pad(order), pad(dest), pad(row2), pad(col2), c0, nv, tile_a, tile_b, *operands)
    return outs[0], (outs[1] if dual else None), (outs[2] if cast is not None else None)


def kernel(x, mem, g_mix, g_xattn, g_mem, g_moe, g_final, w_a_in, w_a_conv, b_a_conv, g_a_ln, b_a_ln, w_a_out, w_b_out, w_c_in, w_c_conv, w_c_out, w_xq, w_xk, w_xv, w_xo, w_route_group, b_route_group, w_route_expert, b_route_expert, w_gate_up, w_down):
    bsz, s, d = x.shape
    t = bsz * s
    depth = g_mix.shape[0]
    mem2d = mem.reshape(bsz * mem.shape[1], d)
    bf = lambda w: w.astype(BF16)
    g_mix, g_xattn, g_mem, g_moe = _rows3(g_mix), _rows3(g_xattn), _rows3(g_mem), _rows3(g_moe)
    b_a_conv, g_a_ln, b_a_ln = _rows3(b_a_conv), _rows3(g_a_ln), _rows3(b_a_ln)
    w_a_in, w_a_out, w_b_out, w_c_in, w_c_out = bf(w_a_in), bf(w_a_out), bf(w_b_out), bf(w_c_in), bf(w_c_out)
    w_xq, w_xo = bf(w_xq), bf(w_xo)

    d_exp = w_down.shape[2]
    gate_job = lambda layer: _CastJob(w_gate_up, layer, d, d_exp, 0)
    up_job = lambda layer: _CastJob(w_gate_up, layer, d, d_exp, 1)
    down_job = lambda layer: _CastJob(w_down, layer, d_exp, d, 0)
    mixer_hosts_two = lambda layer: layer % N_MIXERS == 0

    fm_of_nat, nat_of_fm, row1, col1 = _fourier_row_orders(bsz, s)
    h = x
    fourier_in = None
    w_dn = None
    for i in range(depth):
        kind, j = i % N_MIXERS, i // N_MIXERS
        next_is_fourier = i + 1 < depth and (i + 1) % N_MIXERS == 1
        if kind == 0:
            u, w_gate = _a_in(h.reshape(t, d), g_mix, i, w_a_in, j, gate_job(i))
            h, w_up = _a_conv(u.reshape(bsz, s, d), h, w_a_conv, b_a_conv, g_a_ln, b_a_ln, w_a_out, j, up_job(i))
        elif kind == 1:
            x1, h_fm = fourier_in if fourier_in is not None else _fourier_layouts_from_natural(h)
            h, w_gate = _fourier_mixer(x1, h_fm, g_mix, i, w_b_out, j, gate_job(i))
        else:
            gate, cv = _c_in(h.reshape(t, d), g_mix, i, w_c_in, j)
            h, w_gate = _c_conv(cv.reshape(bsz, s, d), gate.reshape(bsz, s, d), h, w_c_conv, w_c_out, j, gate_job(i))
        k, v = _memory_kv(mem2d, g_mem, w_xk, w_xv, i)
        xattn_job = down_job(i) if mixer_hosts_two(i) else up_job(i)
        haug, counts, w_cast = _cross_attention_route(
            h, g_xattn, w_xq, k.reshape(bsz, -1, d), v.reshape(bsz, -1, d), w_xo, g_moe,
            w_route_group[i], b_route_group[i], w_route_expert[i], b_route_expert[i], i, xattn_job)
        if mixer_hosts_two(i):
            w_dn = w_cast
        else:
            w_up = w_cast
            if w_dn is None:
                w_dn = bf(w_down[i])
        rows_are_fm = kind == 1
        final = i == depth - 1
        next_cast = down_job(i + 1) if (not final and not mixer_hosts_two(i + 1)) else None
        if next_is_fourier:
            row_of = fm_of_nat[nat_of_fm] if rows_are_fm else fm_of_nat
            layout2 = (bsz * (s // SEQ_MINOR),
                       row1[nat_of_fm] if rows_are_fm else row1, col1[nat_of_fm] if rows_are_fm else col1)
        else:
            row_of = nat_of_fm if rows_are_fm else None
            layout2 = None
        out, out2, w_dn = _moe_experts(haug, counts, g_moe, w_gate, w_up, w_dn, i, g_final, final,
                                       row_of, layout2, next_cast)
        if next_is_fourier:
            fourier_in, h = (out2, out.reshape(bsz, s, d)), None
        else:
            fourier_in, h = None, out.reshape(bsz, s, d)
    return h
```

```python
import functools
import math
from typing import NamedTuple

import numpy as np
import jax
import jax.numpy as jnp
from jax import lax
from jax.experimental import pallas as pl
from jax.experimental.pallas import tpu as pltpu

F32 = jnp.float32
BF16 = jnp.bfloat16
EPS = 1e-6

N_MIXERS = 3
N_FOURIER_GROUPS = 8
N_HEADS = 4
N_GROUPS = 8
EXPERTS_PER_GROUP = 4
N_PAIRS = 6
N_BUCKETS = N_GROUPS * N_PAIRS
PAIR_LO = (0, 0, 0, 1, 1, 2)
PAIR_HI = (1, 2, 3, 2, 3, 3)

LANES = 128
SUBLANES = 8
SEQ_MINOR = 128
HALO = 16
VMEM_LIMIT_BYTES = 56 * 1024 * 1024

TM_DENSE = 512
TM_MOE = 256
ROW_CHUNK = 64
N_CHUNK = 512
CONV_ROWS = 128
CONV_COLS = 128


def _params(n_axes):
    return pltpu.CompilerParams(
        dimension_semantics=("arbitrary",) * n_axes,
        vmem_limit_bytes=VMEM_LIMIT_BYTES)


def _layer_block(shape, layer):
    zeros = (0,) * (len(shape) - 1)
    return pl.BlockSpec((1,) + tuple(shape[1:]), lambda *_: (layer,) + zeros, pipeline_mode=pl.Buffered(1))


def _rows3(a):
    return a.reshape(a.shape[0], 1, a.shape[1])


def _rms(x, g):
    return x * lax.rsqrt(jnp.mean(x * x, axis=-1, keepdims=True) + EPS) * g


def _sigmoid(x):
    return 1.0 / (1.0 + jnp.exp(-x))


def _dot(a, b):
    return jnp.dot(a, b, preferred_element_type=F32)


class _CastJob(NamedTuple):
    src: jax.Array
    layer: int
    rows: int
    cols: int
    col_block: int


def _cast_io(job, n_steps, step_of):
    n_experts = job.src.shape[1]
    per_step = -(-n_experts // n_steps)
    assert n_experts % per_step == 0
    last = n_experts // per_step - 1
    block = lambda *g: jnp.minimum(step_of(*g), last)
    in_spec = pl.BlockSpec((1, per_step, job.rows, job.cols), lambda *g: (job.layer, block(*g), 0, job.col_block))
    out_spec = pl.BlockSpec((per_step, job.rows, job.cols), lambda *g: (block(*g), 0, 0))
    return in_spec, out_spec, jax.ShapeDtypeStruct((n_experts, job.rows, job.cols), BF16)


def _cast_step(src_ref, dst_ref):
    dst_ref[...] = src_ref[0].astype(BF16)


def _a_in_kernel(x_ref, g_ref, w_ref, cast_src_ref, u_ref, cast_dst_ref, xn_ref):
    d = x_ref.shape[1]
    xn_ref[...] = _rms(x_ref[...], g_ref[0]).astype(BF16)
    for c in range(0, d, N_CHUNK):
        a = _dot(xn_ref[...], w_ref[0, :, c:c + N_CHUNK])
        gate = _dot(xn_ref[...], w_ref[0, :, d + c:d + c + N_CHUNK])
        u_ref[:, c:c + N_CHUNK] = (a * _sigmoid(gate)).astype(BF16)
    _cast_step(cast_src_ref, cast_dst_ref)


def _a_in(h, g_all, layer, w_all, j, cast):
    t, d = h.shape
    steps = t // TM_DENSE
    cast_in, cast_out, cast_shape = _cast_io(cast, steps, lambda i: i)
    return pl.pallas_call(
        _a_in_kernel,
        grid=(steps,),
        in_specs=[pl.BlockSpec((TM_DENSE, d), lambda i: (i, 0)),
                  _layer_block(g_all.shape, layer),
                  _layer_block(w_all.shape, j),
                  cast_in],
        out_specs=[pl.BlockSpec((TM_DENSE, d), lambda i: (i, 0)), cast_out],
        out_shape=[jax.ShapeDtypeStruct((t, d), BF16), cast_shape],
        scratch_shapes=[pltpu.VMEM((TM_DENSE, d), BF16)],
        compiler_params=_params(1),
        name="conformer_in",
    )(h, g_all, w_all, cast.src)


def _c_in_kernel(x_ref, g_ref, w_ref, b_ref, cv_ref, xn_ref):
    d = x_ref.shape[1]
    xn_ref[...] = _rms(x_ref[...], g_ref[0]).astype(BF16)
    for c in range(0, d, N_CHUNK):
        b_ref[:, c:c + N_CHUNK] = _dot(xn_ref[...], w_ref[0, :, c:c + N_CHUNK]).astype(BF16)
        cg = _dot(xn_ref[...], w_ref[0, :, d + c:d + c + N_CHUNK])
        v = _dot(xn_ref[...], w_ref[0, :, 2 * d + c:2 * d + c + N_CHUNK])
        cv_ref[:, c:c + N_CHUNK] = (cg * v).astype(BF16)


def _c_in(h, g_all, layer, w_all, j):
    t, d = h.shape
    tile = pl.BlockSpec((TM_DENSE, d), lambda i: (i, 0))
    return pl.pallas_call(
        _c_in_kernel,
        grid=(t // TM_DENSE,),
        in_specs=[tile, _layer_block(g_all.shape, layer), _layer_block(w_all.shape, j)],
        out_specs=[tile, tile],
        out_shape=[jax.ShapeDtypeStruct((t, d), BF16)] * 2,
        scratch_shapes=[pltpu.VMEM((TM_DENSE, d), BF16)],
        compiler_params=_params(1),
        name="shortconv_in",
    )(h, g_all, w_all)


def _fill_ext(ext_ref, cur_ref, prev_ref, next_ref, tm):
    i = pl.program_id(1)
    n = pl.num_programs(1)
    prev = prev_ref[0].astype(F32)
    nxt = next_ref[0].astype(F32)
    ext_ref[0:HALO, :] = jnp.where(i > 0, prev, jnp.zeros_like(prev))
    ext_ref[HALO + tm:HALO + tm + HALO, :] = jnp.where(i < n - 1, nxt, jnp.zeros_like(nxt))

    def body(j, carry):
        r0 = pl.multiple_of(j * ROW_CHUNK, ROW_CHUNK)
        ext_ref[pl.ds(HALO + r0, ROW_CHUNK), :] = cur_ref[0, pl.ds(r0, ROW_CHUNK), :].astype(F32)
        return carry

    lax.fori_loop(0, tm // ROW_CHUNK, body, 0)


def _depthwise(ext_ref, w_ref, cv_ref, width, tm):
    d = cv_ref.shape[1]
    rows = CONV_ROWS
    span = rows + 2 * HALO
    off = HALO - width // 2

    def body(j, carry):
        r0 = pl.multiple_of(j * rows, rows)
        for c in range(0, d, CONV_COLS):
            blk = ext_ref[pl.ds(r0, span), c:c + CONV_COLS]
            acc = jnp.zeros((rows, CONV_COLS), F32)
            for r in range(SUBLANES):
                taps = [k for k in range(width) if (off + k) % SUBLANES == r]
                if not taps:
                    continue
                shifted = blk if r == 0 else pltpu.roll(blk, span - r, axis=0)
                for k in taps:
                    q = (off + k) // SUBLANES * SUBLANES
                    acc = acc + shifted[q:q + rows, :] * w_ref[0, k:k + 1, c:c + CONV_COLS]
            cv_ref[pl.ds(r0, rows), c:c + CONV_COLS] = acc
        return carry

    lax.fori_loop(0, tm // rows, body, 0)


def _project_residual(out_ref, h_ref, v_ref, w_ref):
    d = v_ref.shape[1]
    for c in range(0, d, N_CHUNK):
        out_ref[0, :, c:c + N_CHUNK] = h_ref[0, :, c:c + N_CHUNK] + _dot(v_ref[...], w_ref[0, :, c:c + N_CHUNK])


def _a_conv_kernel(cur_ref, prev_ref, next_ref, h_ref, wc_ref, bc_ref, gl_ref, bl_ref, wo_ref, cast_src_ref,
                   out_ref, cast_dst_ref, ext_ref, cv_ref, v_ref, *, width):
    tm = cur_ref.shape[1]
    _fill_ext(ext_ref, cur_ref, prev_ref, next_ref, tm)
    _depthwise(ext_ref, wc_ref, cv_ref, width, tm)

    def body(j, carry):
        r0 = pl.multiple_of(j * ROW_CHUNK, ROW_CHUNK)
        u = cv_ref[pl.ds(r0, ROW_CHUNK), :] + bc_ref[0]
        mu = jnp.mean(u, axis=-1, keepdims=True)
        uc = u - mu
        var = jnp.mean(uc * uc, axis=-1, keepdims=True)
        y = uc * lax.rsqrt(var + EPS) * gl_ref[0] + bl_ref[0]
        v_ref[pl.ds(r0, ROW_CHUNK), :] = (y * _sigmoid(y)).astype(BF16)
        return carry

    lax.fori_loop(0, tm // ROW_CHUNK, body, 0)
    _project_residual(out_ref, h_ref, v_ref, wo_ref)
    _cast_step(cast_src_ref, cast_dst_ref)


def _c_conv_kernel(cur_ref, prev_ref, next_ref, gate_ref, h_ref, wc_ref, wo_ref, cast_src_ref,
                   out_ref, cast_dst_ref, ext_ref, cv_ref, v_ref, *, width):
    tm = cur_ref.shape[1]
    _fill_ext(ext_ref, cur_ref, prev_ref, next_ref, tm)
    _depthwise(ext_ref, wc_ref, cv_ref, width, tm)

    def body(j, carry):
        r0 = pl.multiple_of(j * ROW_CHUNK, ROW_CHUNK)
        gate = gate_ref[0, pl.ds(r0, ROW_CHUNK), :].astype(F32)
        v_ref[pl.ds(r0, ROW_CHUNK), :] = (gate * cv_ref[pl.ds(r0, ROW_CHUNK), :]).astype(BF16)
        return carry

    lax.fori_loop(0, tm // ROW_CHUNK, body, 0)
    _project_residual(out_ref, h_ref, v_ref, wo_ref)
    _cast_step(cast_src_ref, cast_dst_ref)


def _conv_specs(s, d, tm):
    per = tm // HALO
    last = s // HALO - 1
    cur = pl.BlockSpec((1, tm, d), lambda b, i: (b, i, 0))
    prev = pl.BlockSpec((1, HALO, d), lambda b, i: (b, jnp.maximum(i * per - 1, 0), 0))
    nxt = pl.BlockSpec((1, HALO, d), lambda b, i: (b, jnp.minimum((i + 1) * per, last), 0))
    return cur, prev, nxt


def _conv_scratch(tm, d):
    return [pltpu.VMEM((tm + 2 * HALO, d), F32), pltpu.VMEM((tm, d), F32), pltpu.VMEM((tm, d), BF16)]


def _a_conv(u, h, w_conv, b_conv, g_ln, b_ln, w_out, j, cast):
    bsz, s, d = h.shape
    tm = TM_DENSE
    per = s // tm
    cur, prev, nxt = _conv_specs(s, d, tm)
    cast_in, cast_out, cast_shape = _cast_io(cast, bsz * per, lambda b, i: b * per + i)
    return pl.pallas_call(
        functools.partial(_a_conv_kernel, width=w_conv.shape[1]),
        grid=(bsz, per),
        in_specs=[cur, prev, nxt, cur, _layer_block(w_conv.shape, j), _layer_block(b_conv.shape, j),
                  _layer_block(g_ln.shape, j), _layer_block(b_ln.shape, j), _layer_block(w_out.shape, j), cast_in],
        out_specs=[cur, cast_out],
        out_shape=[jax.ShapeDtypeStruct((bsz, s, d), F32), cast_shape],
        scratch_shapes=_conv_scratch(tm, d),
        compiler_params=_params(2),
        name="conformer_conv_out",
    )(u, u, u, h, w_conv, b_conv, g_ln, b_ln, w_out, cast.src)


def _c_conv(cv, gate, h, w_conv, w_out, j, cast):
    bsz, s, d = h.shape
    tm = TM_DENSE
    per = s // tm
    cur, prev, nxt = _conv_specs(s, d, tm)
    cast_in, cast_out, cast_shape = _cast_io(cast, bsz * per, lambda b, i: b * per + i)
    return pl.pallas_call(
        functools.partial(_c_conv_kernel, width=w_conv.shape[1]),
        grid=(bsz, per),
        in_specs=[cur, prev, nxt, cur, cur, _layer_block(w_conv.shape, j), _layer_block(w_out.shape, j), cast_in],
        out_specs=[cur, cast_out],
        out_shape=[jax.ShapeDtypeStruct((bsz, s, d), F32), cast_shape],
        scratch_shapes=_conv_scratch(tm, d),
        compiler_params=_params(2),
        name="shortconv_conv_out",
    )(cv, cv, cv, gate, h, w_conv, w_out, cast.src)


def _dft_tables(s, group):
    s1 = s // SEQ_MINOR

    def cs(n_rows, n_cols, period):
        m = (np.outer(np.arange(n_rows), np.arange(n_cols)) % period).astype(np.float64)
        ang = 2.0 * np.pi * m / period
        return np.cos(ang), -np.sin(ang)

    c1, i1 = cs(s1, s1, s1)
    w1 = np.concatenate([c1, i1], axis=0)
    twr, twi = cs(s1, SEQ_MINOR, s)
    c2, i2 = cs(SEQ_MINOR, SEQ_MINOR, SEQ_MINOR)
    w2 = np.block([[c2, -i2], [i2, c2]])
    cc, ic = cs(group, group, group)
    scale = 1.0 / math.sqrt(float(s) * float(group))
    return (jnp.asarray(w1, BF16), jnp.asarray(twr, F32), jnp.asarray(twi, F32),
            jnp.asarray(w2, BF16), jnp.asarray(cc * scale, BF16), jnp.asarray(-ic * scale, BF16))


def _b_stage1_kernel(x_ref, g_ref, w1_ref, twr_ref, twi_ref, br_ref, bi_ref, *, nb):
    s1 = x_ref.shape[1]
    d = g_ref.shape[2]
    brs, bis = [], []
    for n in range(nb):
        xn = _rms(x_ref[0, :, n * d:(n + 1) * d], g_ref[0]).astype(BF16)
        y = _dot(w1_ref[...], xn)
        yr, yi = y[:s1], y[s1:]
        tr = twr_ref[0, :, n:n + 1]
        ti = twi_ref[0, :, n:n + 1]
        brs.append(yr * tr - yi * ti)
        bis.append(yr * ti + yi * tr)
    br_ref[0] = jnp.swapaxes(jnp.stack(brs, axis=0), 0, 1).astype(BF16)
    bi_ref[0] = jnp.swapaxes(jnp.stack(bis, axis=0), 0, 1).astype(BF16)


def _b_stage2_kernel(br_ref, bi_ref, h_ref, w2_ref, cc_ref, sc_ref, wo_ref, cast_src_ref, out_ref, cast_dst_ref,
                     ar_ref, ai_ref, f_ref, *, ka, group):
    d = br_ref.shape[2]
    m = SEQ_MINOR
    for n in range(ka):
        cat = jnp.concatenate([br_ref[0, n * m:(n + 1) * m, :], bi_ref[0, n * m:(n + 1) * m, :]], axis=0)
        a = _dot(w2_ref[...], cat)
        ar_ref[n * m:(n + 1) * m, :] = a[:m].astype(BF16)
        ai_ref[n * m:(n + 1) * m, :] = a[m:].astype(BF16)
    for c in range(0, d, group):
        f = _dot(ar_ref[:, c:c + group], cc_ref[...]) + _dot(ai_ref[:, c:c + group], sc_ref[...])
        f_ref[:, c:c + group] = f.astype(BF16)
    for c in range(0, d, N_CHUNK):
        out_ref[0, :, c:c + N_CHUNK] = h_ref[0, :, c:c + N_CHUNK] + _dot(f_ref[...], wo_ref[0, :, c:c + N_CHUNK])
    _cast_step(cast_src_ref, cast_dst_ref)


def _const_block(shape):
    nd = len(shape)
    return pl.BlockSpec(shape, lambda *_: (0,) * nd, pipeline_mode=pl.Buffered(1))


def _fourier_row_orders(bsz, s):
    s1 = s // SEQ_MINOR
    k = np.arange(s)
    base = (np.arange(bsz) * s)[:, None]
    fm_of_nat = (base + ((k % s1) * SEQ_MINOR + k // s1)[None, :]).reshape(-1)
    nat_of_fm = np.argsort(fm_of_nat)
    row1 = ((np.arange(bsz) * s1)[:, None] + (k // SEQ_MINOR)[None, :]).reshape(-1)
    col1 = np.tile(k % SEQ_MINOR, bsz)
    as_i32 = lambda a: jnp.asarray(a, jnp.int32)
    return as_i32(fm_of_nat), as_i32(nat_of_fm), as_i32(row1), as_i32(col1)


def _fourier_layouts_from_natural(h):
    bsz, s, d = h.shape
    s1 = s // SEQ_MINOR
    x1 = h.reshape(bsz * s1, SEQ_MINOR * d)
    h_fm = h.reshape(bsz, SEQ_MINOR, s1, d).transpose(0, 2, 1, 3).reshape(bsz, s, d)
    return x1, h_fm


def _fourier_mixer(x1, h_fm, g_all, layer, w_out, j, cast):
    bsz, s, d = h_fm.shape
    s1 = s // SEQ_MINOR
    group = d // N_FOURIER_GROUPS
    nb = 16
    ka = 4
    w1, twr, twi, w2, cc, sc = _dft_tables(s, group)
    steps1 = SEQ_MINOR // nb
    twr = twr.reshape(s1, steps1, nb).transpose(1, 0, 2)
    twi = twi.reshape(s1, steps1, nb).transpose(1, 0, 2)

    x1 = x1.reshape(bsz, s1, SEQ_MINOR * d)
    blk1 = pl.BlockSpec((1, s1, nb * d), lambda b, i: (b, 0, i))
    tw_spec = pl.BlockSpec((1, s1, nb), lambda b, i: (i, 0, 0))
    br, bi = pl.pallas_call(
        functools.partial(_b_stage1_kernel, nb=nb),
        grid=(bsz, steps1),
        in_specs=[blk1, _layer_block(g_all.shape, layer), _const_block((2 * s1, s1)), tw_spec, tw_spec],
        out_specs=[pl.BlockSpec((1, s1, nb, d), lambda b, i: (b, 0, i, 0))] * 2,
        out_shape=[jax.ShapeDtypeStruct((bsz, s1, SEQ_MINOR, d), BF16)] * 2,
        compiler_params=_params(2),
        name="fourier_stage1",
    )(x1, g_all, w1, twr, twi)

    br = br.reshape(bsz, s, d)
    bi = bi.reshape(bsz, s, d)
    rows = pl.BlockSpec((1, ka * SEQ_MINOR, d), lambda b, i: (b, i, 0))
    per = s1 // ka
    cast_in, cast_out, cast_shape = _cast_io(cast, bsz * per, lambda b, i: b * per + i)
    return pl.pallas_call(
        functools.partial(_b_stage2_kernel, ka=ka, group=group),
        grid=(bsz, per),
        in_specs=[rows, rows, rows, _const_block((2 * SEQ_MINOR, 2 * SEQ_MINOR)),
                  _const_block((group, group)), _const_block((group, group)), _layer_block(w_out.shape, j), cast_in],
        out_specs=[rows, cast_out],
        out_shape=[jax.ShapeDtypeStruct((bsz, s, d), F32), cast_shape],
        scratch_shapes=[pltpu.VMEM((ka * SEQ_MINOR, d), BF16)] * 3,
        compiler_params=_params(2),
        name="fourier_stage2_out",
    )(br, bi, h_fm, w2, cc, sc, w_out, cast.src)


def _kv_kernel(mem_ref, g_ref, wk_ref, wv_ref, k_ref, v_ref):
    memn = _rms(mem_ref[...], g_ref[0]).astype(BF16)
    k_ref[...] = _dot(memn, wk_ref[0].astype(BF16)).astype(BF16)
    v_ref[...] = _dot(memn, wv_ref[0].astype(BF16)).astype(BF16)


def _memory_kv(mem2d, g_all, wk, wv, layer):
    r, d = mem2d.shape
    col = pl.BlockSpec((1, d, N_CHUNK), lambda c: (layer, 0, c))
    out = pl.BlockSpec((r, N_CHUNK), lambda c: (0, c))
    return pl.pallas_call(
        _kv_kernel,
        grid=(d // N_CHUNK,),
        in_specs=[_const_block((r, d)), _layer_block(g_all.shape, layer), col, col],
        out_specs=[out, out],
        out_shape=[jax.ShapeDtypeStruct((r, d), BF16)] * 2,
        compiler_params=_params(1),
        name="memory_kv",
    )(mem2d, g_all, wk, wv)


def _split_bf16(a):
    hi = a.astype(BF16)
    lo = (a - hi.astype(F32)).astype(BF16)
    return hi, lo


def _route(xn, whl, bias):
    xh, xl = _split_bf16(xn)
    both = _dot(xh, whl)
    logits = both[:, :LANES] + both[:, LANES:] + _dot(xl, whl[:, :LANES]) + bias
    lane = lax.broadcasted_iota(jnp.int32, logits.shape, 1)
    neg = jnp.float32(-jnp.inf)
    big = jnp.int32(LANES)

    gl = jnp.where(lane < N_GROUPS, logits, neg)
    gmax = jnp.max(gl, axis=-1, keepdims=True)
    grp = jnp.min(jnp.where(gl == gmax, lane, big), axis=-1, keepdims=True)
    g_w = 1.0 / jnp.sum(jnp.exp(gl - gmax), axis=-1, keepdims=True)

    first = N_GROUPS + grp * EXPERTS_PER_GROUP
    el = jnp.where((lane >= first) & (lane < first + EXPERTS_PER_GROUP), logits, neg)
    m1 = jnp.max(el, axis=-1, keepdims=True)
    i1 = jnp.min(jnp.where(el == m1, lane, big), axis=-1, keepdims=True)
    el2 = jnp.where(lane == i1, neg, el)
    m2 = jnp.max(el2, axis=-1, keepdims=True)
    i2 = jnp.min(jnp.where(el2 == m2, lane, big), axis=-1, keepdims=True)
    e21 = jnp.exp(m2 - m1)
    w1 = g_w / (1.0 + e21)
    w2 = g_w * e21 / (1.0 + e21)

    a1 = i1 - first
    a2 = i2 - first
    lo = jnp.minimum(a1, a2)
    hi = jnp.maximum(a1, a2)
    pair = lo * 3 - jnp.right_shift(lo * (lo - 1), 1) + hi - lo - 1
    bucket = grp * N_PAIRS + pair
    w_lo = jnp.where(a1 < a2, w1, w2)
    w_hi = jnp.where(a1 < a2, w2, w1)
    routed = jnp.where(lane == 0, bucket.astype(F32),
                       jnp.where(lane == 1, w_lo, jnp.where(lane == 2, w_hi, 0.0)))
    return routed, (lane == bucket).astype(F32)


def _xattn_kernel(h_ref, g_ref, wq_ref, k_ref, v_ref, wo_ref, gm_ref, rw_ref, rb_ref, cast_src_ref,
                  out_ref, cnt_ref, cast_dst_ref, xn_ref, o_ref):
    d = h_ref.shape[2]
    hd = d // N_HEADS
    scale = float(hd) ** -0.5
    xn_ref[...] = _rms(h_ref[0], g_ref[0]).astype(BF16)
    for c in range(0, d, hd):
        q = _dot(xn_ref[...], wq_ref[0, :, c:c + hd]).astype(BF16)
        s = lax.dot_general(q, k_ref[0, :, c:c + hd], (((1,), (1,)), ((), ())),
                            preferred_element_type=F32) * scale
        e = jnp.exp(s - jnp.max(s, axis=-1, keepdims=True))
        p = (e / jnp.sum(e, axis=-1, keepdims=True)).astype(BF16)
        o_ref[:, c:c + hd] = _dot(p, v_ref[0, :, c:c + hd]).astype(BF16)
    for c in range(0, d, N_CHUNK):
        out_ref[:, c:c + N_CHUNK] = h_ref[0, :, c:c + N_CHUNK] + _dot(o_ref[...], wo_ref[0, :, c:c + N_CHUNK])

    routed, onehot = _route(_rms(out_ref[:, 0:d], gm_ref[0]), rw_ref[...], rb_ref[...])
    out_ref[:, d:d + LANES] = routed

    @pl.when((pl.program_id(0) == 0) & (pl.program_id(1) == 0))
    def _():
        cnt_ref[...] = jnp.zeros_like(cnt_ref)

    cnt_ref[...] += jnp.sum(onehot, axis=0, keepdims=True)
    _cast_step(cast_src_ref, cast_dst_ref)


def _cross_attention_route(h, g_all, wq, k, v, wo, g_moe_all, w_rg, b_rg, w_re, b_re, layer, cast):
    bsz, s, d = h.shape
    n_mem = k.shape[1]
    tm = TM_DENSE
    per = s // tm
    n = w_rg.shape[1] + w_re.shape[1]
    w = jnp.zeros((d, LANES), F32).at[:, :n].set(jnp.concatenate([w_rg, w_re], axis=1))
    b = jnp.zeros((1, LANES), F32).at[0, :n].set(jnp.concatenate([b_rg, b_re]))
    whl = jnp.concatenate(_split_bf16(w), axis=1)
    cast_in, cast_out, cast_shape = _cast_io(cast, bsz * per, lambda bb, i: bb * per + i)
    tile = pl.BlockSpec((1, tm, d), lambda bb, i: (bb, i, 0))
    kv = pl.BlockSpec((1, n_mem, d), lambda bb, i: (bb, 0, 0))
    return pl.pallas_call(
        _xattn_kernel,
        grid=(bsz, per),
        in_specs=[tile, _layer_block(g_all.shape, layer), _layer_block(wq.shape, layer), kv, kv,
                  _layer_block(wo.shape, layer), _layer_block(g_moe_all.shape, layer),
                  _const_block((d, 2 * LANES)), _const_block((1, LANES)), cast_in],
        out_specs=[pl.BlockSpec((tm, d + LANES), lambda bb, i: (bb * per + i, 0)),
                   pl.BlockSpec((1, LANES), lambda bb, i: (0, 0)), cast_out],
        out_shape=[jax.ShapeDtypeStruct((bsz * s, d + LANES), F32),
                   jax.ShapeDtypeStruct((1, LANES), F32), cast_shape],
        scratch_shapes=[pltpu.VMEM((tm, d), BF16)] * 2,
        compiler_params=_params(2),
        name="cross_attention_route",
    )(h, g_all, wq, k, v, wo, g_moe_all, whl, b, cast.src)


def _wait_rows(count, block_copy, row_copy):
    n8 = pl.multiple_of(jnp.right_shift(count, 3) * SUBLANES, SUBLANES)

    @pl.when(n8 > 0)
    def _():
        block_copy(n8).wait()

    def one(r, carry):
        row_copy().wait()
        return carry

    lax.fori_loop(0, count - n8, one, 0)


def _expert_kernel(order_ref, dest_ref, row2_ref, col2_ref, c0_ref, nv_ref, ea_ref, eb_ref,
                   haug_ref, g_ref, ga_ref, gb_ref, ua_ref, ub_ref, da_ref, db_ref, gf_ref, *rest,
                   final_norm, dual, hosts_cast):
    if hosts_cast:
        cast_src_ref, out_ref, out2_ref, cast_dst_ref, xbuf, obuf, gsem, ssem = rest
        _cast_step(cast_src_ref, cast_dst_ref)
    else:
        out_ref, out2_ref, xbuf, obuf, gsem, ssem = rest
    i = pl.program_id(0)
    n = pl.num_programs(0)
    parity = lax.rem(i, 2)
    tm = xbuf.shape[1]
    d = obuf.shape[2]
    copies_per_row = 2 if dual else 1

    nv = nv_ref[i]
    nxt = jnp.minimum(i + 1, n - 1)
    prev = jnp.maximum(i - 1, 0)
    nv_next = jnp.where(i + 1 < n, nv_ref[nxt], 0)
    nv_prev = jnp.where(i >= 1, nv_ref[prev], 0)
    nv_prev2 = jnp.where(i >= 2, nv_ref[jnp.maximum(i - 2, 0)], 0)

    def gather_row(tok, s, r):
        return pltpu.make_async_copy(haug_ref.at[pl.ds(tok, 1), :], xbuf.at[s, pl.ds(r, 1), :], gsem.at[s])

    def scatter_row(row, s, r):
        return pltpu.make_async_copy(obuf.at[s, pl.ds(r, 1), :], out_ref.at[pl.ds(row, 1), :], ssem.at[s])

    def scatter_row2(row, col, s, r):
        lanes = pl.ds(pl.multiple_of(col * d, d), d)
        return pltpu.make_async_copy(obuf.at[s, pl.ds(r, 1), :], out2_ref.at[pl.ds(row, 1), lanes], ssem.at[s])

    def start_scatter_row(p, s, r):
        prio = r % 2 if isinstance(r, int) else 0
        scatter_row(dest_ref[p], s, r).start(priority=prio)
        if dual:
            scatter_row2(row2_ref[p], col2_ref[p], s, r).start(priority=1 - prio)

    def start_rows(count, base, s, start_row, exact):
        @pl.when(count == tm)
        def _():
            for r in range(tm):
                start_row(base + r, s, r)

        @pl.when((count > 0) & (count < tm))
        def _():
            groups = jnp.right_shift(count if exact else count + (SUBLANES - 1), 3)
            for gi in range(tm // SUBLANES):
                @pl.when(gi < groups)
                def _():
                    for r in range(gi * SUBLANES, (gi + 1) * SUBLANES):
                        start_row(base + r, s, r)
            if exact:
                def one(r, carry):
                    start_row(base + r, s, r)
                    return carry

                lax.fori_loop(groups * SUBLANES, count, one, 0)

    def start_gather(count, base, s):
        start_rows(count, base, s, lambda p, s_, r: gather_row(order_ref[p], s_, r).start(priority=1), exact=False)

    def start_scatter(count, base, s):
        start_rows(count, base, s, start_scatter_row, exact=True)

    def wait_gather(count, s):
        m = pl.multiple_of(jnp.right_shift(count + (SUBLANES - 1), 3) * SUBLANES, SUBLANES)

        @pl.when(m > 0)
        def _():
            pltpu.make_async_copy(haug_ref.at[pl.ds(0, m), :], xbuf.at[s, pl.ds(0, m), :], gsem.at[s]).wait()

    def wait_scatter(count, s):
        for _ in range(copies_per_row):
            _wait_rows(count,
                       lambda m: pltpu.make_async_copy(obuf.at[s, pl.ds(0, m), :], out_ref.at[pl.ds(0, m), :],
                                                       ssem.at[s]),
                       lambda: scatter_row(0, s, 0))

    @pl.when(i == 0)
    def _():
        xbuf[...] = jnp.zeros_like(xbuf)
        start_gather(nv, c0_ref[0], 0)
        if not dual:
            out2_ref[...] = jnp.zeros_like(out2_ref)

    for slot in range(2):
        @pl.when(parity == slot)
        def _():
            start_gather(nv_next, c0_ref[nxt], 1 - slot)
            start_scatter(nv_prev, c0_ref[prev], 1 - slot)
            wait_gather(nv, slot)

    def compute_tile(rows):
        x = xbuf[parity, 0:rows, 0:d]
        xn = _rms(x, g_ref[0]).astype(BF16)
        acc = x
        for lane, (gate_ref, up_ref, dn_ref) in enumerate(((ga_ref, ua_ref, da_ref), (gb_ref, ub_ref, db_ref))):
            gate = _dot(xn, gate_ref[0])
            up = _dot(xn, up_ref[0])
            act = (gate * _sigmoid(gate) * up).astype(BF16)
            acc = acc + _dot(act, dn_ref[0]) * xbuf[parity, 0:rows, d + 1 + lane:d + 2 + lane]
        if final_norm:
            acc = _rms(acc, gf_ref[...])
        wait_scatter(nv_prev2, parity)
        obuf[parity, 0:rows] = acc

    half = tm // 2

    @pl.when(nv > half)
    def _():
        compute_tile(tm)

    @pl.when((nv > 0) & (nv <= half))
    def _():
        compute_tile(half)

    @pl.when(nv == 0)
    def _():
        wait_scatter(nv_prev2, parity)

    @pl.when(i == n - 1)
    def _():
        wait_scatter(nv_prev, 1 - parity)


def _moe_experts(haug, counts, g_all, w_gate, w_up, w_dn, layer, g_final, final_norm,
                 row_of=None, layout2=None, cast=None):
    t, da = haug.shape
    d = da - LANES
    f = w_dn.shape[1]
    tm = TM_MOE
    n_tiles = (t + N_BUCKETS * (tm - 1)) // tm + 1
    dual = layout2 is not None

    bucket = haug[:, d].astype(jnp.int32)
    order = jnp.argsort(bucket, stable=True).astype(jnp.int32)
    pad = lambda a: jnp.concatenate([a, jnp.zeros((tm,), jnp.int32)])
    dest = order if row_of is None else row_of[order]
    row2 = layout2[1][order] if dual else order
    col2 = layout2[2][order] if dual else order
    sizes = counts[0, :N_BUCKETS].astype(jnp.int32)
    tiles_per = (sizes + tm - 1) // tm
    tile_end = jnp.cumsum(tiles_per)
    tile_start = tile_end - tiles_per
    n_used = tile_end[-1]
    start_sorted = jnp.cumsum(sizes) - sizes
    ids = jnp.arange(n_tiles, dtype=jnp.int32)
    tb = jnp.sum((tile_end[None, :] <= jnp.minimum(ids, n_used - 1)[:, None]).astype(jnp.int32), axis=1)
    k = ids - tile_start[tb]
    nv = jnp.where(ids < n_used, jnp.clip(sizes[tb] - k * tm, 0, tm), 0).astype(jnp.int32)
    c0 = jnp.where(ids < n_used, start_sorted[tb] + k * tm, 0).astype(jnp.int32)
    grp = tb // N_PAIRS
    pair = tb % N_PAIRS
    tile_a = (grp * EXPERTS_PER_GROUP + jnp.asarray(PAIR_LO, jnp.int32)[pair]).astype(jnp.int32)
    tile_b = (grp * EXPERTS_PER_GROUP + jnp.asarray(PAIR_HI, jnp.int32)[pair]).astype(jnp.int32)

    any_space = pl.BlockSpec(memory_space=pl.ANY)
    expert_a = lambda i, o, ds, r2, c2, c, v, ea, eb: (ea[i], 0, 0)
    expert_b = lambda i, o, ds, r2, c2, c, v, ea, eb: (eb[i], 0, 0)
    in_specs = [any_space,
                pl.BlockSpec((1, 1, d), lambda i, *_: (layer, 0, 0)),
                pl.BlockSpec((1, d, f), expert_a), pl.BlockSpec((1, d, f), expert_b),
                pl.BlockSpec((1, d, f), expert_a), pl.BlockSpec((1, d, f), expert_b),
                pl.BlockSpec((1, f, d), expert_a), pl.BlockSpec((1, f, d), expert_b),
                pl.BlockSpec((1, d), lambda i, *_: (0, 0))]
    shape2 = (layout2[0], SEQ_MINOR * d) if dual else (SUBLANES, LANES)
    out_specs = [any_space, any_space if dual else pl.BlockSpec((SUBLANES, LANES), lambda i, *_: (0, 0))]
    out_shape = [jax.ShapeDtypeStruct((t, d), F32), jax.ShapeDtypeStruct(shape2, F32)]
    operands = [haug, g_all, w_gate, w_gate, w_up, w_up, w_dn, w_dn, g_final.reshape(1, d)]
    if cast is not None:
        cast_in, cast_out, cast_shape = _cast_io(cast, n_tiles, lambda i, *_: i)
        in_specs.append(cast_in)
        out_specs.append(cast_out)
        out_shape.append(cast_shape)
        operands.append(cast.src)
    grid_spec = pltpu.PrefetchScalarGridSpec(
        num_scalar_prefetch=8,
        grid=(n_tiles,),
        in_specs=in_specs,
        out_specs=out_specs,
        scratch_shapes=[pltpu.VMEM((2, tm, da), F32), pltpu.VMEM((2, tm, d), F32),
                        pltpu.SemaphoreType.DMA((2,)), pltpu.SemaphoreType.DMA((2,))],
    )
    outs = pl.pallas_call(
        functools.partial(_expert_kernel, final_norm=final_norm, dual=dual, hosts_cast=cast is not None),
        grid_spec=grid_spec,
        out_shape=out_shape,
        compiler_params=_params(1),
        name="moe_experts",
    )(pad(order), pad(dest), pad(row2), pad(col2), c0, nv, tile_a, tile_b, *operands)
    return outs[0], (outs[1] if dual else None), (outs[2] if cast is not None else None)


def kernel(x, mem, g_mix, g_xattn, g_mem, g_moe, g_final, w_a_in, w_a_conv, b_a_conv, g_a_ln, b_a_ln, w_a_out, w_b_out, w_c_in, w_c_conv, w_c_out, w_xq, w_xk, w_xv, w_xo, w_route_group, b_route_group, w_route_expert, b_route_expert, w_gate_up, w_down):
    bsz, s, d = x.shape
    t = bsz * s
    depth = g_mix.shape[0]
    mem2d = mem.reshape(bsz * mem.shape[1], d)
    bf = lambda w: w.astype(BF16)
    g_mix, g_xattn, g_mem, g_moe = _rows3(g_mix), _rows3(g_xattn), _rows3(g_mem), _rows3(g_moe)
    b_a_conv, g_a_ln, b_a_ln = _rows3(b_a_conv), _rows3(g_a_ln), _rows3(b_a_ln)
    w_a_in, w_a_out, w_b_out, w_c_in, w_c_out = bf(w_a_in), bf(w_a_out), bf(w_b_out), bf(w_c_in), bf(w_c_out)
    w_xq, w_xo = bf(w_xq), bf(w_xo)

    d_exp = w_down.shape[2]
    gate_job = lambda layer: _CastJob(w_gate_up, layer, d, d_exp, 0)
    up_job = lambda layer: _CastJob(w_gate_up, layer, d, d_exp, 1)
    down_job = lambda layer: _CastJob(w_down, layer, d_exp, d, 0)
    mixer_hosts_two = lambda layer: layer % N_MIXERS == 0

    fm_of_nat, nat_of_fm, row1, col1 = _fourier_row_orders(bsz, s)
    h = x
    fourier_in = None
    w_dn = None
    for i in range(depth):
        kind, j = i % N_MIXERS, i // N_MIXERS
        next_is_fourier = i + 1 < depth and (i + 1) % N_MIXERS == 1
        if kind == 0:
            u, w_gate = _a_in(h.reshape(t, d), g_mix, i, w_a_in, j, gate_job(i))
            h, w_up = _a_conv(u.reshape(bsz, s, d), h, w_a_conv, b_a_conv, g_a_ln, b_a_ln, w_a_out, j, up_job(i))
        elif kind == 1:
            x1, h_fm = fourier_in if fourier_in is not None else _fourier_layouts_from_natural(h)
            h, w_gate = _fourier_mixer(x1, h_fm, g_mix, i, w_b_out, j, gate_job(i))
        else:
            gate, cv = _c_in(h.reshape(t, d), g_mix, i, w_c_in, j)
            h, w_gate = _c_conv(cv.reshape(bsz, s, d), gate.reshape(bsz, s, d), h, w_c_conv, w_c_out, j, gate_job(i))
        k, v = _memory_kv(mem2d, g_mem, w_xk, w_xv, i)
        xattn_job = down_job(i) if mixer_hosts_two(i) else up_job(i)
        haug, counts, w_cast = _cross_attention_route(
            h, g_xattn, w_xq, k.reshape(bsz, -1, d), v.reshape(bsz, -1, d), w_xo, g_moe,
            w_route_group[i], b_route_group[i], w_route_expert[i], b_route_expert[i], i, xattn_job)
        if mixer_hosts_two(i):
            w_dn = w_cast
        else:
            w_up = w_cast
            if w_dn is None:
                w_dn = bf(w_down[i])
        rows_are_fm = kind == 1
        final = i == depth - 1
        next_cast = down_job(i + 1) if (not final and not mixer_hosts_two(i + 1)) else None
        if next_is_fourier:
            row_of = fm_of_nat[nat_of_fm] if rows_are_fm else fm_of_nat
            layout2 = (bsz * (s // SEQ_MINOR),
                       row1[nat_of_fm] if rows_are_fm else row1, col1[nat_of_fm] if rows_are_fm else col1)
        else:
            row_of = nat_of_fm if rows_are_fm else None
            layout2 = None
        out, out2, w_dn = _moe_experts(haug, counts, g_moe, w_gate, w_up, w_dn, i, g_final, final,
                                       row_of, layout2, next_cast)
        if next_is_fourier:
            fourier_in, h = (out2, out.reshape(bsz, s, d)), None
        else:
            fourier_in, h = None, out.reshape(bsz, s, d)
    return h
```

```python
import functools
import math
from typing import NamedTuple

import numpy as np
import jax
import jax.numpy as jnp
from jax import lax
from jax.experimental import pallas as pl
from jax.experimental.pallas import tpu as pltpu

F32 = jnp.float32
BF16 = jnp.bfloat16
EPS = 1e-6

N_MIXERS = 3
N_FOURIER_GROUPS = 8
N_HEADS = 4
N_GROUPS = 8
EXPERTS_PER_GROUP = 4
N_PAIRS = 6
N_BUCKETS = N_GROUPS * N_PAIRS
PAIR_LO = (0, 0, 0, 1, 1, 2)
PAIR_HI = (1, 2, 3, 2, 3, 3)

LANES = 128
SUBLANES = 8
SEQ_MINOR = 128
HALO = 16
VMEM_LIMIT_BYTES = 56 * 1024 * 1024

TM_DENSE = 512
TM_MOE = 256
ROW_CHUNK = 128
N_CHUNK = 256
CONV_ROWS = 128
CONV_COLS = 128


def _params(n_axes):
    return pltpu.CompilerParams(
        dimension_semantics=("arbitrary",) * n_axes,
        vmem_limit_bytes=VMEM_LIMIT_BYTES)


def _layer_block(shape, layer):
    zeros = (0,) * (len(shape) - 1)
    return pl.BlockSpec((1,) + tuple(shape[1:]), lambda *_: (layer,) + zeros, pipeline_mode=pl.Buffered(1))


def _rows3(a):
    return a.reshape(a.shape[0], 1, a.shape[1])


def _rms(x, g):
    return x * lax.rsqrt(jnp.mean(x * x, axis=-1, keepdims=True) + EPS) * g


def _sigmoid(x):
    return 1.0 / (1.0 + jnp.exp(-x))


def _dot(a, b):
    return jnp.dot(a, b, preferred_element_type=F32)


class _CastJob(NamedTuple):
    src: jax.Array
    layer: int
    rows: int
    cols: int
    col_block: int


def _cast_io(job, n_steps, step_of):
    n_experts = job.src.shape[1]
    per_step = -(-n_experts // n_steps)
    assert n_experts % per_step == 0
    last = n_experts // per_step - 1
    block = lambda *g: jnp.minimum(step_of(*g), last)
    in_spec = pl.BlockSpec((1, per_step, job.rows, job.cols), lambda *g: (job.layer, block(*g), 0, job.col_block))
    out_spec = pl.BlockSpec((per_step, job.rows, job.cols), lambda *g: (block(*g), 0, 0))
    return in_spec, out_spec, jax.ShapeDtypeStruct((n_experts, job.rows, job.cols), BF16)


def _cast_step(src_ref, dst_ref):
    dst_ref[...] = src_ref[0].astype(BF16)


def _a_in_kernel(x_ref, g_ref, w_ref, cast_src_ref, u_ref, cast_dst_ref, xn_ref):
    d = x_ref.shape[1]
    xn_ref[...] = _rms(x_ref[...], g_ref[0]).astype(BF16)
    for c in range(0, d, N_CHUNK):
        a = _dot(xn_ref[...], w_ref[0, :, c:c + N_CHUNK])
        gate = _dot(xn_ref[...], w_ref[0, :, d + c:d + c + N_CHUNK])
        u_ref[:, c:c + N_CHUNK] = (a * _sigmoid(gate)).astype(BF16)
    _cast_step(cast_src_ref, cast_dst_ref)


def _a_in(h, g_all, layer, w_all, j, cast):
    t, d = h.shape
    steps = t // TM_DENSE
    cast_in, cast_out, cast_shape = _cast_io(cast, steps, lambda i: i)
    return pl.pallas_call(
        _a_in_kernel,
        grid=(steps,),
        in_specs=[pl.BlockSpec((TM_DENSE, d), lambda i: (i, 0)),
                  _layer_block(g_all.shape, layer),
                  _layer_block(w_all.shape, j),
                  cast_in],
        out_specs=[pl.BlockSpec((TM_DENSE, d), lambda i: (i, 0)), cast_out],
        out_shape=[jax.ShapeDtypeStruct((t, d), BF16), cast_shape],
        scratch_shapes=[pltpu.VMEM((TM_DENSE, d), BF16)],
        compiler_params=_params(1),
        name="conformer_in",
    )(h, g_all, w_all, cast.src)


def _c_in_kernel(x_ref, g_ref, w_ref, b_ref, cv_ref, xn_ref):
    d = x_ref.shape[1]
    xn_ref[...] = _rms(x_ref[...], g_ref[0]).astype(BF16)
    for c in range(0, d, N_CHUNK):
        b_ref[:, c:c + N_CHUNK] = _dot(xn_ref[...], w_ref[0, :, c:c + N_CHUNK]).astype(BF16)
        cg = _dot(xn_ref[...], w_ref[0, :, d + c:d + c + N_CHUNK])
        v = _dot(xn_ref[...], w_ref[0, :, 2 * d + c:2 * d + c + N_CHUNK])
        cv_ref[:, c:c + N_CHUNK] = (cg * v).astype(BF16)


def _c_in(h, g_all, layer, w_all, j):
    t, d = h.shape
    tile = pl.BlockSpec((TM_DENSE, d), lambda i: (i, 0))
    return pl.pallas_call(
        _c_in_kernel,
        grid=(t // TM_DENSE,),
        in_specs=[tile, _layer_block(g_all.shape, layer), _layer_block(w_all.shape, j)],
        out_specs=[tile, tile],
        out_shape=[jax.ShapeDtypeStruct((t, d), BF16)] * 2,
        scratch_shapes=[pltpu.VMEM((TM_DENSE, d), BF16)],
        compiler_params=_params(1),
        name="shortconv_in",
    )(h, g_all, w_all)


def _fill_ext(ext_ref, cur_ref, prev_ref, next_ref, tm):
    i = pl.program_id(1)
    n = pl.num_programs(1)
    prev = prev_ref[0].astype(F32)
    nxt = next_ref[0].astype(F32)
    ext_ref[0:HALO, :] = jnp.where(i > 0, prev, jnp.zeros_like(prev))
    ext_ref[HALO + tm:HALO + tm + HALO, :] = jnp.where(i < n - 1, nxt, jnp.zeros_like(nxt))

    def body(j, carry):
        r0 = pl.multiple_of(j * ROW_CHUNK, ROW_CHUNK)
        ext_ref[pl.ds(HALO + r0, ROW_CHUNK), :] = cur_ref[0, pl.ds(r0, ROW_CHUNK), :].astype(F32)
        return carry

    lax.fori_loop(0, tm // ROW_CHUNK, body, 0)


def _depthwise(ext_ref, w_ref, cv_ref, width, tm):
    d = cv_ref.shape[1]
    rows = CONV_ROWS
    span = rows + 2 * HALO
    off = HALO - width // 2

    def body(j, carry):
        r0 = pl.multiple_of(j * rows, rows)
        for c in range(0, d, CONV_COLS):
            blk = ext_ref[pl.ds(r0, span), c:c + CONV_COLS]
            acc = jnp.zeros((rows, CONV_COLS), F32)
            for r in range(SUBLANES):
                taps = [k for k in range(width) if (off + k) % SUBLANES == r]
                if not taps:
                    continue
                shifted = blk if r == 0 else pltpu.roll(blk, span - r, axis=0)
                for k in taps:
                    q = (off + k) // SUBLANES * SUBLANES
                    acc = acc + shifted[q:q + rows, :] * w_ref[0, k:k + 1, c:c + CONV_COLS]
            cv_ref[pl.ds(r0, rows), c:c + CONV_COLS] = acc
        return carry

    lax.fori_loop(0, tm // rows, body, 0)


def _project_residual(out_ref, h_ref, v_ref, w_ref):
    d = v_ref.shape[1]
    for c in range(0, d, N_CHUNK):
        out_ref[0, :, c:c + N_CHUNK] = h_ref[0, :, c:c + N_CHUNK] + _dot(v_ref[...], w_ref[0, :, c:c + N_CHUNK])


def _a_conv_kernel(cur_ref, prev_ref, next_ref, h_ref, wc_ref, bc_ref, gl_ref, bl_ref, wo_ref, cast_src_ref,
                   out_ref, cast_dst_ref, ext_ref, cv_ref, v_ref, *, width):
    tm = cur_ref.shape[1]
    _fill_ext(ext_ref, cur_ref, prev_ref, next_ref, tm)
    _depthwise(ext_ref, wc_ref, cv_ref, width, tm)

    def body(j, carry):
        r0 = pl.multiple_of(j * ROW_CHUNK, ROW_CHUNK)
        u = cv_ref[pl.ds(r0, ROW_CHUNK), :] + bc_ref[0]
        mu = jnp.mean(u, axis=-1, keepdims=True)
        uc = u - mu
        var = jnp.mean(uc * uc, axis=-1, keepdims=True)
        y = uc * lax.rsqrt(var + EPS) * gl_ref[0] + bl_ref[0]
        v_ref[pl.ds(r0, ROW_CHUNK), :] = (y * _sigmoid(y)).astype(BF16)
        return carry

    lax.fori_loop(0, tm // ROW_CHUNK, body, 0)
    _project_residual(out_ref, h_ref, v_ref, wo_ref)
    _cast_step(cast_src_ref, cast_dst_ref)


def _c_conv_kernel(cur_ref, prev_ref, next_ref, gate_ref, h_ref, wc_ref, wo_ref, cast_src_ref,
                   out_ref, cast_dst_ref, ext_ref, cv_ref, v_ref, *, width):
    tm = cur_ref.shape[1]
    _fill_ext(ext_ref, cur_ref, prev_ref, next_ref, tm)
    _depthwise(ext_ref, wc_ref, cv_ref, width, tm)

    def body(j, carry):
        r0 = pl.multiple_of(j * ROW_CHUNK, ROW_CHUNK)
        gate = gate_ref[0, pl.ds(r0, ROW_CHUNK), :].astype(F32)
        v_ref[pl.ds(r0, ROW_CHUNK), :] = (gate * cv_ref[pl.ds(r0, ROW_CHUNK), :]).astype(BF16)
        return carry

    lax.fori_loop(0, tm // ROW_CHUNK, body, 0)
    _project_residual(out_ref, h_ref, v_ref, wo_ref)
    _cast_step(cast_src_ref, cast_dst_ref)


def _conv_specs(s, d, tm):
    per = tm // HALO
    last = s // HALO - 1
    cur = pl.BlockSpec((1, tm, d), lambda b, i: (b, i, 0))
    prev = pl.BlockSpec((1, HALO, d), lambda b, i: (b, jnp.maximum(i * per - 1, 0), 0))
    nxt = pl.BlockSpec((1, HALO, d), lambda b, i: (b, jnp.minimum((i + 1) * per, last), 0))
    return cur, prev, nxt


def _conv_scratch(tm, d):
    return [pltpu.VMEM((tm + 2 * HALO, d), F32), pltpu.VMEM((tm, d), F32), pltpu.VMEM((tm, d), BF16)]


def _a_conv(u, h, w_conv, b_conv, g_ln, b_ln, w_out, j, cast):
    bsz, s, d = h.shape
    tm = TM_DENSE
    per = s // tm
    cur, prev, nxt = _conv_specs(s, d, tm)
    cast_in, cast_out, cast_shape = _cast_io(cast, bsz * per, lambda b, i: b * per + i)
    return pl.pallas_call(
        functools.partial(_a_conv_kernel, width=w_conv.shape[1]),
        grid=(bsz, per),
        in_specs=[cur, prev, nxt, cur, _layer_block(w_conv.shape, j), _layer_block(b_conv.shape, j),
                  _layer_block(g_ln.shape, j), _layer_block(b_ln.shape, j), _layer_block(w_out.shape, j), cast_in],
        out_specs=[cur, cast_out],
        out_shape=[jax.ShapeDtypeStruct((bsz, s, d), F32), cast_shape],
        scratch_shapes=_conv_scratch(tm, d),
        compiler_params=_params(2),
        name="conformer_conv_out",
    )(u, u, u, h, w_conv, b_conv, g_ln, b_ln, w_out, cast.src)


def _c_conv(cv, gate, h, w_conv, w_out, j, cast):
    bsz, s, d = h.shape
    tm = TM_DENSE
    per = s // tm
    cur, prev, nxt = _conv_specs(s, d, tm)
    cast_in, cast_out, cast_shape = _cast_io(cast, bsz * per, lambda b, i: b * per + i)
    return pl.pallas_call(
        functools.partial(_c_conv_kernel, width=w_conv.shape[1]),
        grid=(bsz, per),
        in_specs=[cur, prev, nxt, cur, cur, _layer_block(w_conv.shape, j), _layer_block(w_out.shape, j), cast_in],
        out_specs=[cur, cast_out],
        out_shape=[jax.ShapeDtypeStruct((bsz, s, d), F32), cast_shape],
        scratch_shapes=_conv_scratch(tm, d),
        compiler_params=_params(2),
        name="shortconv_conv_out",
    )(cv, cv, cv, gate, h, w_conv, w_out, cast.src)


def _dft_tables(s, group):
    s1 = s // SEQ_MINOR

    def cs(n_rows, n_cols, period):
        m = (np.outer(np.arange(n_rows), np.arange(n_cols)) % period).astype(np.float64)
        ang = 2.0 * np.pi * m / period
        return np.cos(ang), -np.sin(ang)

    c1, i1 = cs(s1, s1, s1)
    w1 = np.concatenate([c1, i1], axis=0)
    twr, twi = cs(s1, SEQ_MINOR, s)
    c2, i2 = cs(SEQ_MINOR, SEQ_MINOR, SEQ_MINOR)
    w2 = np.block([[c2, -i2], [i2, c2]])
    cc, ic = cs(group, group, group)
    scale = 1.0 / math.sqrt(float(s) * float(group))
    return (jnp.asarray(w1, BF16), jnp.asarray(twr, F32), jnp.asarray(twi, F32),
            jnp.asarray(w2, BF16), jnp.asarray(cc * scale, BF16), jnp.asarray(-ic * scale, BF16))


def _b_stage1_kernel(x_ref, g_ref, w1_ref, twr_ref, twi_ref, br_ref, bi_ref, *, nb):
    s1 = x_ref.shape[1]
    d = g_ref.shape[2]
    brs, bis = [], []
    for n in range(nb):
        xn = _rms(x_ref[0, :, n * d:(n + 1) * d], g_ref[0]).astype(BF16)
        y = _dot(w1_ref[...], xn)
        yr, yi = y[:s1], y[s1:]
        tr = twr_ref[0, :, n:n + 1]
        ti = twi_ref[0, :, n:n + 1]
        brs.append(yr * tr - yi * ti)
        bis.append(yr * ti + yi * tr)
    br_ref[0] = jnp.swapaxes(jnp.stack(brs, axis=0), 0, 1).astype(BF16)
    bi_ref[0] = jnp.swapaxes(jnp.stack(bis, axis=0), 0, 1).astype(BF16)


def _b_stage2_kernel(br_ref, bi_ref, h_ref, w2_ref, cc_ref, sc_ref, wo_ref, cast_src_ref, out_ref, cast_dst_ref,
                     ar_ref, ai_ref, f_ref, *, ka, group):
    d = br_ref.shape[2]
    m = SEQ_MINOR
    for n in range(ka):
        cat = jnp.concatenate([br_ref[0, n * m:(n + 1) * m, :], bi_ref[0, n * m:(n + 1) * m, :]], axis=0)
        a = _dot(w2_ref[...], cat)
        ar_ref[n * m:(n + 1) * m, :] = a[:m].astype(BF16)
        ai_ref[n * m:(n + 1) * m, :] = a[m:].astype(BF16)
    for c in range(0, d, group):
        f = _dot(ar_ref[:, c:c + group], cc_ref[...]) + _dot(ai_ref[:, c:c + group], sc_ref[...])
        f_ref[:, c:c + group] = f.astype(BF16)
    for c in range(0, d, N_CHUNK):
        out_ref[0, :, c:c + N_CHUNK] = h_ref[0, :, c:c + N_CHUNK] + _dot(f_ref[...], wo_ref[0, :, c:c + N_CHUNK])
    _cast_step(cast_src_ref, cast_dst_ref)


def _const_block(shape):
    nd = len(shape)
    return pl.BlockSpec(shape, lambda *_: (0,) * nd, pipeline_mode=pl.Buffered(1))


def _fourier_row_orders(bsz, s):
    s1 = s // SEQ_MINOR
    k = np.arange(s)
    base = (np.arange(bsz) * s)[:, None]
    fm_of_nat = (base + ((k % s1) * SEQ_MINOR + k // s1)[None, :]).reshape(-1)
    nat_of_fm = np.argsort(fm_of_nat)
    row1 = ((np.arange(bsz) * s1)[:, None] + (k // SEQ_MINOR)[None, :]).reshape(-1)
    col1 = np.tile(k % SEQ_MINOR, bsz)
    as_i32 = lambda a: jnp.asarray(a, jnp.int32)
    return as_i32(fm_of_nat), as_i32(nat_of_fm), as_i32(row1), as_i32(col1)


def _fourier_layouts_from_natural(h):
    bsz, s, d = h.shape
    s1 = s // SEQ_MINOR
    x1 = h.reshape(bsz * s1, SEQ_MINOR * d)
    h_fm = h.reshape(bsz, SEQ_MINOR, s1, d).transpose(0, 2, 1, 3).reshape(bsz, s, d)
    return x1, h_fm


def _fourier_mixer(x1, h_fm, g_all, layer, w_out, j, cast):
    bsz, s, d = h_fm.shape
    s1 = s // SEQ_MINOR
    group = d // N_FOURIER_GROUPS
    nb = 16
    ka = 4
    w1, twr, twi, w2, cc, sc = _dft_tables(s, group)
    steps1 = SEQ_MINOR // nb
    twr = twr.reshape(s1, steps1, nb).transpose(1, 0, 2)
    twi = twi.reshape(s1, steps1, nb).transpose(1, 0, 2)

    x1 = x1.reshape(bsz, s1, SEQ_MINOR * d)
    blk1 = pl.BlockSpec((1, s1, nb * d), lambda b, i: (b, 0, i))
    tw_spec = pl.BlockSpec((1, s1, nb), lambda b, i: (i, 0, 0))
    br, bi = pl.pallas_call(
        functools.partial(_b_stage1_kernel, nb=nb),
        grid=(bsz, steps1),
        in_specs=[blk1, _layer_block(g_all.shape, layer), _const_block((2 * s1, s1)), tw_spec, tw_spec],
        out_specs=[pl.BlockSpec((1, s1, nb, d), lambda b, i: (b, 0, i, 0))] * 2,
        out_shape=[jax.ShapeDtypeStruct((bsz, s1, SEQ_MINOR, d), BF16)] * 2,
        compiler_params=_params(2),
        name="fourier_stage1",
    )(x1, g_all, w1, twr, twi)

    br = br.reshape(bsz, s, d)
    bi = bi.reshape(bsz, s, d)
    rows = pl.BlockSpec((1, ka * SEQ_MINOR, d), lambda b, i: (b, i, 0))
    per = s1 // ka
    cast_in, cast_out, cast_shape = _cast_io(cast, bsz * per, lambda b, i: b * per + i)
    return pl.pallas_call(
        functools.partial(_b_stage2_kernel, ka=ka, group=group),
        grid=(bsz, per),
        in_specs=[rows, rows, rows, _const_block((2 * SEQ_MINOR, 2 * SEQ_MINOR)),
                  _const_block((group, group)), _const_block((group, group)), _layer_block(w_out.shape, j), cast_in],
        out_specs=[rows, cast_out],
        out_shape=[jax.ShapeDtypeStruct((bsz, s, d), F32), cast_shape],
        scratch_shapes=[pltpu.VMEM((ka * SEQ_MINOR, d), BF16)] * 3,
        compiler_params=_params(2),
        name="fourier_stage2_out",
    )(br, bi, h_fm, w2, cc, sc, w_out, cast.src)


def _kv_kernel(mem_ref, g_ref, wk_ref, wv_ref, k_ref, v_ref):
    memn = _rms(mem_ref[...], g_ref[0]).astype(BF16)
    k_ref[...] = _dot(memn, wk_ref[0].astype(BF16)).astype(BF16)
    v_ref[...] = _dot(memn, wv_ref[0].astype(BF16)).astype(BF16)


def _memory_kv(mem2d, g_all, wk, wv, layer):
    r, d = mem2d.shape
    col = pl.BlockSpec((1, d, N_CHUNK), lambda c: (layer, 0, c))
    out = pl.BlockSpec((r, N_CHUNK), lambda c: (0, c))
    return pl.pallas_call(
        _kv_kernel,
        grid=(d // N_CHUNK,),
        in_specs=[_const_block((r, d)), _layer_block(g_all.shape, layer), col, col],
        out_specs=[out, out],
        out_shape=[jax.ShapeDtypeStruct((r, d), BF16)] * 2,
        compiler_params=_params(1),
        name="memory_kv",
    )(mem2d, g_all, wk, wv)


def _split_bf16(a):
    hi = a.astype(BF16)
    lo = (a - hi.astype(F32)).astype(BF16)
    return hi, lo


def _route(xn, whl, bias):
    xh, xl = _split_bf16(xn)
    both = _dot(xh, whl)
    logits = both[:, :LANES] + both[:, LANES:] + _dot(xl, whl[:, :LANES]) + bias
    lane = lax.broadcasted_iota(jnp.int32, logits.shape, 1)
    neg = jnp.float32(-jnp.inf)
    big = jnp.int32(LANES)

    gl = jnp.where(lane < N_GROUPS, logits, neg)
    gmax = jnp.max(gl, axis=-1, keepdims=True)
    grp = jnp.min(jnp.where(gl == gmax, lane, big), axis=-1, keepdims=True)
    g_w = 1.0 / jnp.sum(jnp.exp(gl - gmax), axis=-1, keepdims=True)

    first = N_GROUPS + grp * EXPERTS_PER_GROUP
    el = jnp.where((lane >= first) & (lane < first + EXPERTS_PER_GROUP), logits, neg)
    m1 = jnp.max(el, axis=-1, keepdims=True)
    i1 = jnp.min(jnp.where(el == m1, lane, big), axis=-1, keepdims=True)
    el2 = jnp.where(lane == i1, neg, el)
    m2 = jnp.max(el2, axis=-1, keepdims=True)
    i2 = jnp.min(jnp.where(el2 == m2, lane, big), axis=-1, keepdims=True)
    e21 = jnp.exp(m2 - m1)
    w1 = g_w / (1.0 + e21)
    w2 = g_w * e21 / (1.0 + e21)

    a1 = i1 - first
    a2 = i2 - first
    lo = jnp.minimum(a1, a2)
    hi = jnp.maximum(a1, a2)
    pair = lo * 3 - jnp.right_shift(lo * (lo - 1), 1) + hi - lo - 1
    bucket = grp * N_PAIRS + pair
    w_lo = jnp.where(a1 < a2, w1, w2)
    w_hi = jnp.where(a1 < a2, w2, w1)
    routed = jnp.where(lane == 0, bucket.astype(F32),
                       jnp.where(lane == 1, w_lo, jnp.where(lane == 2, w_hi, 0.0)))
    return routed, (lane == bucket).astype(F32)


def _xattn_kernel(h_ref, g_ref, wq_ref, k_ref, v_ref, wo_ref, gm_ref, rw_ref, rb_ref, cast_src_ref,
                  out_ref, cnt_ref, cast_dst_ref, xn_ref, o_ref):
    d = h_ref.shape[2]
    hd = d // N_HEADS
    scale = float(hd) ** -0.5
    xn_ref[...] = _rms(h_ref[0], g_ref[0]).astype(BF16)
    for c in range(0, d, hd):
        q = _dot(xn_ref[...], wq_ref[0, :, c:c + hd]).astype(BF16)
        s = lax.dot_general(q, k_ref[0, :, c:c + hd], (((1,), (1,)), ((), ())),
                            preferred_element_type=F32) * scale
        e = jnp.exp(s - jnp.max(s, axis=-1, keepdims=True))
        p = (e / jnp.sum(e, axis=-1, keepdims=True)).astype(BF16)
        o_ref[:, c:c + hd] = _dot(p, v_ref[0, :, c:c + hd]).astype(BF16)
    for c in range(0, d, N_CHUNK):
        out_ref[:, c:c + N_CHUNK] = h_ref[0, :, c:c + N_CHUNK] + _dot(o_ref[...], wo_ref[0, :, c:c + N_CHUNK])

    routed, onehot = _route(_rms(out_ref[:, 0:d], gm_ref[0]), rw_ref[...], rb_ref[...])
    out_ref[:, d:d + LANES] = routed

    @pl.when((pl.program_id(0) == 0) & (pl.program_id(1) == 0))
    def _():
        cnt_ref[...] = jnp.zeros_like(cnt_ref)

    cnt_ref[...] += jnp.sum(onehot, axis=0, keepdims=True)
    _cast_step(cast_src_ref, cast_dst_ref)


def _cross_attention_route(h, g_all, wq, k, v, wo, g_moe_all, w_rg, b_rg, w_re, b_re, layer, cast):
    bsz, s, d = h.shape
    n_mem = k.shape[1]
    tm = TM_DENSE
    per = s // tm
    n = w_rg.shape[1] + w_re.shape[1]
    w = jnp.zeros((d, LANES), F32).at[:, :n].set(jnp.concatenate([w_rg, w_re], axis=1))
    b = jnp.zeros((1, LANES), F32).at[0, :n].set(jnp.concatenate([b_rg, b_re]))
    whl = jnp.concatenate(_split_bf16(w), axis=1)
    cast_in, cast_out, cast_shape = _cast_io(cast, bsz * per, lambda bb, i: bb * per + i)
    tile = pl.BlockSpec((1, tm, d), lambda bb, i: (bb, i, 0))
    kv = pl.BlockSpec((1, n_mem, d), lambda bb, i: (bb, 0, 0))
    return pl.pallas_call(
        _xattn_kernel,
        grid=(bsz, per),
        in_specs=[tile, _layer_block(g_all.shape, layer), _layer_block(wq.shape, layer), kv, kv,
                  _layer_block(wo.shape, layer), _layer_block(g_moe_all.shape, layer),
                  _const_block((d, 2 * LANES)), _const_block((1, LANES)), cast_in],
        out_specs=[pl.BlockSpec((tm, d + LANES), lambda bb, i: (bb * per + i, 0)),
                   pl.BlockSpec((1, LANES), lambda bb, i: (0, 0)), cast_out],
        out_shape=[jax.ShapeDtypeStruct((bsz * s, d + LANES), F32),
                   jax.ShapeDtypeStruct((1, LANES), F32), cast_shape],
        scratch_shapes=[pltpu.VMEM((tm, d), BF16)] * 2,
        compiler_params=_params(2),
        name="cross_attention_route",
    )(h, g_all, wq, k, v, wo, g_moe_all, whl, b, cast.src)


def _wait_rows(count, block_copy, row_copy):
    n8 = pl.multiple_of(jnp.right_shift(count, 3) * SUBLANES, SUBLANES)

    @pl.when(n8 > 0)
    def _():
        block_copy(n8).wait()

    def one(r, carry):
        row_copy().wait()
        return carry

    lax.fori_loop(0, count - n8, one, 0)


def _expert_kernel(order_ref, dest_ref, row2_ref, col2_ref, c0_ref, nv_ref, ea_ref, eb_ref,
                   haug_ref, g_ref, ga_ref, gb_ref, ua_ref, ub_ref, da_ref, db_ref, gf_ref, *rest,
                   final_norm, dual, hosts_cast):
    if hosts_cast:
        cast_src_ref, out_ref, out2_ref, cast_dst_ref, xbuf, obuf, gsem, ssem = rest
        _cast_step(cast_src_ref, cast_dst_ref)
    else:
        out_ref, out2_ref, xbuf, obuf, gsem, ssem = rest
    i = pl.program_id(0)
    n = pl.num_programs(0)
    parity = lax.rem(i, 2)
    tm = xbuf.shape[1]
    d = obuf.shape[2]
    copies_per_row = 2 if dual else 1

    nv = nv_ref[i]
    nxt = jnp.minimum(i + 1, n - 1)
    prev = jnp.maximum(i - 1, 0)
    nv_next = jnp.where(i + 1 < n, nv_ref[nxt], 0)
    nv_prev = jnp.where(i >= 1, nv_ref[prev], 0)
    nv_prev2 = jnp.where(i >= 2, nv_ref[jnp.maximum(i - 2, 0)], 0)

    def gather_row(tok, s, r):
        return pltpu.make_async_copy(haug_ref.at[pl.ds(tok, 1), :], xbuf.at[s, pl.ds(r, 1), :], gsem.at[s])

    def scatter_row(row, s, r):
        return pltpu.make_async_copy(obuf.at[s, pl.ds(r, 1), :], out_ref.at[pl.ds(row, 1), :], ssem.at[s])

    def scatter_row2(row, col, s, r):
        lanes = pl.ds(pl.multiple_of(col * d, d), d)
        return pltpu.make_async_copy(obuf.at[s, pl.ds(r, 1), :], out2_ref.at[pl.ds(row, 1), lanes], ssem.at[s])

    def start_scatter_row(p, s, r):
        scatter_row(dest_ref[p], s, r).start()
        if dual:
            scatter_row2(row2_ref[p], col2_ref[p], s, r).start()

    def start_rows(count, base, s, start_row, exact):
        @pl.when(count == tm)
        def _():
            for r in range(tm):
                start_row(base + r, s, r)

        @pl.when((count > 0) & (count < tm))
        def _():
            groups = jnp.right_shift(count if exact else count + (SUBLANES - 1), 3)
            for gi in range(tm // SUBLANES):
                @pl.when(gi < groups)
                def _():
                    for r in range(gi * SUBLANES, (gi + 1) * SUBLANES):
                        start_row(base + r, s, r)
            if exact:
                def one(r, carry):
                    start_row(base + r, s, r)
                    return carry

                lax.fori_loop(groups * SUBLANES, count, one, 0)

    def start_gather(count, base, s):
        start_rows(count, base, s, lambda p, s_, r: gather_row(order_ref[p], s_, r).start(), exact=False)

    def start_scatter(count, base, s):
        start_rows(count, base, s, start_scatter_row, exact=True)

    def wait_gather(count, s):
        m = pl.multiple_of(jnp.right_shift(count + (SUBLANES - 1), 3) * SUBLANES, SUBLANES)

        @pl.when(m > 0)
        def _():
            pltpu.make_async_copy(haug_ref.at[pl.ds(0, m), :], xbuf.at[s, pl.ds(0, m), :], gsem.at[s]).wait()

    def wait_scatter(count, s):
        for _ in range(copies_per_row):
            _wait_rows(count,
                       lambda m: pltpu.make_async_copy(obuf.at[s, pl.ds(0, m), :], out_ref.at[pl.ds(0, m), :],
                                                       ssem.at[s]),
                       lambda: scatter_row(0, s, 0))

    @pl.when(i == 0)
    def _():
        xbuf[...] = jnp.zeros_like(xbuf)
        start_gather(nv, c0_ref[0], 0)
        if not dual:
            out2_ref[...] = jnp.zeros_like(out2_ref)

    for slot in range(2):
        @pl.when(parity == slot)
        def _():
            start_gather(nv_next, c0_ref[nxt], 1 - slot)
            start_scatter(nv_prev, c0_ref[prev], 1 - slot)
            wait_gather(nv, slot)

    def compute_tile(rows):
        x = xbuf[parity, 0:rows, 0:d]
        xn = _rms(x, g_ref[0]).astype(BF16)
        acc = x
        for lane, (gate_ref, up_ref, dn_ref) in enumerate(((ga_ref, ua_ref, da_ref), (gb_ref, ub_ref, db_ref))):
            gate = _dot(xn, gate_ref[0])
            up = _dot(xn, up_ref[0])
            act = (gate * _sigmoid(gate) * up).astype(BF16)
            acc = acc + _dot(act, dn_ref[0]) * xbuf[parity, 0:rows, d + 1 + lane:d + 2 + lane]
        if final_norm:
            acc = _rms(acc, gf_ref[...])
        wait_scatter(nv_prev2, parity)
        obuf[parity, 0:rows] = acc

    half = tm // 2

    @pl.when(nv > half)
    def _():
        compute_tile(tm)

    @pl.when((nv > 0) & (nv <= half))
    def _():
        compute_tile(half)

    @pl.when(nv == 0)
    def _():
        wait_scatter(nv_prev2, parity)

    @pl.when(i == n - 1)
    def _():
        wait_scatter(nv_prev, 1 - parity)


def _moe_experts(haug, counts, g_all, w_gate, w_up, w_dn, layer, g_final, final_norm,
                 row_of=None, layout2=None, cast=None):
    t, da = haug.shape
    d = da - LANES
    f = w_dn.shape[1]
    tm = TM_MOE
    n_tiles = (t + N_BUCKETS * (tm - 1)) // tm + 1
    dual = layout2 is not None

    bucket = haug[:, d].astype(jnp.int32)
    order = jnp.argsort(bucket, stable=True).astype(jnp.int32)
    pad = lambda a: jnp.concatenate([a, jnp.zeros((tm,), jnp.int32)])
    dest = order if row_of is None else row_of[order]
    row2 = layout2[1][order] if dual else order
    col2 = layout2[2][order] if dual else order
    sizes = counts[0, :N_BUCKETS].astype(jnp.int32)
    tiles_per = (sizes + tm - 1) // tm
    tile_end = jnp.cumsum(tiles_per)
    tile_start = tile_end - tiles_per
    n_used = tile_end[-1]
    start_sorted = jnp.cumsum(sizes) - sizes
    ids = jnp.arange(n_tiles, dtype=jnp.int32)
    tb = jnp.sum((tile_end[None, :] <= jnp.minimum(ids, n_used - 1)[:, None]).astype(jnp.int32), axis=1)
    k = ids - tile_start[tb]
    nv = jnp.where(ids < n_used, jnp.clip(sizes[tb] - k * tm, 0, tm), 0).astype(jnp.int32)
    c0 = jnp.where(ids < n_used, start_sorted[tb] + k * tm, 0).astype(jnp.int32)
    grp = tb // N_PAIRS
    pair = tb % N_PAIRS
    tile_a = (grp * EXPERTS_PER_GROUP + jnp.asarray(PAIR_LO, jnp.int32)[pair]).astype(jnp.int32)
    tile_b = (grp * EXPERTS_PER_GROUP + jnp.asarray(PAIR_HI, jnp.int32)[pair]).astype(jnp.int32)

    any_space = pl.BlockSpec(memory_space=pl.ANY)
    expert_a = lambda i, o, ds, r2, c2, c, v, ea, eb: (ea[i], 0, 0)
    expert_b = lambda i, o, ds, r2, c2, c, v, ea, eb: (eb[i], 0, 0)
    in_specs = [any_space,
                pl.BlockSpec((1, 1, d), lambda i, *_: (layer, 0, 0)),
                pl.BlockSpec((1, d, f), expert_a), pl.BlockSpec((1, d, f), expert_b),
                pl.BlockSpec((1, d, f), expert_a), pl.BlockSpec((1, d, f), expert_b),
                pl.BlockSpec((1, f, d), expert_a), pl.BlockSpec((1, f, d), expert_b),
                pl.BlockSpec((1, d), lambda i, *_: (0, 0))]
    shape2 = (layout2[0], SEQ_MINOR * d) if dual else (SUBLANES, LANES)
    out_specs = [any_space, any_space if dual else pl.BlockSpec((SUBLANES, LANES), lambda i, *_: (0, 0))]
    out_shape = [jax.ShapeDtypeStruct((t, d), F32), jax.ShapeDtypeStruct(shape2, F32)]
    operands = [haug, g_all, w_gate, w_gate, w_up, w_up, w_dn, w_dn, g_final.reshape(1, d)]
    if cast is not None:
        cast_in, cast_out, cast_shape = _cast_io(cast, n_tiles, lambda i, *_: i)
        in_specs.append(cast_in)
        out_specs.append(cast_out)
        out_shape.append(cast_shape)
        operands.append(cast.src)
    grid_spec = pltpu.PrefetchScalarGridSpec(
        num_scalar_prefetch=8,
        grid=(n_tiles,),
        in_specs=in_specs,
        out_specs=out_specs,
        scratch_shapes=[pltpu.VMEM((2, tm, da), F32), pltpu.VMEM((2, tm, d), F32),
                        pltpu.SemaphoreType.DMA((2,)), pltpu.SemaphoreType.DMA((2,))],
    )
    outs = pl.pallas_call(
        functools.partial(_expert_kernel, final_norm=final_norm, dual=dual, hosts_cast=cast is not None),
        grid_spec=grid_spec,
        out_shape=out_shape,
        compiler_params=_params(1),
        name="moe_experts",
    )(pad(order), pad(dest), pad(row2), pad(col2), c0, nv, tile_a, tile_b, *operands)
    return outs[0], (outs[1] if dual else None), (outs[2] if cast is not None else None)


def kernel(x, mem, g_mix, g_xattn, g_mem, g_moe, g_final, w_a_in, w_a_conv, b_a_conv, g_a_ln, b_a_ln, w_a_out, w_b_out, w_c_in, w_c_conv, w_c_out, w_xq, w_xk, w_xv, w_xo, w_route_group, b_route_group, w_route_expert, b_route_expert, w_gate_up, w_down):
    bsz, s, d = x.shape
    t = bsz * s
    depth = g_mix.shape[0]
    mem2d = mem.reshape(bsz * mem.shape[1], d)
    bf = lambda w: w.astype(BF16)
    g_mix, g_xattn, g_mem, g_moe = _rows3(g_mix), _rows3(g_xattn), _rows3(g_mem), _rows3(g_moe)
    b_a_conv, g_a_ln, b_a_ln = _rows3(b_a_conv), _rows3(g_a_ln), _rows3(b_a_ln)
    w_a_in, w_a_out, w_b_out, w_c_in, w_c_out = bf(w_a_in), bf(w_a_out), bf(w_b_out), bf(w_c_in), bf(w_c_out)
    w_xq, w_xo = bf(w_xq), bf(w_xo)

    d_exp = w_down.shape[2]
    gate_job = lambda layer: _CastJob(w_gate_up, layer, d, d_exp, 0)
    up_job = lambda layer: _CastJob(w_gate_up, layer, d, d_exp, 1)
    down_job = lambda layer: _CastJob(w_down, layer, d_exp, d, 0)
    mixer_hosts_two = lambda layer: layer % N_MIXERS == 0

    fm_of_nat, nat_of_fm, row1, col1 = _fourier_row_orders(bsz, s)
    h = x
    fourier_in = None
    w_dn = None
    for i in range(depth):
        kind, j = i % N_MIXERS, i // N_MIXERS
        next_is_fourier = i + 1 < depth and (i + 1) % N_MIXERS == 1
        if kind == 0:
            u, w_gate = _a_in(h.reshape(t, d), g_mix, i, w_a_in, j, gate_job(i))
            h, w_up = _a_conv(u.reshape(bsz, s, d), h, w_a_conv, b_a_conv, g_a_ln, b_a_ln, w_a_out, j, up_job(i))
        elif kind == 1:
            x1, h_fm = fourier_in if fourier_in is not None else _fourier_layouts_from_natural(h)
            h, w_gate = _fourier_mixer(x1, h_fm, g_mix, i, w_b_out, j, gate_job(i))
        else:
            gate, cv = _c_in(h.reshape(t, d), g_mix, i, w_c_in, j)
            h, w_gate = _c_conv(cv.reshape(bsz, s, d), gate.reshape(bsz, s, d), h, w_c_conv, w_c_out, j, gate_job(i))
        k, v = _memory_kv(mem2d, g_mem, w_xk, w_xv, i)
        xattn_job = down_job(i) if mixer_hosts_two(i) else up_job(i)
        haug, counts, w_cast = _cross_attention_route(
            h, g_xattn, w_xq, k.reshape(bsz, -1, d), v.reshape(bsz, -1, d), w_xo, g_moe,
            w_route_group[i], b_route_group[i], w_route_expert[i], b_route_expert[i], i, xattn_job)
        if mixer_hosts_two(i):
            w_dn = w_cast
        else:
            w_up = w_cast
            if w_dn is None:
                w_dn = bf(w_down[i])
        rows_are_fm = kind == 1
        final = i == depth - 1
        next_cast = down_job(i + 1) if (not final and not mixer_hosts_two(i + 1)) else None
        if next_is_fourier:
            row_of = fm_of_nat[nat_of_fm] if rows_are_fm else fm_of_nat
            layout2 = (bsz * (s // SEQ_MINOR),
                       row1[nat_of_fm] if rows_are_fm else row1, col1[nat_of_fm] if rows_are_fm else col1)
        else:
            row_of = nat_of_fm if rows_are_fm else None
            layout2 = None
        out, out2, w_dn = _moe_experts(haug, counts, g_moe, w_gate, w_up, w_dn, i, g_final, final,
                                       row_of, layout2, next_cast)
        if next_is_fourier:
            fourier_in, h = (out2, out.reshape(bsz, s, d)), None
        else:
            fourier_in, h = None, out.reshape(bsz, s, d)
    return h
```
